```python
import math
import jax, jax.numpy as jnp
from jax import lax
import numpy as np

D_MODEL = 2048
BATCH = 4
SEQ = 2048
DEPTH = 1

HEAD_DIM = 64
N_ATTN_HEADS = 16
D_ATTN = N_ATTN_HEADS * HEAD_DIM
SSM_GROUP = 16
N_SSM_GROUPS = 64
D_SSM = N_SSM_GROUPS * SSM_GROUP
SSM_STATE = 64
D_MIX = D_ATTN + D_SSM
D_IN_PROJ = 3 * D_ATTN + D_SSM
DILATED_BRANCHES = ((128, 1), (512, 4), (2048, 16))
BLK = 128
N_BUCKETS = 32
MAX_DISTANCE = 2048
PEER_HEADS = 8
PEER_KEYS = 128
PEER_EXPERTS = PEER_KEYS * PEER_KEYS
PEER_QDIM = 256
PEER_TOPK = 16
PEER_TOKEN_BLOCK = 128
EPS = 1e-6
NEG = -1e30

kernel_name = "hymba_s5_dilated_attn_peer_layer"


def rmsnorm(x, g):
    x32 = x.astype(jnp.float32)
    y = x32 * lax.rsqrt(jnp.mean(x32 * x32, axis=-1, keepdims=True) + EPS)
    return (y * g.astype(jnp.float32)).astype(x.dtype)


def t5_bucket(dist):
    max_exact = N_BUCKETS // 2
    n = jnp.maximum(dist, 0)
    nf = jnp.maximum(n, 1).astype(jnp.float32)
    large = max_exact + (jnp.log(nf / max_exact) / math.log(MAX_DISTANCE / max_exact)
                         * (N_BUCKETS - max_exact)).astype(jnp.int32)
    large = jnp.minimum(large, N_BUCKETS - 1)
    return jnp.where(n < max_exact, n, large)


def dilated_branch(q, k, v, rel_bias, window, dil):
    B, S, H, Dh = q.shape
    L = S // dil
    W = window // dil
    Lp = -(-L // BLK) * BLK
    nb = Lp // BLK

    def sub(t, front):
        t = t.reshape(B, L, dil, H, Dh)
        return jnp.pad(t, ((0, 0), (front, Lp - L), (0, 0), (0, 0), (0, 0)))

    qb = sub(q, 0).reshape(B, nb, BLK, dil, H, Dh)
    kp = sub(k, BLK).reshape(B, nb + 1, BLK, dil, H, Dh)
    vp = sub(v, BLK).reshape(B, nb + 1, BLK, dil, H, Dh)
    kb = jnp.concatenate([kp[:, :-1], kp[:, 1:]], axis=2)
    vb = jnp.concatenate([vp[:, :-1], vp[:, 1:]], axis=2)

    qi = jnp.arange(BLK)[:, None]
    kj = jnp.arange(2 * BLK)[None, :]
    rel = qi - kj + BLK
    blk_idx = jnp.arange(nb)[:, None, None]
    valid = (rel >= 0) & (rel <= W) & (blk_idx * BLK - BLK + kj >= 0)
    bias = jnp.transpose(rel_bias.astype(jnp.float32)[t5_bucket(rel * dil)], (2, 0, 1))

    s = jnp.einsum('bnqrhd,bnkrhd->bnrhqk', qb, kb).astype(jnp.float32) / math.sqrt(Dh)
    s = jnp.where(valid[None, :, None, None], s + bias, NEG)
    m = jnp.max(s, axis=-1)
    p = jnp.exp(s - m[..., None])
    l = jnp.sum(p, axis=-1)
    o = jnp.einsum('bnrhqk,bnkrhd->bnrhqd', p, vb.astype(jnp.float32)) / l[..., None]

    o = jnp.transpose(o, (0, 1, 4, 2, 3, 5)).reshape(B, Lp, dil, H, Dh)[:, :L].reshape(B, S, H, Dh)
    m = jnp.transpose(m, (0, 1, 4, 2, 3)).reshape(B, Lp, dil, H)[:, :L].reshape(B, S, H)
    l = jnp.transpose(l, (0, 1, 4, 2, 3)).reshape(B, Lp, dil, H)[:, :L].reshape(B, S, H)
    return o, m, l


def dilated_attention(q, k, v, rel_bias):
    outs, ms, ls = [], [], []
    for window, dil in DILATED_BRANCHES:
        o, m, l = dilated_branch(q, k, v, rel_bias, window, dil)
        outs.append(o)
        ms.append(m)
        ls.append(l)
    m = jnp.stack(ms)
    w = jnp.stack(ls) * jnp.exp(m - jnp.max(m, axis=0, keepdims=True))
    o = jnp.einsum('nbsh,nbshd->bshd', w, jnp.stack(outs)) / jnp.sum(w, axis=0)[..., None]
    return o


def s5_mixer(u, lam_re, lam_im, log_dt, b_re, b_im, c_re, c_im, d_skip, glu_w, glu_b):
    B, S, _ = u.shape
    f32 = jnp.float32
    u32 = u.astype(f32).reshape(B, S, N_SSM_GROUPS, SSM_GROUP)
    lr, li = lam_re.astype(f32), lam_im.astype(f32)
    dt = jnp.exp(log_dt.astype(f32))[:, None]
    mag = jnp.exp(lr * dt)
    a_re, a_im = mag * jnp.cos(li * dt), mag * jnp.sin(li * dt)
    den = lr * lr + li * li
    f_re = ((a_re - 1.0) * lr + a_im * li) / den
    f_im = (a_im * lr - (a_re - 1.0) * li) / den
    br, bi = b_re.astype(f32), b_im.astype(f32)
    bb_re = f_re[..., None] * br - f_im[..., None] * bi
    bb_im = f_re[..., None] * bi + f_im[..., None] * br
    bu_re = jnp.einsum('bsgc,gnc->bsgn', u32, bb_re)
    bu_im = jnp.einsum('bsgc,gnc->bsgn', u32, bb_im)
    A_re = jnp.broadcast_to(a_re, bu_re.shape)
    A_im = jnp.broadcast_to(a_im, bu_im.shape)

    def combine(e1, e2):
        a1r, a1i, b1r, b1i = e1
        a2r, a2i, b2r, b2i = e2
        return (a2r * a1r - a2i * a1i,
                a2r * a1i + a2i * a1r,
                a2r * b1r - a2i * b1i + b2r,
                a2r * b1i + a2i * b1r + b2i)

    _, _, x_re, x_im = lax.associative_scan(combine, (A_re, A_im, bu_re, bu_im), axis=1)
    y = (jnp.einsum('bsgn,gcn->bsgc', x_re, c_re.astype(f32))
         - jnp.einsum('bsgn,gcn->bsgc', x_im, c_im.astype(f32))
         + d_skip.astype(f32) * u32)
    y = jax.nn.gelu(y, approximate=False).reshape(B, S, D_SSM)
    y = y * jax.nn.sigmoid(y @ glu_w.astype(f32) + glu_b.astype(f32))
    return y.astype(u.dtype)


def peer(h, w_q, keys1, keys2, u_tab, v_tab):
    B, S, D = h.shape
    T = B * S
    ht = h.reshape(T, D)
    q = (ht @ w_q).reshape(T, PEER_HEADS, 2, PEER_QDIM // 2)
    s1 = jnp.einsum('thd,hkd->thk', q[:, :, 0], keys1).astype(jnp.float32)
    s2 = jnp.einsum('thd,hkd->thk', q[:, :, 1], keys2).astype(jnp.float32)
    v1, i1 = lax.top_k(s1, PEER_TOPK)
    v2, i2 = lax.top_k(s2, PEER_TOPK)
    cand = (v1[..., :, None] + v2[..., None, :]).reshape(T, PEER_HEADS, PEER_TOPK * PEER_TOPK)
    sc, ci = lax.top_k(cand, PEER_TOPK)
    e1 = jnp.take_along_axis(i1, ci // PEER_TOPK, axis=-1)
    e2 = jnp.take_along_axis(i2, ci % PEER_TOPK, axis=-1)
    experts = e1 * PEER_KEYS + e2
    gates = jax.nn.softmax(sc, axis=-1)

    n_blk = T // PEER_TOKEN_BLOCK
    K = PEER_HEADS * PEER_TOPK

    def block(args):
        hb, eb, gb = args
        a = jnp.einsum('td,tkd->tk', hb, u_tab[eb]).astype(jnp.float32)
        w = gb * jax.nn.gelu(a, approximate=False)
        return jnp.einsum('tk,tkd->td', w, v_tab[eb].astype(jnp.float32))

    out = lax.map(block, (ht.reshape(n_blk, PEER_TOKEN_BLOCK, D),
                          experts.reshape(n_blk, PEER_TOKEN_BLOCK, K),
                          gates.reshape(n_blk, PEER_TOKEN_BLOCK, K)))
    return out.reshape(B, S, D).astype(h.dtype)


def setup_inputs(seed: int = 0) -> dict:
    key = jax.random.key(seed)
    ks = jax.random.split(key, 26)
    f32 = jnp.float32
    nrm = lambda k, shape, scale: jax.random.normal(k, shape, f32) * scale
    gain = lambda k, n: 1.0 + 0.02 * jax.random.normal(k, (n,), f32)
    n_idx = jnp.arange(SSM_STATE, dtype=f32)[None, :]
    return {
        "x": nrm(ks[0], (BATCH, SEQ, D_MODEL), 1.0),
        "norm_mix_g": gain(ks[1], D_MODEL),
        "w_in": nrm(ks[2], (D_MODEL, D_IN_PROJ), D_MODEL ** -0.5),
        "q_norm_g": gain(ks[3], HEAD_DIM),
        "k_norm_g": gain(ks[4], HEAD_DIM),
        "rel_bias": nrm(ks[5], (N_BUCKETS, N_ATTN_HEADS), 0.5),
        "ssm_lambda_re": -0.5 + 0.01 * jax.random.normal(ks[6], (N_SSM_GROUPS, SSM_STATE), f32),
        "ssm_lambda_im": math.pi * n_idx + 0.01 * jax.random.normal(ks[7], (N_SSM_GROUPS, SSM_STATE), f32),
        "ssm_log_dt": jax.random.uniform(ks[8], (N_SSM_GROUPS,), f32, math.log(1e-3), math.log(1e-1)),
        "ssm_b_re": nrm(ks[9], (N_SSM_GROUPS, SSM_STATE, SSM_GROUP), (2 * SSM_GROUP) ** -0.5),
        "ssm_b_im": nrm(ks[10], (N_SSM_GROUPS, SSM_STATE, SSM_GROUP), (2 * SSM_GROUP) ** -0.5),
        "ssm_c_re": nrm(ks[11], (N_SSM_GROUPS, SSM_GROUP, SSM_STATE), (2 * SSM_STATE) ** -0.5),
        "ssm_c_im": nrm(ks[12], (N_SSM_GROUPS, SSM_GROUP, SSM_STATE), (2 * SSM_STATE) ** -0.5),
        "ssm_d": nrm(ks[13], (N_SSM_GROUPS, SSM_GROUP), 1.0),
        "ssm_glu_w": nrm(ks[14], (D_SSM, D_SSM), D_SSM ** -0.5),
        "ssm_glu_b": nrm(ks[15], (D_SSM,), 0.01),
        "attn_out_g": gain(ks[16], D_ATTN),
        "ssm_out_g": gain(ks[17], D_SSM),
        "w_out": nrm(ks[18], (D_MIX, D_MODEL), D_MIX ** -0.5),
        "norm_ffn_g": gain(ks[19], D_MODEL),
        "peer_w_q": nrm(ks[20], (D_MODEL, PEER_HEADS * PEER_QDIM), D_MODEL ** -0.5),
        "peer_keys1": nrm(ks[21], (PEER_HEADS, PEER_KEYS, PEER_QDIM // 2), (PEER_QDIM // 2) ** -0.5),
        "peer_keys2": nrm(ks[22], (PEER_HEADS, PEER_KEYS, PEER_QDIM // 2), (PEER_QDIM // 2) ** -0.5),
        "peer_u": nrm(ks[23], (PEER_EXPERTS, D_MODEL), D_MODEL ** -0.5),
        "peer_v": nrm(ks[24], (PEER_EXPERTS, D_MODEL), 0.5),
    }


def reference(x, norm_mix_g, w_in, q_norm_g, k_norm_g, rel_bias,
              ssm_lambda_re, ssm_lambda_im, ssm_log_dt, ssm_b_re, ssm_b_im,
              ssm_c_re, ssm_c_im, ssm_d, ssm_glu_w, ssm_glu_b,
              attn_out_g, ssm_out_g, w_out, norm_ffn_g,
              peer_w_q, peer_keys1, peer_keys2, peer_u, peer_v):
    B, S, _ = x.shape
    for _layer in range(DEPTH):
        h = rmsnorm(x, norm_mix_g)
        proj = h @ w_in
        q = proj[..., :D_ATTN].reshape(B, S, N_ATTN_HEADS, HEAD_DIM)
        k = proj[..., D_ATTN:2 * D_ATTN].reshape(B, S, N_ATTN_HEADS, HEAD_DIM)
        v = proj[..., 2 * D_ATTN:3 * D_ATTN].reshape(B, S, N_ATTN_HEADS, HEAD_DIM)
        u = proj[..., 3 * D_ATTN:]
        q = rmsnorm(q, q_norm_g)
        k = rmsnorm(k, k_norm_g)
        attn = dilated_attention(q, k, v, rel_bias).reshape(B, S, D_ATTN).astype(x.dtype)
        ssm = s5_mixer(u, ssm_lambda_re, ssm_lambda_im, ssm_log_dt, ssm_b_re, ssm_b_im,
                       ssm_c_re, ssm_c_im, ssm_d, ssm_glu_w, ssm_glu_b)
        mixed = jnp.concatenate([rmsnorm(attn, attn_out_g), rmsnorm(ssm, ssm_out_g)], axis=-1)
        x = x + (mixed @ w_out).astype(x.dtype)
        x = x + peer(rmsnorm(x, norm_ffn_g), peer_w_q, peer_keys1, peer_keys2, peer_u, peer_v)
    return x
```

```python
import functools
import math

import jax
import jax.numpy as jnp
from jax import lax
from jax.experimental import pallas as pl
from jax.experimental.pallas import tpu as pltpu

F32 = jnp.float32
BF16 = jnp.bfloat16

D_MODEL = 2048
HEAD_DIM = 64
N_ATTN_HEADS = 16
D_ATTN = N_ATTN_HEADS * HEAD_DIM
SSM_GROUP = 16
N_SSM_GROUPS = 64
D_SSM = N_SSM_GROUPS * SSM_GROUP
SSM_STATE = 64
D_IN_PROJ = 3 * D_ATTN + D_SSM
DILATED_BRANCHES = ((128, 1), (512, 4), (2048, 16))
BLK = 128
N_BUCKETS = 32
MAX_DISTANCE = 2048
PEER_HEADS = 8
PEER_KEYS = 128
PEER_QDIM = 256
PEER_TOPK = 16
EPS = 1e-6
NEG = -1e30

LANES = 128
SUBLANES = 8
VMEM_LIMIT = 56 * 1024 * 1024

ROW_TILE = 256
SSM_CHUNK = 128
SSM_SLAB_GROUPS = LANES // SSM_GROUP
N_SLABS = N_SSM_GROUPS // SSM_SLAB_GROUPS
SLAB_STATE = SSM_SLAB_GROUPS * SSM_STATE
ROUTE_TILE = 256
PEER_TOKENS = 512
PEER_EXPERTS_STEP = 512
NO_RANK = 99.0


def _params(sem, vmem=VMEM_LIMIT):
    return pltpu.CompilerParams(dimension_semantics=sem, vmem_limit_bytes=vmem)


def _erf_gelu(x):
    return 0.5 * x * (1.0 + lax.erf(x * math.sqrt(0.5)))


def _t5_bucket(dist):
    max_exact = N_BUCKETS // 2
    n = jnp.maximum(dist, 0)
    nf = jnp.maximum(n, 1).astype(F32)
    large = max_exact + (jnp.log(nf / max_exact) / math.log(MAX_DISTANCE / max_exact)
                         * (N_BUCKETS - max_exact)).astype(jnp.int32)
    large = jnp.minimum(large, N_BUCKETS - 1)
    return jnp.where(n < max_exact, n, large)


def _bias_kernel(bkt_ref, rb_ref, out_ref):
    bkt = bkt_ref[0]
    for h in range(N_ATTN_HEADS):
        acc = jnp.zeros((BLK, 2 * BLK), F32)
        for b in range(N_BUCKETS):
            acc = jnp.where(bkt == b, rb_ref[b, h], acc)
        out_ref[0, h] = acc


def _bias_tables(rel_bias):
    qi = jnp.arange(BLK)[:, None]
    kj = jnp.arange(2 * BLK)[None, :]
    rel = qi - kj + BLK
    buckets = jnp.stack([_t5_bucket(rel * dil) for _, dil in DILATED_BRANCHES]).astype(jnp.int32)
    nbr = len(DILATED_BRANCHES)
    return pl.pallas_call(
        _bias_kernel,
        grid=(nbr,),
        in_specs=[pl.BlockSpec((1, BLK, 2 * BLK), lambda i: (i, 0, 0)),
                  pl.BlockSpec(memory_space=pltpu.SMEM)],
        out_specs=pl.BlockSpec((1, N_ATTN_HEADS, BLK, 2 * BLK), lambda i: (i, 0, 0, 0)),
        out_shape=jax.ShapeDtypeStruct((nbr, N_ATTN_HEADS, BLK, 2 * BLK), F32),
        compiler_params=_params(("arbitrary",)),
        name="bias_table",
    )(buckets, rel_bias.astype(F32))


def _head_rmsnorm(z, gain, ones, scale):
    outs = []
    for c in range(z.shape[1] // LANES):
        zc = z[:, LANES * c:LANES * (c + 1)]
        sq = zc * zc
        hi = sq.astype(BF16)
        lo = (sq - hi.astype(F32)).astype(BF16)
        msq = (jnp.dot(hi, ones, preferred_element_type=F32)
               + jnp.dot(lo, ones, preferred_element_type=F32))
        y = zc * lax.rsqrt(msq + EPS)
        outs.append(y * gain[:, LANES * c:LANES * (c + 1)] * scale)
    return jnp.concatenate(outs, axis=1)


def _inproj_kernel(x_ref, g_ref, w_ref, qg_ref, kg_ref, ones_ref, q_ref, k_ref, v_ref, u_ref):
    x = x_ref[...]
    ms = jnp.mean(x * x, axis=-1, keepdims=True)
    h = (x * lax.rsqrt(ms + EPS) * g_ref[...]).astype(BF16)
    proj = jnp.dot(h, w_ref[...], preferred_element_type=F32)
    ones = ones_ref[...]
    q_ref[...] = _head_rmsnorm(proj[:, :D_ATTN], qg_ref[...], ones, 1.0 / math.sqrt(HEAD_DIM))
    k_ref[...] = _head_rmsnorm(proj[:, D_ATTN:2 * D_ATTN], kg_ref[...], ones, 1.0)
    v_ref[...] = proj[:, 2 * D_ATTN:3 * D_ATTN]
    u_ref[...] = proj[:, 3 * D_ATTN:]


def _in_proj(x2, norm_g, w_in, q_g, k_g):
    t = x2.shape[0]
    head_of_lane = jnp.arange(LANES) // HEAD_DIM
    ones = jnp.where(head_of_lane[:, None] == head_of_lane[None, :], 1.0 / HEAD_DIM, 0.0).astype(BF16)
    qg = jnp.tile(q_g.astype(F32), N_ATTN_HEADS)[None, :]
    kg = jnp.tile(k_g.astype(F32), N_ATTN_HEADS)[None, :]
    row = lambda i: (i, 0)
    fixed = lambda i: (0, 0)
    outs = pl.pallas_call(
        _inproj_kernel,
        grid=(t // ROW_TILE,),
        in_specs=[pl.BlockSpec((ROW_TILE, D_MODEL), row),
                  pl.BlockSpec((1, D_MODEL), fixed),
                  pl.BlockSpec((D_MODEL, D_IN_PROJ), fixed),
                  pl.BlockSpec((1, D_ATTN), fixed),
                  pl.BlockSpec((1, D_ATTN), fixed),
                  pl.BlockSpec((LANES, LANES), fixed)],
        out_specs=[pl.BlockSpec((ROW_TILE, D_ATTN), row)] * 3 + [pl.BlockSpec((ROW_TILE, D_SSM), row)],
        out_shape=[jax.ShapeDtypeStruct((t, D_ATTN), F32)] * 3 + [jax.ShapeDtypeStruct((t, D_SSM), F32)],
        compiler_params=_params(("arbitrary",)),
        name="in_proj",
    )(x2, norm_g.astype(F32)[None, :], w_in.astype(BF16), qg, kg, ones)
    return outs


def _rows(start, size, stride):
    return pl.ds(start, size, stride=stride) if stride > 1 else pl.ds(start, size)


def _attn_block(q_ref, k_ref, v_ref, bias_ref, acc_ref, m_ref, l_ref, br, dil, n, r, tri_cur, tri_prev, head0):
    qstart = r + dil * BLK * n
    qrows = _rows(qstart, BLK, dil)
    qb = q_ref[qrows, :].astype(BF16)
    if n == 0:
        krows = qrows
        valid = tri_cur
    else:
        krows = _rows(r + dil * BLK * (n - 1), 2 * BLK, dil)
        valid = jnp.concatenate([tri_prev, tri_cur], axis=1)
    kb = k_ref[krows, :].astype(BF16)
    vb = v_ref[krows, :].astype(BF16)
    pv, mx, den = [], [], []
    for h in range(2):
        qh = jnp.where(head0 if h == 0 else jnp.logical_not(head0), qb, jnp.zeros_like(qb))
        s = lax.dot_general(qh, kb, (((1,), (1,)), ((), ())), preferred_element_type=F32)
        bias = bias_ref[br, h, :, BLK:] if n == 0 else bias_ref[br, h]
        s = jnp.where(valid, s + bias, NEG)
        m = jnp.max(s, axis=-1, keepdims=True)
        p = jnp.exp(s - m)
        den.append(jnp.sum(p, axis=-1, keepdims=True))
        mx.append(m)
        pv.append(jnp.dot(p.astype(BF16), vb, preferred_element_type=F32))
    pv2 = jnp.where(head0, pv[0], pv[1])
    m2 = jnp.where(head0, mx[0], mx[1])
    l2 = jnp.where(head0, den[0], den[1])
    if br == 0:
        acc_ref[qrows, :] = pv2
        m_ref[qrows, :] = m2
        l_ref[qrows, :] = l2
    else:
        m_old = m_ref[qrows, :]
        m_new = jnp.maximum(m_old, m2)
        a = jnp.exp(m_old - m_new)
        b = jnp.exp(m2 - m_new)
        acc_ref[qrows, :] = a * acc_ref[qrows, :] + b * pv2
        l_ref[qrows, :] = a * l_ref[qrows, :] + b * l2
        m_ref[qrows, :] = m_new


def _attn_kernel(q_ref, k_ref, v_ref, bias_ref, o_ref, acc_ref, m_ref, l_ref, *, seq):
    qi = lax.broadcasted_iota(jnp.int32, (BLK, BLK), 0)
    kj = lax.broadcasted_iota(jnp.int32, (BLK, BLK), 1)
    tri_cur = qi >= kj
    tri_prev = kj >= qi
    head0 = lax.broadcasted_iota(jnp.int32, (1, LANES), 1) < HEAD_DIM
    for br, (window, dil) in enumerate(DILATED_BRANCHES):
        assert window // dil == BLK
        for r in range(dil):
            for n in range(seq // dil // BLK):
                _attn_block(q_ref, k_ref, v_ref, bias_ref, acc_ref, m_ref, l_ref,
                            br, dil, n, r, tri_cur, tri_prev, head0)
    o_ref[...] = acc_ref[...] / l_ref[...]


def _attention(q, k, v, bias):
    b, s, _ = q.shape
    blk = pl.BlockSpec((None, s, LANES), lambda i, p: (i, 0, p))
    nbr = len(DILATED_BRANCHES)
    return pl.pallas_call(
        functools.partial(_attn_kernel, seq=s),
        grid=(b, D_ATTN // LANES),
        in_specs=[blk, blk, blk,
                  pl.BlockSpec((nbr, 2, BLK, 2 * BLK), lambda i, p: (0, p, 0, 0))],
        out_specs=blk,
        out_shape=jax.ShapeDtypeStruct((b, s, D_ATTN), F32),
        scratch_shapes=[pltpu.VMEM((s, LANES), F32)] * 3,
        compiler_params=_params(("arbitrary", "arbitrary")),
        name="attention",
    )(q, k, v, bias)


def _zoh_kernel(lr_ref, li_ref, dt_ref, lrr_ref, lir_ref, br_ref, bi_ref,
                are_ref, aim_ref, bbr_ref, bbi_ref):
    dt = jnp.exp(dt_ref[...])

    def zoh(lr, li):
        mag = jnp.exp(lr * dt)
        a_re, a_im = mag * jnp.cos(li * dt), mag * jnp.sin(li * dt)
        den = lr * lr + li * li
        f_re = ((a_re - 1.0) * lr + a_im * li) / den
        f_im = (a_im * lr - (a_re - 1.0) * li) / den
        return a_re, a_im, f_re, f_im

    a_re, a_im, _, _ = zoh(lr_ref[...], li_ref[...])
    are_ref[...] = a_re
    aim_ref[...] = a_im
    _, _, f_re, f_im = zoh(lrr_ref[...], lir_ref[...])
    br, bi = br_ref[...], bi_ref[...]
    bbr_ref[...] = f_re * br - f_im * bi
    bbi_ref[...] = f_re * bi + f_im * br


def _ssm_zoh(lam_re, lam_im, log_dt, b_re, b_im):
    g, n, c = b_re.shape
    rep = lambda a: jnp.repeat(a.astype(F32), c, axis=1)
    a_re, a_im, bb_re, bb_im = pl.pallas_call(
        _zoh_kernel,
        out_shape=[jax.ShapeDtypeStruct((g, n), F32)] * 2 + [jax.ShapeDtypeStruct((g, n * c), F32)] * 2,
        name="ssm_zoh",
    )(lam_re.astype(F32), lam_im.astype(F32), log_dt.astype(F32)[:, None], rep(lam_re), rep(lam_im),
      b_re.astype(F32).reshape(g, n * c), b_im.astype(F32).reshape(g, n * c))
    return a_re, a_im, bb_re.reshape(g, n, c), bb_im.reshape(g, n, c)


def _ssm_kernel(u_ref, wb_ref, wc_ref, are_ref, aim_ref, d_ref, gw_ref, gb_ref, gain_ref,
                o_ref, bu_ref, y_ref, sr_ref, si_ref, *, nb, chunk):
    half = N_SLABS // 2
    seqs = 2 * nb
    lane_blocks = 2 * SLAB_STATE // LANES

    @pl.when(pl.program_id(0) == 0)
    def _():
        sr_ref[...] = jnp.zeros_like(sr_ref)
        si_ref[...] = jnp.zeros_like(si_ref)

    for b in range(nb):
        for m in range(N_SLABS):
            gh, mp = divmod(m, half)
            ub = u_ref[b, :, LANES * m:LANES * (m + 1)].astype(BF16)
            bu = jnp.dot(ub, wb_ref[m], preferred_element_type=F32)
            for j in range(lane_blocks):
                bu_ref[lane_blocks * mp + j, pl.ds(gh * nb + b, chunk, stride=seqs), :] = (
                    bu[:, LANES * j:LANES * (j + 1)])

    def load_state(rows, mp, part):
        j0 = lane_blocks * mp + part * (lane_blocks // 2)
        return jnp.concatenate([bu_ref[j0 + j, rows, :] for j in range(lane_blocks // 2)], axis=1)

    def store_state(rows, mp, part, val):
        j0 = lane_blocks * mp + part * (lane_blocks // 2)
        for j in range(lane_blocks // 2):
            bu_ref[j0 + j, rows, :] = val[:, LANES * j:LANES * (j + 1)]

    def step(t, carry):
        base = pl.multiple_of(t * seqs, seqs)
        rows = pl.ds(base, seqs)
        new = []
        for mp in range(half):
            xr, xi = carry[2 * mp], carry[2 * mp + 1]
            ar = are_ref[:, SLAB_STATE * mp:SLAB_STATE * (mp + 1)]
            ai = aim_ref[:, SLAB_STATE * mp:SLAB_STATE * (mp + 1)]
            nr = ar * xr - ai * xi + load_state(rows, mp, 0)
            ni = ar * xi + ai * xr + load_state(rows, mp, 1)
            store_state(rows, mp, 0, nr)
            store_state(rows, mp, 1, ni)
            new += [nr, ni]
        return tuple(new)

    init = []
    for mp in range(half):
        init += [sr_ref[:, SLAB_STATE * mp:SLAB_STATE * (mp + 1)], si_ref[:, SLAB_STATE * mp:SLAB_STATE * (mp + 1)]]
    final = lax.fori_loop(0, chunk, step, tuple(init))
    for mp in range(half):
        sr_ref[:, SLAB_STATE * mp:SLAB_STATE * (mp + 1)] = final[2 * mp]
        si_ref[:, SLAB_STATE * mp:SLAB_STATE * (mp + 1)] = final[2 * mp + 1]

    for b in range(nb):
        for m in range(N_SLABS):
            gh, mp = divmod(m, half)
            xs = jnp.concatenate([bu_ref[lane_blocks * mp + j, pl.ds(gh * nb + b, chunk, stride=seqs), :]
                                  for j in range(lane_blocks)], axis=1)
            y = jnp.dot(xs.astype(BF16), wc_ref[m], preferred_element_type=F32)
            cols = slice(LANES * m, LANES * (m + 1))
            y_ref[b * chunk:(b + 1) * chunk, cols] = y + d_ref[:, cols] * u_ref[b, :, cols]

    y = _erf_gelu(y_ref[...])
    z = jnp.dot(y.astype(BF16), gw_ref[...], preferred_element_type=F32) + gb_ref[...]
    y = y * jax.nn.sigmoid(z)
    ms = jnp.mean(y * y, axis=-1, keepdims=True)
    yn = y * lax.rsqrt(ms + EPS) * gain_ref[...]
    for b in range(nb):
        o_ref[b] = yn[b * chunk:(b + 1) * chunk].astype(o_ref.dtype)


def _s5_mixer(u, lam_re, lam_im, log_dt, b_re, b_im, c_re, c_im, d_skip, glu_w, glu_b, out_gain):
    nb, s, _ = u.shape
    a_re, a_im, bb_re, bb_im = _ssm_zoh(lam_re, lam_im, log_dt, b_re, b_im)
    eye = jnp.eye(SSM_SLAB_GROUPS, dtype=F32)

    def in_slab(bb):
        w = jnp.einsum('mgnc,gh->mgchn', bb.reshape(N_SLABS, SSM_SLAB_GROUPS, SSM_STATE, SSM_GROUP), eye)
        return w.reshape(N_SLABS, LANES, SLAB_STATE)

    def out_slab(cc):
        w = jnp.einsum('mgcn,gh->mgnhc', cc.reshape(N_SLABS, SSM_SLAB_GROUPS, SSM_GROUP, SSM_STATE), eye)
        return w.reshape(N_SLABS, SLAB_STATE, LANES)

    wb = jnp.concatenate([in_slab(bb_re), in_slab(bb_im)], axis=2).astype(BF16)
    wc = jnp.concatenate([out_slab(c_re.astype(F32)), -out_slab(c_im.astype(F32))], axis=1).astype(BF16)
    half_states = (N_SSM_GROUPS // 2) * SSM_STATE

    def seq_rows(a):
        return jnp.repeat(a.reshape(2, half_states), nb, axis=0)

    fixed2 = lambda c: (0, 0)
    fixed3 = lambda c: (0, 0, 0)
    chunk = SSM_CHUNK
    return pl.pallas_call(
        functools.partial(_ssm_kernel, nb=nb, chunk=chunk),
        grid=(s // chunk,),
        in_specs=[pl.BlockSpec((nb, chunk, D_SSM), lambda c: (0, c, 0)),
                  pl.BlockSpec((N_SLABS, LANES, 2 * SLAB_STATE), fixed3),
                  pl.BlockSpec((N_SLABS, 2 * SLAB_STATE, LANES), fixed3),
                  pl.BlockSpec((2 * nb, half_states), fixed2),
                  pl.BlockSpec((2 * nb, half_states), fixed2),
                  pl.BlockSpec((1, D_SSM), fixed2),
                  pl.BlockSpec((D_SSM, D_SSM), fixed2),
                  pl.BlockSpec((1, D_SSM), fixed2),
                  pl.BlockSpec((1, D_SSM), fixed2)],
        out_specs=pl.BlockSpec((nb, chunk, D_SSM), lambda c: (0, c, 0)),
        out_shape=jax.ShapeDtypeStruct((nb, s, D_SSM), BF16),
        scratch_shapes=[pltpu.VMEM((2 * half_states // LANES, 2 * nb * chunk, LANES), F32),
                        pltpu.VMEM((nb * chunk, D_SSM), F32),
                        pltpu.VMEM((2 * nb, half_states), F32),
                        pltpu.VMEM((2 * nb, half_states), F32)],
        compiler_params=_params(("arbitrary",)),
        name="ssm",
    )(u, wb, wc, seq_rows(a_re), seq_rows(a_im), d_skip.astype(F32).reshape(1, D_SSM),
      glu_w.astype(BF16), glu_b.astype(F32)[None, :], out_gain.astype(F32)[None, :])


def _outproj_kernel(a_ref, s_ref, x_ref, ag_ref, wa_ref, ws_ref, fg_ref, x1_ref, hn_ref):
    a = a_ref[...]
    ms = jnp.mean(a * a, axis=-1, keepdims=True)
    an = (a * lax.rsqrt(ms + EPS) * ag_ref[...]).astype(BF16)
    mixed = (jnp.dot(an, wa_ref[...], preferred_element_type=F32)
             + jnp.dot(s_ref[...], ws_ref[...], preferred_element_type=F32))
    x1 = x_ref[...] + mixed
    x1_ref[...] = x1
    ms1 = jnp.mean(x1 * x1, axis=-1, keepdims=True)
    hn_ref[...] = (x1 * lax.rsqrt(ms1 + EPS) * fg_ref[...]).astype(BF16)


def _out_proj(attn, ssm_n, x2, attn_g, w_out, ffn_g):
    t = x2.shape[0]
    row = lambda i: (i, 0)
    fixed = lambda i: (0, 0)
    w = w_out.astype(BF16)
    return pl.pallas_call(
        _outproj_kernel,
        grid=(t // ROW_TILE,),
        in_specs=[pl.BlockSpec((ROW_TILE, D_ATTN), row),
                  pl.BlockSpec((ROW_TILE, D_SSM), row),
                  pl.BlockSpec((ROW_TILE, D_MODEL), row),
                  pl.BlockSpec((1, D_ATTN), fixed),
                  pl.BlockSpec((D_ATTN, D_MODEL), fixed),
                  pl.BlockSpec((D_SSM, D_MODEL), fixed),
                  pl.BlockSpec((1, D_MODEL), fixed)],
        out_specs=[pl.BlockSpec((ROW_TILE, D_MODEL), row)] * 2,
        out_shape=[jax.ShapeDtypeStruct((t, D_MODEL), F32), jax.ShapeDtypeStruct((t, D_MODEL), BF16)],
        compiler_params=_params(("arbitrary",)),
        name="out_proj",
    )(attn, ssm_n, x2, attn_g.astype(F32)[None, :], w[:D_ATTN], w[D_ATTN:], ffn_g.astype(F32)[None, :])


def _top16(s):
    iota = lax.broadcasted_iota(jnp.int32, s.shape, 0)
    rank = jnp.full(s.shape, NO_RANK, F32)
    vals = []
    for it in range(PEER_TOPK):
        m = jnp.max(s, axis=0, keepdims=True)
        idx = jnp.min(jnp.where(s == m, iota, s.shape[0]), axis=0, keepdims=True)
        hit = iota == idx
        rank = jnp.where(hit, float(it), rank)
        s = jnp.where(hit, -jnp.inf, s)
        vals.append(m)
    return vals, rank


def _stack_rows(rows, n):
    iota = lax.broadcasted_iota(jnp.int32, (n, rows[0].shape[1]), 0)
    out = jnp.zeros((n, rows[0].shape[1]), F32)
    for i, r in enumerate(rows):
        out = jnp.where(iota == i, r, out)
    return out


def _route_head(s1, s2):
    tb = s1.shape[1]
    v1, rank1 = _top16(s1)
    v2, rank2 = _top16(s2)
    v2_all = _stack_rows(v2, PEER_TOPK)
    v1_hi = _stack_rows(v1[SUBLANES:], SUBLANES)
    sub = lax.broadcasted_iota(jnp.int32, (SUBLANES, tb), 0)
    pieces = [v1[0] + v2_all]
    flats = [lax.broadcasted_iota(jnp.int32, (PEER_TOPK, tb), 0)]
    for a in range(1, SUBLANES):
        limit = PEER_TOPK // (a + 1)
        pieces.append(jnp.where(sub < limit, v1[a] + v2_all[:SUBLANES], -jnp.inf))
        flats.append(a * PEER_TOPK + sub)
    pieces.append(v1_hi + v2[0])
    flats.append((sub + SUBLANES) * PEER_TOPK)
    cand0 = jnp.concatenate(pieces, axis=0)
    flat = jnp.concatenate(flats, axis=0)
    cand = cand0
    sel = jnp.zeros(cand.shape, F32)
    for _ in range(PEER_TOPK):
        m = jnp.max(cand, axis=0, keepdims=True)
        idx = jnp.min(jnp.where(cand == m, flat, PEER_TOPK * PEER_TOPK), axis=0, keepdims=True)
        hit = flat == idx
        sel = jnp.where(hit, 1.0, sel)
        cand = jnp.where(hit, -jnp.inf, cand)
    top = v1[0] + v2[0]
    z = jnp.sum(sel * jnp.exp(jnp.where(sel > 0, cand0, top) - top), axis=0, keepdims=True)
    cnt = [jnp.sum(sel[:PEER_TOPK], axis=0, keepdims=True)]
    for a in range(1, SUBLANES):
        lo = PEER_TOPK + SUBLANES * (a - 1)
        cnt.append(jnp.sum(sel[lo:lo + SUBLANES], axis=0, keepdims=True))
    lo = PEER_TOPK + SUBLANES * (SUBLANES - 1)
    for i in range(SUBLANES):
        cnt.append(sel[lo + i:lo + i + 1])
    c1 = jnp.zeros(rank1.shape, F32)
    for a in range(PEER_TOPK):
        c1 = jnp.where(rank1 == float(a), cnt[a], c1)
    e1 = jnp.exp(s1 - v1[0])
    e2n = jnp.exp(s2 - v2[0]) / z
    return c1, e1, rank2, e2n


def _route_kernel(hn_ref, wq_ref, k1_ref, k2_ref, c1_ref, e1_ref, r2_ref, e2_ref):
    qt = lax.dot_general(wq_ref[...], hn_ref[...], (((1,), (1,)), ((), ())), preferred_element_type=F32)
    half = PEER_QDIM // 2
    for h in range(PEER_HEADS):
        q1 = qt[PEER_QDIM * h:PEER_QDIM * h + half].astype(BF16)
        q2 = qt[PEER_QDIM * h + half:PEER_QDIM * (h + 1)].astype(BF16)
        s1 = jnp.dot(k1_ref[h], q1, preferred_element_type=F32)
        s2 = jnp.dot(k2_ref[h], q2, preferred_element_type=F32)
        c1, e1, r2, e2n = _route_head(s1, s2)
        c1_ref[h] = c1
        e1_ref[h] = e1
        r2_ref[h] = r2.astype(r2_ref.dtype)
        e2_ref[h] = e2n.astype(e2_ref.dtype)


def _peer_route(hn, w_q, keys1, keys2):
    t = hn.shape[0]
    blk = pl.BlockSpec((PEER_HEADS, PEER_KEYS, ROUTE_TILE), lambda i: (0, 0, i))
    fixed3 = lambda i: (0, 0, 0)
    shape = (PEER_HEADS, PEER_KEYS, t)
    return pl.pallas_call(
        _route_kernel,
        grid=(t // ROUTE_TILE,),
        in_specs=[pl.BlockSpec((ROUTE_TILE, D_MODEL), lambda i: (i, 0)),
                  pl.BlockSpec((PEER_HEADS * PEER_QDIM, D_MODEL), lambda i: (0, 0)),
                  pl.BlockSpec((PEER_HEADS, PEER_KEYS, PEER_QDIM // 2), fixed3),
                  pl.BlockSpec((PEER_HEADS, PEER_KEYS, PEER_QDIM // 2), fixed3)],
        out_specs=[blk] * 4,
        out_shape=[jax.ShapeDtypeStruct(shape, F32), jax.ShapeDtypeStruct(shape, F32),
                   jax.ShapeDtypeStruct(shape, BF16), jax.ShapeDtypeStruct(shape, BF16)],
        compiler_params=_params(("arbitrary",)),
        name="peer_route",
    )(hn, w_q.T.astype(BF16), keys1.astype(BF16), keys2.astype(BF16))


def _peer_kernel(hn_ref, u_ref, vt_ref, c1_ref, e1_ref, r2_ref, e2_ref, x1_ref, o_ref, acc_ref, g_ref):
    e = pl.program_id(1)
    tiles = PEER_EXPERTS_STEP // PEER_KEYS

    @pl.when(e == 0)
    def _():
        acc_ref[...] = jnp.zeros_like(acc_ref)

    at = lax.dot_general(u_ref[...], hn_ref[...], (((1,), (1,)), ((), ())), preferred_element_type=F32)
    for tl in range(tiles):
        tile = e * tiles + tl
        gate = jnp.zeros((PEER_KEYS, PEER_TOKENS), F32)
        for h in range(PEER_HEADS):
            c1 = c1_ref[h, pl.ds(tile, 1), :]
            e1 = e1_ref[h, pl.ds(tile, 1), :]
            r2 = r2_ref[h].astype(F32)
            e2 = e2_ref[h].astype(F32)
            gate = gate + jnp.where(r2 < c1, e1 * e2, 0.0)
        a = at[PEER_KEYS * tl:PEER_KEYS * (tl + 1)]
        g_ref[PEER_KEYS * tl:PEER_KEYS * (tl + 1), :] = (gate * _erf_gelu(a)).astype(g_ref.dtype)
    acc_ref[...] += jnp.dot(vt_ref[...], g_ref[...], preferred_element_type=F32)

    @pl.when(e == pl.num_programs(1) - 1)
    def _():
        o_ref[...] = x1_ref[...] + acc_ref[...].T


def _peer_mix(hn, x1, u_tab, v_tab, c1, e1, r2, e2n):
    t = hn.shape[0]
    n_exp = u_tab.shape[0]
    route = pl.BlockSpec((PEER_HEADS, PEER_KEYS, PEER_TOKENS), lambda i, e: (0, 0, i))
    tok = lambda i, e: (i, 0)
    return pl.pallas_call(
        _peer_kernel,
        grid=(t // PEER_TOKENS, n_exp // PEER_EXPERTS_STEP),
        in_specs=[pl.BlockSpec((PEER_TOKENS, D_MODEL), tok),
                  pl.BlockSpec((PEER_EXPERTS_STEP, D_MODEL), lambda i, e: (e, 0)),
                  pl.BlockSpec((D_MODEL, PEER_EXPERTS_STEP), lambda i, e: (0, e)),
                  route, route, route, route,
                  pl.BlockSpec((PEER_TOKENS, D_MODEL), tok)],
        out_specs=pl.BlockSpec((PEER_TOKENS, D_MODEL), tok),
        out_shape=jax.ShapeDtypeStruct((t, D_MODEL), F32),
        scratch_shapes=[pltpu.VMEM((D_MODEL, PEER_TOKENS), F32),
                        pltpu.VMEM((PEER_EXPERTS_STEP, PEER_TOKENS), BF16)],
        compiler_params=_params(("arbitrary", "arbitrary")),
        name="peer_mix",
    )(hn, u_tab.astype(BF16), v_tab.T.astype(BF16), c1, e1, r2, e2n, x1)


def kernel(x, norm_mix_g, w_in, q_norm_g, k_norm_g, rel_bias, ssm_lambda_re, ssm_lambda_im, ssm_log_dt,
           ssm_b_re, ssm_b_im, ssm_c_re, ssm_c_im, ssm_d, ssm_glu_w, ssm_glu_b, attn_out_g, ssm_out_g,
           w_out, norm_ffn_g, peer_w_q, peer_keys1, peer_keys2, peer_u, peer_v):
    b, s, d = x.shape
    x2 = x.reshape(b * s, d)
    q, k, v, u = _in_proj(x2, norm_mix_g, w_in, q_norm_g, k_norm_g)
    bias = _bias_tables(rel_bias)
    attn = _attention(q.reshape(b, s, D_ATTN), k.reshape(b, s, D_ATTN), v.reshape(b, s, D_ATTN), bias)
    ssm_n = _s5_mixer(u.reshape(b, s, D_SSM), ssm_lambda_re, ssm_lambda_im, ssm_log_dt, ssm_b_re, ssm_b_im,
                      ssm_c_re, ssm_c_im, ssm_d, ssm_glu_w, ssm_glu_b, ssm_out_g)
    x1, hn = _out_proj(attn.reshape(b * s, D_ATTN), ssm_n.reshape(b * s, D_SSM), x2, attn_out_g, w_out, norm_ffn_g)
    c1, e1, r2, e2n = _peer_route(hn, peer_w_q, peer_keys1, peer_keys2)
    out = _peer_mix(hn, x1, peer_u, peer_v, c1, e1, r2, e2n)
    return out.reshape(b, s, d).astype(x.dtype)
```

```python
import functools
import math

import jax
import jax.numpy as jnp
from jax import lax
from jax.experimental import pallas as pl
from jax.experimental.pallas import tpu as pltpu

F32 = jnp.float32
BF16 = jnp.bfloat16

D_MODEL = 2048
HEAD_DIM = 64
N_ATTN_HEADS = 16
D_ATTN = N_ATTN_HEADS * HEAD_DIM
SSM_GROUP = 16
N_SSM_GROUPS = 64
D_SSM = N_SSM_GROUPS * SSM_GROUP
SSM_STATE = 64
D_IN_PROJ = 3 * D_ATTN + D_SSM
DILATED_BRANCHES = ((128, 1), (512, 4), (2048, 16))
BLK = 128
N_BUCKETS = 32
MAX_DISTANCE = 2048
PEER_HEADS = 8
PEER_KEYS = 128
PEER_QDIM = 256
PEER_TOPK = 16
EPS = 1e-6
NEG = -1e30

LANES = 128
SUBLANES = 8
VMEM_LIMIT = 56 * 1024 * 1024

ROW_TILE = 256
SSM_CHUNK = 128
SSM_SLAB_GROUPS = LANES // SSM_GROUP
N_SLABS = N_SSM_GROUPS // SSM_SLAB_GROUPS
SLAB_STATE = SSM_SLAB_GROUPS * SSM_STATE
ROUTE_TILE = 256
PEER_TOKENS = 512
PEER_EXPERTS_STEP = 512
PEER_SUB = 256
NO_RANK = 99.0


def _params(sem, vmem=VMEM_LIMIT):
    return pltpu.CompilerParams(dimension_semantics=sem, vmem_limit_bytes=vmem)


def _erf_gelu(x):
    return 0.5 * x * (1.0 + lax.erf(x * math.sqrt(0.5)))


def _t5_bucket(dist):
    max_exact = N_BUCKETS // 2
    n = jnp.maximum(dist, 0)
    nf = jnp.maximum(n, 1).astype(F32)
    large = max_exact + (jnp.log(nf / max_exact) / math.log(MAX_DISTANCE / max_exact)
                         * (N_BUCKETS - max_exact)).astype(jnp.int32)
    large = jnp.minimum(large, N_BUCKETS - 1)
    return jnp.where(n < max_exact, n, large)


def _bias_kernel(bkt_ref, rb_ref, out_ref):
    bkt = bkt_ref[0]
    for h in range(N_ATTN_HEADS):
        acc = jnp.zeros((BLK, 2 * BLK), F32)
        for b in range(N_BUCKETS):
            acc = jnp.where(bkt == b, rb_ref[b, h], acc)
        out_ref[0, h] = acc


def _bias_tables(rel_bias):
    qi = jnp.arange(BLK)[:, None]
    kj = jnp.arange(2 * BLK)[None, :]
    rel = qi - kj + BLK
    buckets = jnp.stack([_t5_bucket(rel * dil) for _, dil in DILATED_BRANCHES]).astype(jnp.int32)
    nbr = len(DILATED_BRANCHES)
    return pl.pallas_call(
        _bias_kernel,
        grid=(nbr,),
        in_specs=[pl.BlockSpec((1, BLK, 2 * BLK), lambda i: (i, 0, 0)),
                  pl.BlockSpec(memory_space=pltpu.SMEM)],
        out_specs=pl.BlockSpec((1, N_ATTN_HEADS, BLK, 2 * BLK), lambda i: (i, 0, 0, 0)),
        out_shape=jax.ShapeDtypeStruct((nbr, N_ATTN_HEADS, BLK, 2 * BLK), F32),
        compiler_params=_params(("arbitrary",)),
        name="bias_table",
    )(buckets, rel_bias.astype(F32))


def _head_rmsnorm(z, gain, ones, scale):
    outs = []
    for c in range(z.shape[1] // LANES):
        zc = z[:, LANES * c:LANES * (c + 1)]
        sq = zc * zc
        hi = sq.astype(BF16)
        lo = (sq - hi.astype(F32)).astype(BF16)
        msq = (jnp.dot(hi, ones, preferred_element_type=F32)
               + jnp.dot(lo, ones, preferred_element_type=F32))
        y = zc * lax.rsqrt(msq + EPS)
        outs.append(y * gain[:, LANES * c:LANES * (c + 1)] * scale)
    return jnp.concatenate(outs, axis=1)


def _inproj_kernel(x_ref, g_ref, w_ref, qg_ref, kg_ref, ones_ref, q_ref, k_ref, v_ref, u_ref):
    x = x_ref[...]
    ms = jnp.mean(x * x, axis=-1, keepdims=True)
    h = (x * lax.rsqrt(ms + EPS) * g_ref[...]).astype(BF16)
    proj = jnp.dot(h, w_ref[...], preferred_element_type=F32)
    ones = ones_ref[...]
    q_ref[...] = _head_rmsnorm(proj[:, :D_ATTN], qg_ref[...], ones, 1.0 / math.sqrt(HEAD_DIM))
    k_ref[...] = _head_rmsnorm(proj[:, D_ATTN:2 * D_ATTN], kg_ref[...], ones, 1.0)
    v_ref[...] = proj[:, 2 * D_ATTN:3 * D_ATTN]
    u_ref[...] = proj[:, 3 * D_ATTN:]


def _in_proj(x2, norm_g, w_in, q_g, k_g):
    t = x2.shape[0]
    head_of_lane = jnp.arange(LANES) // HEAD_DIM
    ones = jnp.where(head_of_lane[:, None] == head_of_lane[None, :], 1.0 / HEAD_DIM, 0.0).astype(BF16)
    qg = jnp.tile(q_g.astype(F32), N_ATTN_HEADS)[None, :]
    kg = jnp.tile(k_g.astype(F32), N_ATTN_HEADS)[None, :]
    row = lambda i: (i, 0)
    fixed = lambda i: (0, 0)
    outs = pl.pallas_call(
        _inproj_kernel,
        grid=(t // ROW_TILE,),
        in_specs=[pl.BlockSpec((ROW_TILE, D_MODEL), row),
                  pl.BlockSpec((1, D_MODEL), fixed),
                  pl.BlockSpec((D_MODEL, D_IN_PROJ), fixed),
                  pl.BlockSpec((1, D_ATTN), fixed),
                  pl.BlockSpec((1, D_ATTN), fixed),
                  pl.BlockSpec((LANES, LANES), fixed)],
        out_specs=[pl.BlockSpec((ROW_TILE, D_ATTN), row)] * 3 + [pl.BlockSpec((ROW_TILE, D_SSM), row)],
        out_shape=[jax.ShapeDtypeStruct((t, D_ATTN), F32)] * 3 + [jax.ShapeDtypeStruct((t, D_SSM), F32)],
        compiler_params=_params(("arbitrary",)),
        name="in_proj",
    )(x2, norm_g.astype(F32)[None, :], w_in.astype(BF16), qg, kg, ones)
    return outs


def _rows(start, size, stride):
    return pl.ds(start, size, stride=stride) if stride > 1 else pl.ds(start, size)


def _attn_block(q_ref, k_ref, v_ref, bias_ref, acc_ref, m_ref, l_ref, br, dil, n, r, tri_cur, tri_prev, head0):
    qstart = r + dil * BLK * n
    qrows = _rows(qstart, BLK, dil)
    qb = q_ref[qrows, :].astype(BF16)
    if n == 0:
        krows = qrows
        valid = tri_cur
    else:
        krows = _rows(r + dil * BLK * (n - 1), 2 * BLK, dil)
        valid = jnp.concatenate([tri_prev, tri_cur], axis=1)
    kb = k_ref[krows, :].astype(BF16)
    vb = v_ref[krows, :].astype(BF16)
    pv, mx, den = [], [], []
    for h in range(2):
        qh = jnp.where(head0 if h == 0 else jnp.logical_not(head0), qb, jnp.zeros_like(qb))
        s = lax.dot_general(qh, kb, (((1,), (1,)), ((), ())), preferred_element_type=F32)
        bias = bias_ref[br, h, :, BLK:] if n == 0 else bias_ref[br, h]
        s = jnp.where(valid, s + bias, NEG)
        m = jnp.max(s, axis=-1, keepdims=True)
        p = jnp.exp(s - m)
        den.append(jnp.sum(p, axis=-1, keepdims=True))
        mx.append(m)
        pv.append(jnp.dot(p.astype(BF16), vb, preferred_element_type=F32))
    pv2 = jnp.where(head0, pv[0], pv[1])
    m2 = jnp.where(head0, mx[0], mx[1])
    l2 = jnp.where(head0, den[0], den[1])
    if br == 0:
        acc_ref[qrows, :] = pv2
        m_ref[qrows, :] = m2
        l_ref[qrows, :] = l2
    else:
        m_old = m_ref[qrows, :]
        m_new = jnp.maximum(m_old, m2)
        a = jnp.exp(m_old - m_new)
        b = jnp.exp(m2 - m_new)
        acc_ref[qrows, :] = a * acc_ref[qrows, :] + b * pv2
        l_ref[qrows, :] = a * l_ref[qrows, :] + b * l2
        m_ref[qrows, :] = m_new


def _attn_kernel(q_ref, k_ref, v_ref, bias_ref, o_ref, acc_ref, m_ref, l_ref, *, seq):
    qi = lax.broadcasted_iota(jnp.int32, (BLK, BLK), 0)
    kj = lax.broadcasted_iota(jnp.int32, (BLK, BLK), 1)
    tri_cur = qi >= kj
    tri_prev = kj >= qi
    head0 = lax.broadcasted_iota(jnp.int32, (1, LANES), 1) < HEAD_DIM
    for br, (window, dil) in enumerate(DILATED_BRANCHES):
        assert window // dil == BLK
        for r in range(dil):
            for n in range(seq // dil // BLK):
                _attn_block(q_ref, k_ref, v_ref, bias_ref, acc_ref, m_ref, l_ref,
                            br, dil, n, r, tri_cur, tri_prev, head0)
    o_ref[...] = acc_ref[...] / l_ref[...]


def _attention(q, k, v, bias):
    b, s, _ = q.shape
    blk = pl.BlockSpec((None, s, LANES), lambda i, p: (i, 0, p))
    nbr = len(DILATED_BRANCHES)
    return pl.pallas_call(
        functools.partial(_attn_kernel, seq=s),
        grid=(b, D_ATTN // LANES),
        in_specs=[blk, blk, blk,
                  pl.BlockSpec((nbr, 2, BLK, 2 * BLK), lambda i, p: (0, p, 0, 0))],
        out_specs=blk,
        out_shape=jax.ShapeDtypeStruct((b, s, D_ATTN), F32),
        scratch_shapes=[pltpu.VMEM((s, LANES), F32)] * 3,
        compiler_params=_params(("arbitrary", "arbitrary")),
        name="attention",
    )(q, k, v, bias)


def _zoh_kernel(lr_ref, li_ref, dt_ref, lrr_ref, lir_ref, br_ref, bi_ref,
                are_ref, aim_ref, bbr_ref, bbi_ref):
    dt = jnp.exp(dt_ref[...])

    def zoh(lr, li):
        mag = jnp.exp(lr * dt)
        a_re, a_im = mag * jnp.cos(li * dt), mag * jnp.sin(li * dt)
        den = lr * lr + li * li
        f_re = ((a_re - 1.0) * lr + a_im * li) / den
        f_im = (a_im * lr - (a_re - 1.0) * li) / den
        return a_re, a_im, f_re, f_im

    a_re, a_im, _, _ = zoh(lr_ref[...], li_ref[...])
    are_ref[...] = a_re
    aim_ref[...] = a_im
    _, _, f_re, f_im = zoh(lrr_ref[...], lir_ref[...])
    br, bi = br_ref[...], bi_ref[...]
    bbr_ref[...] = f_re * br - f_im * bi
    bbi_ref[...] = f_re * bi + f_im * br


def _ssm_zoh(lam_re, lam_im, log_dt, b_re, b_im):
    g, n, c = b_re.shape
    rep = lambda a: jnp.repeat(a.astype(F32), c, axis=1)
    a_re, a_im, bb_re, bb_im = pl.pallas_call(
        _zoh_kernel,
        out_shape=[jax.ShapeDtypeStruct((g, n), F32)] * 2 + [jax.ShapeDtypeStruct((g, n * c), F32)] * 2,
        name="ssm_zoh",
    )(lam_re.astype(F32), lam_im.astype(F32), log_dt.astype(F32)[:, None], rep(lam_re), rep(lam_im),
      b_re.astype(F32).reshape(g, n * c), b_im.astype(F32).reshape(g, n * c))
    return a_re, a_im, bb_re.reshape(g, n, c), bb_im.reshape(g, n, c)


def _ssm_kernel(u_ref, wb_ref, wc_ref, are_ref, aim_ref, d_ref, gw_ref, gb_ref, gain_ref,
                o_ref, bu_ref, y_ref, sr_ref, si_ref, *, nb, chunk):
    half = N_SLABS // 2
    seqs = 2 * nb
    lane_blocks = 2 * SLAB_STATE // LANES

    @pl.when(pl.program_id(0) == 0)
    def _():
        sr_ref[...] = jnp.zeros_like(sr_ref)
        si_ref[...] = jnp.zeros_like(si_ref)

    for b in range(nb):
        for m in range(N_SLABS):
            gh, mp = divmod(m, half)
            ub = u_ref[b, :, LANES * m:LANES * (m + 1)].astype(BF16)
            bu = jnp.dot(ub, wb_ref[m], preferred_element_type=F32)
            for j in range(lane_blocks):
                bu_ref[lane_blocks * mp + j, pl.ds(gh * nb + b, chunk, stride=seqs), :] = (
                    bu[:, LANES * j:LANES * (j + 1)])

    def load_state(rows, mp, part):
        j0 = lane_blocks * mp + part * (lane_blocks // 2)
        return jnp.concatenate([bu_ref[j0 + j, rows, :] for j in range(lane_blocks // 2)], axis=1)

    def store_state(rows, mp, part, val):
        j0 = lane_blocks * mp + part * (lane_blocks // 2)
        for j in range(lane_blocks // 2):
            bu_ref[j0 + j, rows, :] = val[:, LANES * j:LANES * (j + 1)]

    def step(t, carry):
        base = pl.multiple_of(t * seqs, seqs)
        rows = pl.ds(base, seqs)
        new = []
        for mp in range(half):
            xr, xi = carry[2 * mp], carry[2 * mp + 1]
            ar = are_ref[:, SLAB_STATE * mp:SLAB_STATE * (mp + 1)]
            ai = aim_ref[:, SLAB_STATE * mp:SLAB_STATE * (mp + 1)]
            nr = ar * xr - ai * xi + load_state(rows, mp, 0)
            ni = ar * xi + ai * xr + load_state(rows, mp, 1)
            store_state(rows, mp, 0, nr)
            store_state(rows, mp, 1, ni)
            new += [nr, ni]
        return tuple(new)

    init = []
    for mp in range(half):
        init += [sr_ref[:, SLAB_STATE * mp:SLAB_STATE * (mp + 1)], si_ref[:, SLAB_STATE * mp:SLAB_STATE * (mp + 1)]]
    final = lax.fori_loop(0, chunk, step, tuple(init))
    for mp in range(half):
        sr_ref[:, SLAB_STATE * mp:SLAB_STATE * (mp + 1)] = final[2 * mp]
        si_ref[:, SLAB_STATE * mp:SLAB_STATE * (mp + 1)] = final[2 * mp + 1]

    for b in range(nb):
        for m in range(N_SLABS):
            gh, mp = divmod(m, half)
            xs = jnp.concatenate([bu_ref[lane_blocks * mp + j, pl.ds(gh * nb + b, chunk, stride=seqs), :]
                                  for j in range(lane_blocks)], axis=1)
            y = jnp.dot(xs.astype(BF16), wc_ref[m], preferred_element_type=F32)
            cols = slice(LANES * m, LANES * (m + 1))
            y_ref[b * chunk:(b + 1) * chunk, cols] = y + d_ref[:, cols] * u_ref[b, :, cols]

    y = _erf_gelu(y_ref[...])
    z = jnp.dot(y.astype(BF16), gw_ref[...], preferred_element_type=F32) + gb_ref[...]
    y = y * jax.nn.sigmoid(z)
    ms = jnp.mean(y * y, axis=-1, keepdims=True)
    yn = y * lax.rsqrt(ms + EPS) * gain_ref[...]
    for b in range(nb):
        o_ref[b] = yn[b * chunk:(b + 1) * chunk].astype(o_ref.dtype)


def _s5_mixer(u, lam_re, lam_im, log_dt, b_re, b_im, c_re, c_im, d_skip, glu_w, glu_b, out_gain):
    nb, s, _ = u.shape
    a_re, a_im, bb_re, bb_im = _ssm_zoh(lam_re, lam_im, log_dt, b_re, b_im)
    eye = jnp.eye(SSM_SLAB_GROUPS, dtype=F32)

    def in_slab(bb):
        w = jnp.einsum('mgnc,gh->mgchn', bb.reshape(N_SLABS, SSM_SLAB_GROUPS, SSM_STATE, SSM_GROUP), eye)
        return w.reshape(N_SLABS, LANES, SLAB_STATE)

    def out_slab(cc):
        w = jnp.einsum('mgcn,gh->mgnhc', cc.reshape(N_SLABS, SSM_SLAB_GROUPS, SSM_GROUP, SSM_STATE), eye)
        return w.reshape(N_SLABS, SLAB_STATE, LANES)

    wb = jnp.concatenate([in_slab(bb_re), in_slab(bb_im)], axis=2).astype(BF16)
    wc = jnp.concatenate([out_slab(c_re.astype(F32)), -out_slab(c_im.astype(F32))], axis=1).astype(BF16)
    half_states = (N_SSM_GROUPS // 2) * SSM_STATE

    def seq_rows(a):
        return jnp.repeat(a.reshape(2, half_states), nb, axis=0)

    fixed2 = lambda c: (0, 0)
    fixed3 = lambda c: (0, 0, 0)
    chunk = SSM_CHUNK
    return pl.pallas_call(
        functools.partial(_ssm_kernel, nb=nb, chunk=chunk),
        grid=(s // chunk,),
        in_specs=[pl.BlockSpec((nb, chunk, D_SSM), lambda c: (0, c, 0)),
                  pl.BlockSpec((N_SLABS, LANES, 2 * SLAB_STATE), fixed3),
                  pl.BlockSpec((N_SLABS, 2 * SLAB_STATE, LANES), fixed3),
                  pl.BlockSpec((2 * nb, half_states), fixed2),
                  pl.BlockSpec((2 * nb, half_states), fixed2),
                  pl.BlockSpec((1, D_SSM), fixed2),
                  pl.BlockSpec((D_SSM, D_SSM), fixed2),
                  pl.BlockSpec((1, D_SSM), fixed2),
                  pl.BlockSpec((1, D_SSM), fixed2)],
        out_specs=pl.BlockSpec((nb, chunk, D_SSM), lambda c: (0, c, 0)),
        out_shape=jax.ShapeDtypeStruct((nb, s, D_SSM), BF16),
        scratch_shapes=[pltpu.VMEM((2 * half_states // LANES, 2 * nb * chunk, LANES), F32),
                        pltpu.VMEM((nb * chunk, D_SSM), F32),
                        pltpu.VMEM((2 * nb, half_states), F32),
                        pltpu.VMEM((2 * nb, half_states), F32)],
        compiler_params=_params(("arbitrary",)),
        name="ssm",
    )(u, wb, wc, seq_rows(a_re), seq_rows(a_im), d_skip.astype(F32).reshape(1, D_SSM),
      glu_w.astype(BF16), glu_b.astype(F32)[None, :], out_gain.astype(F32)[None, :])


def _outproj_kernel(a_ref, s_ref, x_ref, ag_ref, wa_ref, ws_ref, fg_ref, x1_ref, hn_ref):
    a = a_ref[...]
    ms = jnp.mean(a * a, axis=-1, keepdims=True)
    an = (a * lax.rsqrt(ms + EPS) * ag_ref[...]).astype(BF16)
    mixed = (jnp.dot(an, wa_ref[...], preferred_element_type=F32)
             + jnp.dot(s_ref[...], ws_ref[...], preferred_element_type=F32))
    x1 = x_ref[...] + mixed
    x1_ref[...] = x1
    ms1 = jnp.mean(x1 * x1, axis=-1, keepdims=True)
    hn_ref[...] = (x1 * lax.rsqrt(ms1 + EPS) * fg_ref[...]).astype(BF16)


def _out_proj(attn, ssm_n, x2, attn_g, w_out, ffn_g):
    t = x2.shape[0]
    row = lambda i: (i, 0)
    fixed = lambda i: (0, 0)
    w = w_out.astype(BF16)
    return pl.pallas_call(
        _outproj_kernel,
        grid=(t // ROW_TILE,),
        in_specs=[pl.BlockSpec((ROW_TILE, D_ATTN), row),
                  pl.BlockSpec((ROW_TILE, D_SSM), row),
                  pl.BlockSpec((ROW_TILE, D_MODEL), row),
                  pl.BlockSpec((1, D_ATTN), fixed),
                  pl.BlockSpec((D_ATTN, D_MODEL), fixed),
                  pl.BlockSpec((D_SSM, D_MODEL), fixed),
                  pl.BlockSpec((1, D_MODEL), fixed)],
        out_specs=[pl.BlockSpec((ROW_TILE, D_MODEL), row)] * 2,
        out_shape=[jax.ShapeDtypeStruct((t, D_MODEL), F32), jax.ShapeDtypeStruct((t, D_MODEL), BF16)],
        compiler_params=_params(("arbitrary",)),
        name="out_proj",
    )(attn, ssm_n, x2, attn_g.astype(F32)[None, :], w[:D_ATTN], w[D_ATTN:], ffn_g.astype(F32)[None, :])


def _top16(s):
    iota = lax.broadcasted_iota(jnp.int32, s.shape, 0)
    rank = jnp.full(s.shape, NO_RANK, F32)
    vals = []
    for it in range(PEER_TOPK):
        m = jnp.max(s, axis=0, keepdims=True)
        idx = jnp.min(jnp.where(s == m, iota, s.shape[0]), axis=0, keepdims=True)
        hit = iota == idx
        rank = jnp.where(hit, float(it), rank)
        s = jnp.where(hit, -jnp.inf, s)
        vals.append(m)
    return vals, rank


def _stack_rows(rows, n):
    iota = lax.broadcasted_iota(jnp.int32, (n, rows[0].shape[1]), 0)
    out = jnp.zeros((n, rows[0].shape[1]), F32)
    for i, r in enumerate(rows):
        out = jnp.where(iota == i, r, out)
    return out


def _route_head(s1, s2):
    tb = s1.shape[1]
    v1, rank1 = _top16(s1)
    v2, rank2 = _top16(s2)
    v2_all = _stack_rows(v2, PEER_TOPK)
    v1_hi = _stack_rows(v1[SUBLANES:], SUBLANES)
    sub = lax.broadcasted_iota(jnp.int32, (SUBLANES, tb), 0)
    pieces = [v1[0] + v2_all]
    flats = [lax.broadcasted_iota(jnp.int32, (PEER_TOPK, tb), 0)]
    for a in range(1, SUBLANES):
        limit = PEER_TOPK // (a + 1)
        pieces.append(jnp.where(sub < limit, v1[a] + v2_all[:SUBLANES], -jnp.inf))
        flats.append(a * PEER_TOPK + sub)
    pieces.append(v1_hi + v2[0])
    flats.append((sub + SUBLANES) * PEER_TOPK)
    cand0 = jnp.concatenate(pieces, axis=0)
    flat = jnp.concatenate(flats, axis=0)
    cand = cand0
    sel = jnp.zeros(cand.shape, F32)
    for _ in range(PEER_TOPK):
        m = jnp.max(cand, axis=0, keepdims=True)
        idx = jnp.min(jnp.where(cand == m, flat, PEER_TOPK * PEER_TOPK), axis=0, keepdims=True)
        hit = flat == idx
        sel = jnp.where(hit, 1.0, sel)
        cand = jnp.where(hit, -jnp.inf, cand)
    top = v1[0] + v2[0]
    z = jnp.sum(sel * jnp.exp(jnp.where(sel > 0, cand0, top) - top), axis=0, keepdims=True)
    cnt = [jnp.sum(sel[:PEER_TOPK], axis=0, keepdims=True)]
    for a in range(1, SUBLANES):
        lo = PEER_TOPK + SUBLANES * (a - 1)
        cnt.append(jnp.sum(sel[lo:lo + SUBLANES], axis=0, keepdims=True))
    lo = PEER_TOPK + SUBLANES * (SUBLANES - 1)
    for i in range(SUBLANES):
        cnt.append(sel[lo + i:lo + i + 1])
    c1 = jnp.zeros(rank1.shape, F32)
    for a in range(PEER_TOPK):
        c1 = jnp.where(rank1 == float(a), cnt[a], c1)
    e1 = jnp.exp(s1 - v1[0])
    e2n = jnp.exp(s2 - v2[0]) / z
    return c1, e1, rank2, e2n


def _route_kernel(hn_ref, wq_ref, k1_ref, k2_ref, c1_ref, e1_ref, r2_ref, e2_ref):
    qt = lax.dot_general(wq_ref[...], hn_ref[...], (((1,), (1,)), ((), ())), preferred_element_type=F32)
    half = PEER_QDIM // 2
    for h in range(PEER_HEADS):
        q1 = qt[PEER_QDIM * h:PEER_QDIM * h + half].astype(BF16)
        q2 = qt[PEER_QDIM * h + half:PEER_QDIM * (h + 1)].astype(BF16)
        s1 = jnp.dot(k1_ref[h], q1, preferred_element_type=F32)
        s2 = jnp.dot(k2_ref[h], q2, preferred_element_type=F32)
        c1, e1, r2, e2n = _route_head(s1, s2)
        c1_ref[h] = c1
        e1_ref[h] = e1
        r2_ref[h] = r2.astype(r2_ref.dtype)
        e2_ref[h] = e2n.astype(e2_ref.dtype)


def _peer_route(hn, w_q, keys1, keys2):
    t = hn.shape[0]
    blk = pl.BlockSpec((PEER_HEADS, PEER_KEYS, ROUTE_TILE), lambda i: (0, 0, i))
    fixed3 = lambda i: (0, 0, 0)
    shape = (PEER_HEADS, PEER_KEYS, t)
    return pl.pallas_call(
        _route_kernel,
        grid=(t // ROUTE_TILE,),
        in_specs=[pl.BlockSpec((ROUTE_TILE, D_MODEL), lambda i: (i, 0)),
                  pl.BlockSpec((PEER_HEADS * PEER_QDIM, D_MODEL), lambda i: (0, 0)),
                  pl.BlockSpec((PEER_HEADS, PEER_KEYS, PEER_QDIM // 2), fixed3),
                  pl.BlockSpec((PEER_HEADS, PEER_KEYS, PEER_QDIM // 2), fixed3)],
        out_specs=[blk] * 4,
        out_shape=[jax.ShapeDtypeStruct(shape, F32), jax.ShapeDtypeStruct(shape, F32),
                   jax.ShapeDtypeStruct(shape, BF16), jax.ShapeDtypeStruct(shape, BF16)],
        compiler_params=_params(("arbitrary",)),
        name="peer_route",
    )(hn, w_q.T.astype(BF16), keys1.astype(BF16), keys2.astype(BF16))


def _peer_kernel(hn_ref, u_ref, vt_ref, c1_ref, e1_ref, r2_ref, e2_ref, x1_ref, o_ref, acc_ref):
    e = pl.program_id(1)
    tiles = PEER_EXPERTS_STEP // PEER_KEYS

    @pl.when(e == 0)
    def _():
        acc_ref[...] = jnp.zeros_like(acc_ref)

    hn = hn_ref[...]
    packed = 2 * SUBLANES
    total = None
    for c in range(PEER_EXPERTS_STEP // PEER_SUB):
        rows = slice(PEER_SUB * c, PEER_SUB * (c + 1))
        at = lax.dot_general(u_ref[rows, :], hn, (((1,), (1,)), ((), ())), preferred_element_type=F32)
        gs = []
        for tl in range(PEER_SUB // PEER_KEYS):
            tile = e * tiles + c * (PEER_SUB // PEER_KEYS) + tl
            gate = jnp.zeros((PEER_KEYS // packed, packed, PEER_TOKENS), BF16)
            for h in range(PEER_HEADS):
                c1 = jnp.broadcast_to(c1_ref[h, pl.ds(tile, 1), :], (packed, PEER_TOKENS)).astype(BF16)
                e1 = jnp.broadcast_to(e1_ref[h, pl.ds(tile, 1), :], (packed, PEER_TOKENS)).astype(BF16)
                r2 = r2_ref[h].reshape(gate.shape)
                e2 = e2_ref[h].reshape(gate.shape)
                gate = gate + jnp.where(r2 < c1[None], e1[None] * e2, jnp.zeros_like(e2))
            a = at[PEER_KEYS * tl:PEER_KEYS * (tl + 1)].astype(BF16)
            gs.append(gate.reshape(PEER_KEYS, PEER_TOKENS) * _erf_gelu(a))
        g = jnp.concatenate(gs, axis=0)
        part = jnp.dot(vt_ref[:, rows], g, preferred_element_type=F32)
        total = part if total is None else total + part
    acc_ref[...] += total

    @pl.when(e == pl.num_programs(1) - 1)
    def _():
        o_ref[...] = x1_ref[...] + acc_ref[...].T


def _peer_mix(hn, x1, u_tab, v_tab, c1, e1, r2, e2n):
    t = hn.shape[0]
    n_exp = u_tab.shape[0]
    route = pl.BlockSpec((PEER_HEADS, PEER_KEYS, PEER_TOKENS), lambda i, e: (0, 0, i))
    tok = lambda i, e: (i, 0)
    return pl.pallas_call(
        _peer_kernel,
        grid=(t // PEER_TOKENS, n_exp // PEER_EXPERTS_STEP),
        in_specs=[pl.BlockSpec((PEER_TOKENS, D_MODEL), tok),
                  pl.BlockSpec((PEER_EXPERTS_STEP, D_MODEL), lambda i, e: (e, 0)),
                  pl.BlockSpec((D_MODEL, PEER_EXPERTS_STEP), lambda i, e: (0, e)),
                  route, route, route, route,
                  pl.BlockSpec((PEER_TOKENS, D_MODEL), tok)],
        out_specs=pl.BlockSpec((PEER_TOKENS, D_MODEL), tok),
        out_shape=jax.ShapeDtypeStruct((t, D_MODEL), F32),
        scratch_shapes=[pltpu.VMEM((D_MODEL, PEER_TOKENS), F32)],
        compiler_params=_params(("arbitrary", "arbitrary")),
        name="peer_mix",
    )(hn, u_tab.astype(BF16), v_tab.T.astype(BF16), c1, e1, r2, e2n, x1)


def kernel(x, norm_mix_g, w_in, q_norm_g, k_norm_g, rel_bias, ssm_lambda_re, ssm_lambda_im, ssm_log_dt,
           ssm_b_re, ssm_b_im, ssm_c_re, ssm_c_im, ssm_d, ssm_glu_w, ssm_glu_b, attn_out_g, ssm_out_g,
           w_out, norm_ffn_g, peer_w_q, peer_keys1, peer_keys2, peer_u, peer_v):
    b, s, d = x.shape
    x2 = x.reshape(b * s, d)
    q, k, v, u = _in_proj(x2, norm_mix_g, w_in, q_norm_g, k_norm_g)
    bias = _bias_tables(rel_bias)
    attn = _attention(q.reshape(b, s, D_ATTN), k.reshape(b, s, D_ATTN), v.reshape(b, s, D_ATTN), bias)
    ssm_n = _s5_mixer(u.reshape(b, s, D_SSM), ssm_lambda_re, ssm_lambda_im, ssm_log_dt, ssm_b_re, ssm_b_im,
                      ssm_c_re, ssm_c_im, ssm_d, ssm_glu_w, ssm_glu_b, ssm_out_g)
    x1, hn = _out_proj(attn.reshape(b * s, D_ATTN), ssm_n.reshape(b * s, D_SSM), x2, attn_out_g, w_out, norm_ffn_g)
    c1, e1, r2, e2n = _peer_route(hn, peer_w_q, peer_keys1, peer_keys2)
    out = _peer_mix(hn, x1, peer_u, peer_v, c1, e1, r2, e2n)
    return out.reshape(b, s, d).astype(x.dtype)
```

```python
import functools
import math

import jax
import jax.numpy as jnp
import numpy as np
from jax import lax
from jax.experimental import pallas as pl
from jax.experimental.pallas import tpu as pltpu

F32 = jnp.float32
BF16 = jnp.bfloat16

D_MODEL = 2048
HEAD_DIM = 64
N_ATTN_HEADS = 16
D_ATTN = N_ATTN_HEADS * HEAD_DIM
SSM_GROUP = 16
N_SSM_GROUPS = 64
D_SSM = N_SSM_GROUPS * SSM_GROUP
SSM_STATE = 64
D_IN_PROJ = 3 * D_ATTN + D_SSM
DILATED_BRANCHES = ((128, 1), (512, 4), (2048, 16))
BLK = 128
N_BUCKETS = 32
MAX_DISTANCE = 2048
PEER_HEADS = 8
PEER_KEYS = 128
PEER_QDIM = 256
PEER_TOPK = 16
EPS = 1e-6
NEG = -1e30

LANES = 128
SUBLANES = 8
VMEM_LIMIT = 56 * 1024 * 1024

ROW_TILE = 256
SSM_CHUNK = 128
SSM_SLAB_GROUPS = LANES // SSM_GROUP
N_SLABS = N_SSM_GROUPS // SSM_SLAB_GROUPS
SLAB_STATE = SSM_SLAB_GROUPS * SSM_STATE
ATTN_LOOKAHEAD = 2
ROUTE_TILE = 256
PEER_TOKENS = 512
PEER_EXPERTS_STEP = 512
PEER_SUB = 256
NO_RANK = 99.0


def _params(sem, vmem=VMEM_LIMIT):
    return pltpu.CompilerParams(dimension_semantics=sem, vmem_limit_bytes=vmem)


def _erf_gelu(x):
    return 0.5 * x * (1.0 + lax.erf(x * math.sqrt(0.5)))


def _t5_bucket(dist):
    max_exact = N_BUCKETS // 2
    n = np.maximum(dist, 0)
    nf = np.maximum(n, 1).astype(np.float32)
    large = max_exact + (np.log(nf / np.float32(max_exact)) / np.float32(math.log(MAX_DISTANCE / max_exact))
                         * np.float32(N_BUCKETS - max_exact)).astype(np.int32)
    large = np.minimum(large, N_BUCKETS - 1)
    return np.where(n < max_exact, n, large)


def _bias_kernel(bkt_ref, rb_ref, out_ref):
    bkt = bkt_ref[0]
    for h in range(N_ATTN_HEADS):
        acc = jnp.zeros((BLK, 2 * BLK), F32)
        for b in range(N_BUCKETS):
            acc = jnp.where(bkt == b, rb_ref[b * N_ATTN_HEADS + h], acc)
        out_ref[0, h] = acc


def _bias_tables(rel_bias):
    qi = np.arange(BLK)[:, None]
    kj = np.arange(2 * BLK)[None, :]
    rel = qi - kj + BLK
    buckets = jnp.asarray(np.stack([_t5_bucket(rel * dil) for _, dil in DILATED_BRANCHES]).astype(np.int32))
    nbr = len(DILATED_BRANCHES)
    return pl.pallas_call(
        _bias_kernel,
        grid=(nbr,),
        in_specs=[pl.BlockSpec((1, BLK, 2 * BLK), lambda i: (i, 0, 0)),
                  pl.BlockSpec(memory_space=pltpu.SMEM)],
        out_specs=pl.BlockSpec((1, N_ATTN_HEADS, BLK, 2 * BLK), lambda i: (i, 0, 0, 0)),
        out_shape=jax.ShapeDtypeStruct((nbr, N_ATTN_HEADS, BLK, 2 * BLK), F32),
        compiler_params=_params(("arbitrary",)),
        name="bias_table",
    )(buckets, rel_bias.astype(F32).reshape(N_BUCKETS * N_ATTN_HEADS))


def _head_rmsnorm(z, gain, ones, scale):
    outs = []
    for c in range(z.shape[1] // LANES):
        zc = z[:, LANES * c:LANES * (c + 1)]
        sq = zc * zc
        hi = sq.astype(BF16)
        lo = (sq - hi.astype(F32)).astype(BF16)
        msq = (jnp.dot(hi, ones, preferred_element_type=F32)
               + jnp.dot(lo, ones, preferred_element_type=F32))
        y = zc * lax.rsqrt(msq + EPS)
        outs.append(y * gain[:, LANES * c:LANES * (c + 1)] * scale)
    return jnp.concatenate(outs, axis=1)


def _inproj_kernel(x_ref, g_ref, w_ref, qg_ref, kg_ref, ones_ref, q_ref, k_ref, v_ref, u_ref):
    x = x_ref[...]
    ms = jnp.mean(x * x, axis=-1, keepdims=True)
    h = (x * lax.rsqrt(ms + EPS) * g_ref[...]).astype(BF16)
    proj = jnp.dot(h, w_ref[...], preferred_element_type=F32)
    ones = ones_ref[...]
    q_ref[...] = _head_rmsnorm(proj[:, :D_ATTN], qg_ref[...], ones, 1.0 / math.sqrt(HEAD_DIM))
    k_ref[...] = _head_rmsnorm(proj[:, D_ATTN:2 * D_ATTN], kg_ref[...], ones, 1.0)
    v_ref[...] = proj[:, 2 * D_ATTN:3 * D_ATTN]
    u_ref[...] = proj[:, 3 * D_ATTN:]


def _in_proj(x2, norm_g, w_in, q_g, k_g):
    t = x2.shape[0]
    head_of_lane = jnp.arange(LANES) // HEAD_DIM
    ones = jnp.where(head_of_lane[:, None] == head_of_lane[None, :], 1.0 / HEAD_DIM, 0.0).astype(BF16)
    qg = jnp.tile(q_g.astype(F32), N_ATTN_HEADS)[None, :]
    kg = jnp.tile(k_g.astype(F32), N_ATTN_HEADS)[None, :]
    row = lambda i: (i, 0)
    fixed = lambda i: (0, 0)
    outs = pl.pallas_call(
        _inproj_kernel,
        grid=(t // ROW_TILE,),
        in_specs=[pl.BlockSpec((ROW_TILE, D_MODEL), row),
                  pl.BlockSpec((1, D_MODEL), fixed),
                  pl.BlockSpec((D_MODEL, D_IN_PROJ), fixed),
                  pl.BlockSpec((1, D_ATTN), fixed),
                  pl.BlockSpec((1, D_ATTN), fixed),
                  pl.BlockSpec((LANES, LANES), fixed)],
        out_specs=[pl.BlockSpec((ROW_TILE, D_ATTN), row)] * 3 + [pl.BlockSpec((ROW_TILE, D_SSM), row)],
        out_shape=[jax.ShapeDtypeStruct((t, D_ATTN), F32)] * 3 + [jax.ShapeDtypeStruct((t, D_SSM), F32)],
        compiler_params=_params(("arbitrary",)),
        name="in_proj",
    )(x2, norm_g.astype(F32)[None, :], w_in.astype(BF16), qg, kg, ones)
    return outs


def _rows(start, size, stride):
    return pl.ds(start, size, stride=stride) if stride > 1 else pl.ds(start, size)


def _attn_scores(q_ref, k_ref, v_ref, bias_ref, blk, tri_cur, tri_prev, head0):
    br, dil, n, r = blk
    qrows = _rows(r + dil * BLK * n, BLK, dil)
    qb = q_ref[qrows, :].astype(BF16)
    if n == 0:
        krows = qrows
        valid = tri_cur
    else:
        krows = _rows(r + dil * BLK * (n - 1), 2 * BLK, dil)
        valid = jnp.concatenate([tri_prev, tri_cur], axis=1)
    kb = k_ref[krows, :].astype(BF16)
    vb = v_ref[krows, :].astype(BF16)
    scores = []
    for h in range(2):
        mine = head0 if h == 0 else jnp.logical_not(head0)
        qh = jnp.where(mine, qb, jnp.zeros_like(qb))
        s = lax.dot_general(qh, kb, (((1,), (1,)), ((), ())), preferred_element_type=F32)
        bias = bias_ref[br, h, :, BLK:] if n == 0 else bias_ref[br, h]
        scores.append(jnp.where(valid, s + bias, NEG))
    return scores, vb, qrows


def _attn_values(scores, vb, qrows, br, pv_ref, den_ref, m_ref, head0):
    ones = jnp.ones_like(vb)
    pv, mx = [], []
    for h in range(2):
        mine = head0 if h == 0 else jnp.logical_not(head0)
        m = jnp.max(scores[h], axis=-1, keepdims=True)
        p = jnp.exp(scores[h] - m).astype(BF16)
        pv.append(jnp.dot(p, jnp.where(mine, vb, ones), preferred_element_type=F32))
        mx.append(m)
    pv_ref[br, qrows, :] = jnp.where(head0, pv[0], pv[1])
    den_ref[br, qrows, :] = jnp.where(head0, pv[1], pv[0])
    m_ref[br, qrows, :] = jnp.where(head0, mx[0], mx[1])


def _attn_kernel(q_ref, k_ref, v_ref, bias_ref, o_ref, pv_ref, den_ref, m_ref, *, seq):
    qi = lax.broadcasted_iota(jnp.int32, (BLK, BLK), 0)
    kj = lax.broadcasted_iota(jnp.int32, (BLK, BLK), 1)
    tri_cur = qi >= kj
    tri_prev = kj >= qi
    head0 = lax.broadcasted_iota(jnp.int32, (1, LANES), 1) < HEAD_DIM
    nbr = len(DILATED_BRANCHES)
    blocks = []
    for br, (window, dil) in enumerate(DILATED_BRANCHES):
        assert window // dil == BLK
        blocks += [(br, dil, n, r) for r in range(dil) for n in range(seq // dil // BLK)]
    ahead = [_attn_scores(q_ref, k_ref, v_ref, bias_ref, b, tri_cur, tri_prev, head0)
             for b in blocks[:ATTN_LOOKAHEAD]]
    for i, blk in enumerate(blocks):
        if i + ATTN_LOOKAHEAD < len(blocks):
            ahead.append(_attn_scores(q_ref, k_ref, v_ref, bias_ref, blocks[i + ATTN_LOOKAHEAD],
                                      tri_cur, tri_prev, head0))
        _attn_values(*ahead.pop(0), blk[0], pv_ref, den_ref, m_ref, head0)
    m_all = [m_ref[br] for br in range(nbr)]
    m_top = functools.reduce(jnp.maximum, m_all)
    num = jnp.zeros((seq, LANES), F32)
    den = jnp.zeros((seq, LANES), F32)
    for br in range(nbr):
        w = jnp.exp(m_all[br] - m_top)
        num = num + w * pv_ref[br]
        den = den + w * pltpu.roll(den_ref[br], HEAD_DIM, axis=1)
    o_ref[...] = num / den


def _attention(q, k, v, bias):
    b, s, _ = q.shape
    blk = pl.BlockSpec((None, s, LANES), lambda i, p: (i, 0, p))
    nbr = len(DILATED_BRANCHES)
    return pl.pallas_call(
        functools.partial(_attn_kernel, seq=s),
        grid=(b, D_ATTN // LANES),
        in_specs=[blk, blk, blk,
                  pl.BlockSpec((nbr, 2, BLK, 2 * BLK), lambda i, p: (0, p, 0, 0))],
        out_specs=blk,
        out_shape=jax.ShapeDtypeStruct((b, s, D_ATTN), F32),
        scratch_shapes=[pltpu.VMEM((nbr, s, LANES), F32)] * 3,
        compiler_params=_params(("arbitrary", "arbitrary")),
        name="attention",
    )(q, k, v, bias)


def _zoh_kernel(lr_ref, li_ref, dt_ref, lrr_ref, lir_ref, br_ref, bi_ref,
                are_ref, aim_ref, bbr_ref, bbi_ref):
    dt = jnp.exp(dt_ref[...])

    def zoh(lr, li):
        mag = jnp.exp(lr * dt)
        a_re, a_im = mag * jnp.cos(li * dt), mag * jnp.sin(li * dt)
        den = lr * lr + li * li
        f_re = ((a_re - 1.0) * lr + a_im * li) / den
        f_im = (a_im * lr - (a_re - 1.0) * li) / den
        return a_re, a_im, f_re, f_im

    a_re, a_im, _, _ = zoh(lr_ref[...], li_ref[...])
    are_ref[...] = a_re
    aim_ref[...] = a_im
    _, _, f_re, f_im = zoh(lrr_ref[...], lir_ref[...])
    br, bi = br_ref[...], bi_ref[...]
    bbr_ref[...] = f_re * br - f_im * bi
    bbi_ref[...] = f_re * bi + f_im * br


def _ssm_zoh(lam_re, lam_im, log_dt, b_re, b_im):
    g, n, c = b_re.shape
    rep = lambda a: jnp.repeat(a.astype(F32), c, axis=1)
    a_re, a_im, bb_re, bb_im = pl.pallas_call(
        _zoh_kernel,
        out_shape=[jax.ShapeDtypeStruct((g, n), F32)] * 2 + [jax.ShapeDtypeStruct((g, n * c), F32)] * 2,
        name="ssm_zoh",
    )(lam_re.astype(F32), lam_im.astype(F32), log_dt.astype(F32)[:, None], rep(lam_re), rep(lam_im),
      b_re.astype(F32).reshape(g, n * c), b_im.astype(F32).reshape(g, n * c))
    return a_re, a_im, bb_re.reshape(g, n, c), bb_im.reshape(g, n, c)


def _ssm_kernel(u_ref, wb_ref, wc_ref, are_ref, aim_ref, d_ref, gw_ref, gb_ref, gain_ref,
                o_ref, bu_ref, y_ref, sr_ref, si_ref, *, nb, chunk):
    half = N_SLABS // 2
    seqs = 2 * nb
    lane_blocks = 2 * SLAB_STATE // LANES

    @pl.when(pl.program_id(0) == 0)
    def _():
        sr_ref[...] = jnp.zeros_like(sr_ref)
        si_ref[...] = jnp.zeros_like(si_ref)

    for b in range(nb):
        for m in range(N_SLABS):
            gh, mp = divmod(m, half)
            ub = u_ref[b, :, LANES * m:LANES * (m + 1)].astype(BF16)
            bu = jnp.dot(ub, wb_ref[m], preferred_element_type=F32)
            for j in range(lane_blocks):
                bu_ref[lane_blocks * mp + j, pl.ds(gh * nb + b, chunk, stride=seqs), :] = (
                    bu[:, LANES * j:LANES * (j + 1)])

    def load_state(rows, mp, part):
        j0 = lane_blocks * mp + part * (lane_blocks // 2)
        return jnp.concatenate([bu_ref[j0 + j, rows, :] for j in range(lane_blocks // 2)], axis=1)

    def store_state(rows, mp, part, val):
        j0 = lane_blocks * mp + part * (lane_blocks // 2)
        for j in range(lane_blocks // 2):
            bu_ref[j0 + j, rows, :] = val[:, LANES * j:LANES * (j + 1)]

    def step(t, carry):
        base = pl.multiple_of(t * seqs, seqs)
        rows = pl.ds(base, seqs)
        new = []
        for mp in range(half):
            xr, xi = carry[2 * mp], carry[2 * mp + 1]
            ar = are_ref[:, SLAB_STATE * mp:SLAB_STATE * (mp + 1)]
            ai = aim_ref[:, SLAB_STATE * mp:SLAB_STATE * (mp + 1)]
            nr = ar * xr - ai * xi + load_state(rows, mp, 0)
            ni = ar * xi + ai * xr + load_state(rows, mp, 1)
            store_state(rows, mp, 0, nr)
            store_state(rows, mp, 1, ni)
            new += [nr, ni]
        return tuple(new)

    init = []
    for mp in range(half):
        init += [sr_ref[:, SLAB_STATE * mp:SLAB_STATE * (mp + 1)], si_ref[:, SLAB_STATE * mp:SLAB_STATE * (mp + 1)]]
    final = lax.fori_loop(0, chunk, step, tuple(init))
    for mp in range(half):
        sr_ref[:, SLAB_STATE * mp:SLAB_STATE * (mp + 1)] = final[2 * mp]
        si_ref[:, SLAB_STATE * mp:SLAB_STATE * (mp + 1)] = final[2 * mp + 1]

    for b in range(nb):
        for m in range(N_SLABS):
            gh, mp = divmod(m, half)
            xs = jnp.concatenate([bu_ref[lane_blocks * mp + j, pl.ds(gh * nb + b, chunk, stride=seqs), :]
                                  for j in range(lane_blocks)], axis=1)
            y = jnp.dot(xs.astype(BF16), wc_ref[m], preferred_element_type=F32)
            cols = slice(LANES * m, LANES * (m + 1))
            y_ref[b * chunk:(b + 1) * chunk, cols] = y + d_ref[:, cols] * u_ref[b, :, cols]

    y = _erf_gelu(y_ref[...])
    z = jnp.dot(y.astype(BF16), gw_ref[...], preferred_element_type=F32) + gb_ref[...]
    y = y * jax.nn.sigmoid(z)
    ms = jnp.mean(y * y, axis=-1, keepdims=True)
    yn = y * lax.rsqrt(ms + EPS) * gain_ref[...]
    for b in range(nb):
        o_ref[b] = yn[b * chunk:(b + 1) * chunk].astype(o_ref.dtype)


def _s5_mixer(u, lam_re, lam_im, log_dt, b_re, b_im, c_re, c_im, d_skip, glu_w, glu_b, out_gain):
    nb, s, _ = u.shape
    a_re, a_im, bb_re, bb_im = _ssm_zoh(lam_re, lam_im, log_dt, b_re, b_im)
    eye = jnp.eye(SSM_SLAB_GROUPS, dtype=F32)

    def in_slab(bb):
        w = jnp.einsum('mgnc,gh->mgchn', bb.reshape(N_SLABS, SSM_SLAB_GROUPS, SSM_STATE, SSM_GROUP), eye)
        return w.reshape(N_SLABS, LANES, SLAB_STATE)

    def out_slab(cc):
        w = jnp.einsum('mgcn,gh->mgnhc', cc.reshape(N_SLABS, SSM_SLAB_GROUPS, SSM_GROUP, SSM_STATE), eye)
        return w.reshape(N_SLABS, SLAB_STATE, LANES)

    wb = jnp.concatenate([in_slab(bb_re), in_slab(bb_im)], axis=2).astype(BF16)
    wc = jnp.concatenate([out_slab(c_re.astype(F32)), -out_slab(c_im.astype(F32))], axis=1).astype(BF16)
    half_states = (N_SSM_GROUPS // 2) * SSM_STATE

    def seq_rows(a):
        return jnp.repeat(a.reshape(2, half_states), nb, axis=0)

    fixed2 = lambda c: (0, 0)
    fixed3 = lambda c: (0, 0, 0)
    chunk = SSM_CHUNK
    return pl.pallas_call(
        functools.partial(_ssm_kernel, nb=nb, chunk=chunk),
        grid=(s // chunk,),
        in_specs=[pl.BlockSpec((nb, chunk, D_SSM), lambda c: (0, c, 0)),
                  pl.BlockSpec((N_SLABS, LANES, 2 * SLAB_STATE), fixed3),
                  pl.BlockSpec((N_SLABS, 2 * SLAB_STATE, LANES), fixed3),
                  pl.BlockSpec((2 * nb, half_states), fixed2),
                  pl.BlockSpec((2 * nb, half_states), fixed2),
                  pl.BlockSpec((1, D_SSM), fixed2),
                  pl.BlockSpec((D_SSM, D_SSM), fixed2),
                  pl.BlockSpec((1, D_SSM), fixed2),
                  pl.BlockSpec((1, D_SSM), fixed2)],
        out_specs=pl.BlockSpec((nb, chunk, D_SSM), lambda c: (0, c, 0)),
        out_shape=jax.ShapeDtypeStruct((nb, s, D_SSM), BF16),
        scratch_shapes=[pltpu.VMEM((2 * half_states // LANES, 2 * nb * chunk, LANES), F32),
                        pltpu.VMEM((nb * chunk, D_SSM), F32),
                        pltpu.VMEM((2 * nb, half_states), F32),
                        pltpu.VMEM((2 * nb, half_states), F32)],
        compiler_params=_params(("arbitrary",)),
        name="ssm",
    )(u, wb, wc, seq_rows(a_re), seq_rows(a_im), d_skip.astype(F32).reshape(1, D_SSM),
      glu_w.astype(BF16), glu_b.astype(F32)[None, :], out_gain.astype(F32)[None, :])


def _outproj_kernel(a_ref, s_ref, x_ref, ag_ref, wa_ref, ws_ref, fg_ref, x1_ref, hn_ref):
    a = a_ref[...]
    ms = jnp.mean(a * a, axis=-1, keepdims=True)
    an = (a * lax.rsqrt(ms + EPS) * ag_ref[...]).astype(BF16)
    mixed = (jnp.dot(an, wa_ref[...], preferred_element_type=F32)
             + jnp.dot(s_ref[...], ws_ref[...], preferred_element_type=F32))
    x1 = x_ref[...] + mixed
    x1_ref[...] = x1
    ms1 = jnp.mean(x1 * x1, axis=-1, keepdims=True)
    hn_ref[...] = (x1 * lax.rsqrt(ms1 + EPS) * fg_ref[...]).astype(BF16)


def _out_proj(attn, ssm_n, x2, attn_g, w_out, ffn_g):
    t = x2.shape[0]
    row = lambda i: (i, 0)
    fixed = lambda i: (0, 0)
    w = w_out.astype(BF16)
    return pl.pallas_call(
        _outproj_kernel,
        grid=(t // ROW_TILE,),
        in_specs=[pl.BlockSpec((ROW_TILE, D_ATTN), row),
                  pl.BlockSpec((ROW_TILE, D_SSM), row),
                  pl.BlockSpec((ROW_TILE, D_MODEL), row),
                  pl.BlockSpec((1, D_ATTN), fixed),
                  pl.BlockSpec((D_ATTN, D_MODEL), fixed),
                  pl.BlockSpec((D_SSM, D_MODEL), fixed),
                  pl.BlockSpec((1, D_MODEL), fixed)],
        out_specs=[pl.BlockSpec((ROW_TILE, D_MODEL), row)] * 2,
        out_shape=[jax.ShapeDtypeStruct((t, D_MODEL), F32), jax.ShapeDtypeStruct((t, D_MODEL), BF16)],
        compiler_params=_params(("arbitrary",)),
        name="out_proj",
    )(attn, ssm_n, x2, attn_g.astype(F32)[None, :], w[:D_ATTN], w[D_ATTN:], ffn_g.astype(F32)[None, :])


def _top16(s):
    iota = lax.broadcasted_iota(jnp.int32, s.shape, 0)
    rank = jnp.full(s.shape, NO_RANK, F32)
    vals = []
    for it in range(PEER_TOPK):
        m = jnp.max(s, axis=0, keepdims=True)
        idx = jnp.min(jnp.where(s == m, iota, s.shape[0]), axis=0, keepdims=True)
        hit = iota == idx
        rank = jnp.where(hit, float(it), rank)
        s = jnp.where(hit, -jnp.inf, s)
        vals.append(m)
    return vals, rank


def _stack_rows(rows, n):
    iota = lax.broadcasted_iota(jnp.int32, (n, rows[0].shape[1]), 0)
    out = jnp.zeros((n, rows[0].shape[1]), F32)
    for i, r in enumerate(rows):
        out = jnp.where(iota == i, r, out)
    return out


def _route_head(s1, s2):
    tb = s1.shape[1]
    v1, rank1 = _top16(s1)
    v2, rank2 = _top16(s2)
    v2_all = _stack_rows(v2, PEER_TOPK)
    v1_hi = _stack_rows(v1[SUBLANES:], SUBLANES)
    sub = lax.broadcasted_iota(jnp.int32, (SUBLANES, tb), 0)
    pieces = [v1[0] + v2_all]
    flats = [lax.broadcasted_iota(jnp.int32, (PEER_TOPK, tb), 0)]
    for a in range(1, SUBLANES):
        limit = PEER_TOPK // (a + 1)
        pieces.append(jnp.where(sub < limit, v1[a] + v2_all[:SUBLANES], -jnp.inf))
        flats.append(a * PEER_TOPK + sub)
    pieces.append(v1_hi + v2[0])
    flats.append((sub + SUBLANES) * PEER_TOPK)
    cand0 = jnp.concatenate(pieces, axis=0)
    flat = jnp.concatenate(flats, axis=0)
    cand = cand0
    sel = jnp.zeros(cand.shape, F32)
    for _ in range(PEER_TOPK):
        m = jnp.max(cand, axis=0, keepdims=True)
        idx = jnp.min(jnp.where(cand == m, flat, PEER_TOPK * PEER_TOPK), axis=0, keepdims=True)
        hit = flat == idx
        sel = jnp.where(hit, 1.0, sel)
        cand = jnp.where(hit, -jnp.inf, cand)
    top = v1[0] + v2[0]
    z = jnp.sum(sel * jnp.exp(jnp.where(sel > 0, cand0, top) - top), axis=0, keepdims=True)
    cnt = [jnp.sum(sel[:PEER_TOPK], axis=0, keepdims=True)]
    for a in range(1, SUBLANES):
        lo = PEER_TOPK + SUBLANES * (a - 1)
        cnt.append(jnp.sum(sel[lo:lo + SUBLANES], axis=0, keepdims=True))
    lo = PEER_TOPK + SUBLANES * (SUBLANES - 1)
    for i in range(SUBLANES):
        cnt.append(sel[lo + i:lo + i + 1])
    c1 = jnp.zeros(rank1.shape, F32)
    for a in range(PEER_TOPK):
        c1 = jnp.where(rank1 == float(a), cnt[a], c1)
    e1 = jnp.exp(s1 - v1[0])
    e2n = jnp.exp(s2 - v2[0]) / z
    return c1, e1, rank2, e2n


def _route_kernel(hn_ref, wq_ref, k1_ref, k2_ref, c1_ref, e1_ref, r2_ref, e2_ref):
    qt = lax.dot_general(wq_ref[...], hn_ref[...], (((1,), (1,)), ((), ())), preferred_element_type=F32)
    half = PEER_QDIM // 2
    for h in range(PEER_HEADS):
        q1 = qt[PEER_QDIM * h:PEER_QDIM * h + half].astype(BF16)
        q2 = qt[PEER_QDIM * h + half:PEER_QDIM * (h + 1)].astype(BF16)
        s1 = jnp.dot(k1_ref[h], q1, preferred_element_type=F32)
        s2 = jnp.dot(k2_ref[h], q2, preferred_element_type=F32)
        c1, e1, r2, e2n = _route_head(s1, s2)
        c1_ref[h] = c1
        e1_ref[h] = e1
        r2_ref[h] = r2.astype(r2_ref.dtype)
        e2_ref[h] = e2n.astype(e2_ref.dtype)


def _peer_route(hn, w_q, keys1, keys2):
    t = hn.shape[0]
    blk = pl.BlockSpec((PEER_HEADS, PEER_KEYS, ROUTE_TILE), lambda i: (0, 0, i))
    fixed3 = lambda i: (0, 0, 0)
    shape = (PEER_HEADS, PEER_KEYS, t)
    return pl.pallas_call(
        _route_kernel,
        grid=(t // ROUTE_TILE,),
        in_specs=[pl.BlockSpec((ROUTE_TILE, D_MODEL), lambda i: (i, 0)),
                  pl.BlockSpec((PEER_HEADS * PEER_QDIM, D_MODEL), lambda i: (0, 0)),
                  pl.BlockSpec((PEER_HEADS, PEER_KEYS, PEER_QDIM // 2), fixed3),
                  pl.BlockSpec((PEER_HEADS, PEER_KEYS, PEER_QDIM // 2), fixed3)],
        out_specs=[blk] * 4,
        out_shape=[jax.ShapeDtypeStruct(shape, F32), jax.ShapeDtypeStruct(shape, F32),
                   jax.ShapeDtypeStruct(shape, BF16), jax.ShapeDtypeStruct(shape, BF16)],
        compiler_params=_params(("arbitrary",)),
        name="peer_route",
    )(hn, w_q.T.astype(BF16), keys1.astype(BF16), keys2.astype(BF16))


def _peer_gated(c1_ref, e1_ref, r2_ref, e2_ref, at_ref, chunk, valid):
    packed = 2 * SUBLANES
    tiles = PEER_SUB // PEER_KEYS
    gs = []
    for tl in range(tiles):
        tile = chunk * tiles + tl
        gate = jnp.zeros((PEER_KEYS // packed, packed, PEER_TOKENS), BF16)
        for h in range(PEER_HEADS):
            c1 = jnp.broadcast_to(c1_ref[h, pl.ds(tile, 1), :], (packed, PEER_TOKENS)).astype(BF16)
            e1 = jnp.broadcast_to(e1_ref[h, pl.ds(tile, 1), :], (packed, PEER_TOKENS)).astype(BF16)
            r2 = r2_ref[h].reshape(gate.shape)
            e2 = e2_ref[h].reshape(gate.shape)
            gate = gate + jnp.where(r2 < c1[None], e1[None] * e2, jnp.zeros_like(e2))
        a = at_ref[PEER_KEYS * tl:PEER_KEYS * (tl + 1), :].astype(BF16)
        gs.append(gate.reshape(PEER_KEYS, PEER_TOKENS) * _erf_gelu(a))
    g = jnp.concatenate(gs, axis=0)
    return jnp.where(valid, g, jnp.zeros_like(g))


def _peer_kernel(hn_ref, u_ref, vta_ref, vtb_ref, c1_ref, e1_ref, r2_ref, e2_ref, x1_ref, o_ref,
                 acc_ref, at0_ref, at1_ref):
    e = pl.program_id(1)
    last = pl.num_programs(1) - 1
    nt = (((1,), (1,)), ((), ()))

    @pl.when(e == 0)
    def _():
        acc_ref[...] = jnp.zeros_like(acc_ref)
        at1_ref[...] = jnp.zeros_like(at1_ref)

    hn = hn_ref[...]
    route = (c1_ref, e1_ref, r2_ref, e2_ref)
    at0_ref[...] = lax.dot_general(u_ref[:PEER_SUB, :], hn, nt, preferred_element_type=F32)
    g_prev = _peer_gated(*route, at1_ref, jnp.maximum(2 * e - 1, 0), e > 0)
    total = jnp.dot(vta_ref[...], g_prev, preferred_element_type=F32)
    at1_ref[...] = lax.dot_general(u_ref[PEER_SUB:, :], hn, nt, preferred_element_type=F32)
    g_cur = _peer_gated(*route, at0_ref, jnp.minimum(2 * e, 2 * last - 1), e < last)
    total = total + jnp.dot(vtb_ref[...], g_cur, preferred_element_type=F32)
    acc_ref[...] += total

    @pl.when(e == last)
    def _():
        o_ref[...] = x1_ref[...] + acc_ref[...].T


def _peer_mix(hn, x1, u_tab, v_tab, c1, e1, r2, e2n):
    t = hn.shape[0]
    assert PEER_EXPERTS_STEP == 2 * PEER_SUB
    n_steps = u_tab.shape[0] // PEER_EXPERTS_STEP
    n_chunks = 2 * n_steps
    route = pl.BlockSpec((PEER_HEADS, PEER_KEYS, PEER_TOKENS), lambda i, e: (0, 0, i))
    tok = lambda i, e: (i, 0)
    vt = v_tab.T.astype(BF16)
    return pl.pallas_call(
        _peer_kernel,
        grid=(t // PEER_TOKENS, n_steps + 1),
        in_specs=[pl.BlockSpec((PEER_TOKENS, D_MODEL), tok),
                  pl.BlockSpec((PEER_EXPERTS_STEP, D_MODEL), lambda i, e: (jnp.minimum(e, n_steps - 1), 0)),
                  pl.BlockSpec((D_MODEL, PEER_SUB), lambda i, e: (0, jnp.maximum(2 * e - 1, 0))),
                  pl.BlockSpec((D_MODEL, PEER_SUB), lambda i, e: (0, jnp.minimum(2 * e, n_chunks - 1))),
                  route, route, route, route,
                  pl.BlockSpec((PEER_TOKENS, D_MODEL), tok)],
        out_specs=pl.BlockSpec((PEER_TOKENS, D_MODEL), tok),
        out_shape=jax.ShapeDtypeStruct((t, D_MODEL), F32),
        scratch_shapes=[pltpu.VMEM((D_MODEL, PEER_TOKENS), F32),
                        pltpu.VMEM((PEER_SUB, PEER_TOKENS), F32),
                        pltpu.VMEM((PEER_SUB, PEER_TOKENS), F32)],
        compiler_params=_params(("arbitrary", "arbitrary")),
        name="peer_mix",
    )(hn, u_tab.astype(BF16), vt, vt, c1, e1, r2, e2n, x1)


def kernel(x, norm_mix_g, w_in, q_norm_g, k_norm_g, rel_bias, ssm_lambda_re, ssm_lambda_im, ssm_log_dt,
           ssm_b_re, ssm_b_im, ssm_c_re, ssm_c_im, ssm_d, ssm_glu_w, ssm_glu_b, attn_out_g, ssm_out_g,
           w_out, norm_ffn_g, peer_w_q, peer_keys1, peer_keys2, peer_u, peer_v):
    b, s, d = x.shape
    x2 = x.reshape(b * s, d)
    q, k, v, u = _in_proj(x2, norm_mix_g, w_in, q_norm_g, k_norm_g)
    bias = _bias_tables(rel_bias)
    attn = _attention(q.reshape(b, s, D_ATTN), k.reshape(b, s, D_ATTN), v.reshape(b, s, D_ATTN), bias)
    ssm_n = _s5_mixer(u.reshape(b, s, D_SSM), ssm_lambda_re, ssm_lambda_im, ssm_log_dt, ssm_b_re, ssm_b_im,
                      ssm_c_re, ssm_c_im, ssm_d, ssm_glu_w, ssm_glu_b, ssm_out_g)
    x1, hn = _out_proj(attn.reshape(b * s, D_ATTN), ssm_n.reshape(b * s, D_SSM), x2, attn_out_g, w_out, norm_ffn_g)
    c1, e1, r2, e2n = _peer_route(hn, peer_w_q, peer_keys1, peer_keys2)
    out = _peer_mix(hn, x1, peer_u, peer_v, c1, e1, r2, e2n)
    return out.reshape(b, s, d).astype(x.dtype)
```

```python
import functools
import math

import jax
import jax.numpy as jnp
import numpy as np
from jax import lax
from jax.experimental import pallas as pl
from jax.experimental.pallas import tpu as pltpu

F32 = jnp.float32
BF16 = jnp.bfloat16

D_MODEL = 2048
HEAD_DIM = 64
N_ATTN_HEADS = 16
D_ATTN = N_ATTN_HEADS * HEAD_DIM
SSM_GROUP = 16
N_SSM_GROUPS = 64
D_SSM = N_SSM_GROUPS * SSM_GROUP
SSM_STATE = 64
D_IN_PROJ = 3 * D_ATTN + D_SSM
DILATED_BRANCHES = ((128, 1), (512, 4), (2048, 16))
BLK = 128
N_BUCKETS = 32
MAX_DISTANCE = 2048
PEER_HEADS = 8
PEER_KEYS = 128
PEER_QDIM = 256
PEER_TOPK = 16
EPS = 1e-6
NEG = -1e30

LANES = 128
SUBLANES = 8
VMEM_LIMIT = 56 * 1024 * 1024

ROW_TILE = 256
SSM_CHUNK = 128
SSM_SLAB_GROUPS = LANES // SSM_GROUP
N_SLABS = N_SSM_GROUPS // SSM_SLAB_GROUPS
SLAB_STATE = SSM_SLAB_GROUPS * SSM_STATE
ATTN_LOOKAHEAD = 2
ROUTE_TILE = 256
PEER_TOKENS = 512
PEER_EXPERTS_STEP = 512
PEER_SUB = 256
NO_RANK = 99.0


def _params(sem, vmem=VMEM_LIMIT):
    return pltpu.CompilerParams(dimension_semantics=sem, vmem_limit_bytes=vmem)


def _erf_gelu(x):
    return 0.5 * x * (1.0 + lax.erf(x * math.sqrt(0.5)))


def _t5_bucket(dist):
    max_exact = N_BUCKETS // 2
    n = np.maximum(dist, 0)
    nf = np.maximum(n, 1).astype(np.float32)
    large = max_exact + (np.log(nf / np.float32(max_exact)) / np.float32(math.log(MAX_DISTANCE / max_exact))
                         * np.float32(N_BUCKETS - max_exact)).astype(np.int32)
    large = np.minimum(large, N_BUCKETS - 1)
    return np.where(n < max_exact, n, large)


def _bias_kernel(bkt_ref, rb_ref, out_ref):
    bkt = bkt_ref[0]
    for h in range(N_ATTN_HEADS):
        acc = jnp.zeros((BLK, 2 * BLK), F32)
        for b in range(N_BUCKETS):
            acc = jnp.where(bkt == b, rb_ref[b * N_ATTN_HEADS + h], acc)
        out_ref[0, h] = acc


def _bias_tables(rel_bias):
    qi = np.arange(BLK)[:, None]
    kj = np.arange(2 * BLK)[None, :]
    rel = qi - kj + BLK
    buckets = jnp.asarray(np.stack([_t5_bucket(rel * dil) for _, dil in DILATED_BRANCHES]).astype(np.int32))
    nbr = len(DILATED_BRANCHES)
    return pl.pallas_call(
        _bias_kernel,
        grid=(nbr,),
        in_specs=[pl.BlockSpec((1, BLK, 2 * BLK), lambda i: (i, 0, 0)),
                  pl.BlockSpec(memory_space=pltpu.SMEM)],
        out_specs=pl.BlockSpec((1, N_ATTN_HEADS, BLK, 2 * BLK), lambda i: (i, 0, 0, 0)),
        out_shape=jax.ShapeDtypeStruct((nbr, N_ATTN_HEADS, BLK, 2 * BLK), F32),
        compiler_params=_params(("arbitrary",)),
        name="bias_table",
    )(buckets, rel_bias.astype(F32).reshape(N_BUCKETS * N_ATTN_HEADS))


def _head_rmsnorm(z, gain, ones, scale):
    outs = []
    for c in range(z.shape[1] // LANES):
        zc = z[:, LANES * c:LANES * (c + 1)]
        sq = zc * zc
        hi = sq.astype(BF16)
        lo = (sq - hi.astype(F32)).astype(BF16)
        msq = (jnp.dot(hi, ones, preferred_element_type=F32)
               + jnp.dot(lo, ones, preferred_element_type=F32))
        y = zc * lax.rsqrt(msq + EPS)
        outs.append(y * gain[:, LANES * c:LANES * (c + 1)] * scale)
    return jnp.concatenate(outs, axis=1)


def _inproj_kernel(x_ref, g_ref, w_ref, qg_ref, kg_ref, ones_ref, q_ref, k_ref, v_ref, u_ref):
    x = x_ref[...]
    ms = jnp.mean(x * x, axis=-1, keepdims=True)
    h = (x * lax.rsqrt(ms + EPS) * g_ref[...]).astype(BF16)
    proj = jnp.dot(h, w_ref[...], preferred_element_type=F32)
    ones = ones_ref[...]
    q_ref[...] = _head_rmsnorm(proj[:, :D_ATTN], qg_ref[...], ones, 1.0 / math.sqrt(HEAD_DIM))
    k_ref[...] = _head_rmsnorm(proj[:, D_ATTN:2 * D_ATTN], kg_ref[...], ones, 1.0)
    v_ref[...] = proj[:, 2 * D_ATTN:3 * D_ATTN]
    u_ref[...] = proj[:, 3 * D_ATTN:]


def _in_proj(x2, norm_g, w_in, q_g, k_g):
    t = x2.shape[0]
    head_of_lane = jnp.arange(LANES) // HEAD_DIM
    ones = jnp.where(head_of_lane[:, None] == head_of_lane[None, :], 1.0 / HEAD_DIM, 0.0).astype(BF16)
    qg = jnp.tile(q_g.astype(F32), N_ATTN_HEADS)[None, :]
    kg = jnp.tile(k_g.astype(F32), N_ATTN_HEADS)[None, :]
    row = lambda i: (i, 0)
    fixed = lambda i: (0, 0)
    outs = pl.pallas_call(
        _inproj_kernel,
        grid=(t // ROW_TILE,),
        in_specs=[pl.BlockSpec((ROW_TILE, D_MODEL), row),
                  pl.BlockSpec((1, D_MODEL), fixed),
                  pl.BlockSpec((D_MODEL, D_IN_PROJ), fixed),
                  pl.BlockSpec((1, D_ATTN), fixed),
                  pl.BlockSpec((1, D_ATTN), fixed),
                  pl.BlockSpec((LANES, LANES), fixed)],
        out_specs=[pl.BlockSpec((ROW_TILE, D_ATTN), row)] * 3 + [pl.BlockSpec((ROW_TILE, D_SSM), row)],
        out_shape=[jax.ShapeDtypeStruct((t, D_ATTN), F32)] * 3 + [jax.ShapeDtypeStruct((t, D_SSM), F32)],
        compiler_params=_params(("arbitrary",)),
        name="in_proj",
    )(x2, norm_g.astype(F32)[None, :], w_in.astype(BF16), qg, kg, ones)
    return outs


def _rows(start, size, stride):
    return pl.ds(start, size, stride=stride) if stride > 1 else pl.ds(start, size)


def _attn_scores(q_ref, k_ref, v_ref, bias_ref, blk, tri_cur, tri_prev, head0):
    br, dil, n, r = blk
    qrows = _rows(r + dil * BLK * n, BLK, dil)
    qb = q_ref[qrows, :].astype(BF16)
    if n == 0:
        krows = qrows
        valid = tri_cur
    else:
        krows = _rows(r + dil * BLK * (n - 1), 2 * BLK, dil)
        valid = jnp.concatenate([tri_prev, tri_cur], axis=1)
    kb = k_ref[krows, :].astype(BF16)
    vb = v_ref[krows, :].astype(BF16)
    scores = []
    for h in range(2):
        mine = head0 if h == 0 else jnp.logical_not(head0)
        qh = jnp.where(mine, qb, jnp.zeros_like(qb))
        s = lax.dot_general(qh, kb, (((1,), (1,)), ((), ())), preferred_element_type=F32)
        bias = bias_ref[br, h, :, BLK:] if n == 0 else bias_ref[br, h]
        scores.append(jnp.where(valid, s + bias, NEG))
    return scores, vb, qrows


def _attn_values(scores, vb, qrows, br, pv_ref, den_ref, m_ref, head0):
    ones = jnp.ones_like(vb)
    pv, mx = [], []
    for h in range(2):
        mine = head0 if h == 0 else jnp.logical_not(head0)
        m = jnp.max(scores[h], axis=-1, keepdims=True)
        p = jnp.exp(scores[h] - m).astype(BF16)
        pv.append(jnp.dot(p, jnp.where(mine, vb, ones), preferred_element_type=F32))
        mx.append(m)
    pv_ref[br, qrows, :] = jnp.where(head0, pv[0], pv[1])
    den_ref[br, qrows, :] = jnp.where(head0, pv[1], pv[0])
    m_ref[br, qrows, :] = jnp.where(head0, mx[0], mx[1])


def _attn_kernel(q_ref, k_ref, v_ref, bias_ref, o_ref, pv_ref, den_ref, m_ref, *, seq):
    qi = lax.broadcasted_iota(jnp.int32, (BLK, BLK), 0)
    kj = lax.broadcasted_iota(jnp.int32, (BLK, BLK), 1)
    tri_cur = qi >= kj
    tri_prev = kj >= qi
    head0 = lax.broadcasted_iota(jnp.int32, (1, LANES), 1) < HEAD_DIM
    nbr = len(DILATED_BRANCHES)
    blocks = []
    for br, (window, dil) in enumerate(DILATED_BRANCHES):
        assert window // dil == BLK
        blocks += [(br, dil, n, r) for r in range(dil) for n in range(seq // dil // BLK)]
    ahead = [_attn_scores(q_ref, k_ref, v_ref, bias_ref, b, tri_cur, tri_prev, head0)
             for b in blocks[:ATTN_LOOKAHEAD]]
    for i, blk in enumerate(blocks):
        if i + ATTN_LOOKAHEAD < len(blocks):
            ahead.append(_attn_scores(q_ref, k_ref, v_ref, bias_ref, blocks[i + ATTN_LOOKAHEAD],
                                      tri_cur, tri_prev, head0))
        _attn_values(*ahead.pop(0), blk[0], pv_ref, den_ref, m_ref, head0)
    m_all = [m_ref[br] for br in range(nbr)]
    m_top = functools.reduce(jnp.maximum, m_all)
    num = jnp.zeros((seq, LANES), F32)
    den = jnp.zeros((seq, LANES), F32)
    for br in range(nbr):
        w = jnp.exp(m_all[br] - m_top)
        num = num + w * pv_ref[br]
        den = den + w * pltpu.roll(den_ref[br], HEAD_DIM, axis=1)
    o_ref[...] = num / den


def _attention(q, k, v, bias):
    b, s, _ = q.shape
    blk = pl.BlockSpec((None, s, LANES), lambda i, p: (i, 0, p))
    nbr = len(DILATED_BRANCHES)
    return pl.pallas_call(
        functools.partial(_attn_kernel, seq=s),
        grid=(b, D_ATTN // LANES),
        in_specs=[blk, blk, blk,
                  pl.BlockSpec((nbr, 2, BLK, 2 * BLK), lambda i, p: (0, p, 0, 0))],
        out_specs=blk,
        out_shape=jax.ShapeDtypeStruct((b, s, D_ATTN), F32),
        scratch_shapes=[pltpu.VMEM((nbr, s, LANES), F32)] * 3,
        compiler_params=_params(("arbitrary", "arbitrary")),
        name="attention",
    )(q, k, v, bias)


def _zoh_kernel(lr_ref, li_ref, dt_ref, lrr_ref, lir_ref, br_ref, bi_ref,
                are_ref, aim_ref, bbr_ref, bbi_ref):
    dt = jnp.exp(dt_ref[...])

    def zoh(lr, li):
        mag = jnp.exp(lr * dt)
        a_re, a_im = mag * jnp.cos(li * dt), mag * jnp.sin(li * dt)
        den = lr * lr + li * li
        f_re = ((a_re - 1.0) * lr + a_im * li) / den
        f_im = (a_im * lr - (a_re - 1.0) * li) / den
        return a_re, a_im, f_re, f_im

    a_re, a_im, _, _ = zoh(lr_ref[...], li_ref[...])
    are_ref[...] = a_re
    aim_ref[...] = a_im
    _, _, f_re, f_im = zoh(lrr_ref[...], lir_ref[...])
    br, bi = br_ref[...], bi_ref[...]
    bbr_ref[...] = f_re * br - f_im * bi
    bbi_ref[...] = f_re * bi + f_im * br


def _ssm_zoh(lam_re, lam_im, log_dt, b_re, b_im):
    g, n, c = b_re.shape
    rep = lambda a: jnp.repeat(a.astype(F32), c, axis=1)
    a_re, a_im, bb_re, bb_im = pl.pallas_call(
        _zoh_kernel,
        out_shape=[jax.ShapeDtypeStruct((g, n), F32)] * 2 + [jax.ShapeDtypeStruct((g, n * c), F32)] * 2,
        name="ssm_zoh",
    )(lam_re.astype(F32), lam_im.astype(F32), log_dt.astype(F32)[:, None], rep(lam_re), rep(lam_im),
      b_re.astype(F32).reshape(g, n * c), b_im.astype(F32).reshape(g, n * c))
    return a_re, a_im, bb_re.reshape(g, n, c), bb_im.reshape(g, n, c)


def _ssm_kernel(u_ref, wb_ref, wc_ref, are_ref, aim_ref, d_ref, gw_ref, gb_ref, gain_ref,
                o_ref, bu_ref, y_ref, sr_ref, si_ref, *, nb, chunk):
    half = N_SLABS // 2
    seqs = 2 * nb
    lane_blocks = 2 * SLAB_STATE // LANES

    @pl.when(pl.program_id(0) == 0)
    def _():
        sr_ref[...] = jnp.zeros_like(sr_ref)
        si_ref[...] = jnp.zeros_like(si_ref)

    for b in range(nb):
        for m in range(N_SLABS):
            gh, mp = divmod(m, half)
            ub = u_ref[b, :, LANES * m:LANES * (m + 1)].astype(BF16)
            bu = jnp.dot(ub, wb_ref[m], preferred_element_type=F32)
            for j in range(lane_blocks):
                bu_ref[lane_blocks * mp + j, pl.ds(gh * nb + b, chunk, stride=seqs), :] = (
                    bu[:, LANES * j:LANES * (j + 1)])

    def load_state(rows, mp, part):
        j0 = lane_blocks * mp + part * (lane_blocks // 2)
        return jnp.concatenate([bu_ref[j0 + j, rows, :] for j in range(lane_blocks // 2)], axis=1)

    def store_state(rows, mp, part, val):
        j0 = lane_blocks * mp + part * (lane_blocks // 2)
        for j in range(lane_blocks // 2):
            bu_ref[j0 + j, rows, :] = val[:, LANES * j:LANES * (j + 1)]

    def step(t, carry):
        base = pl.multiple_of(t * seqs, seqs)
        rows = pl.ds(base, seqs)
        new = []
        for mp in range(half):
            xr, xi = carry[2 * mp], carry[2 * mp + 1]
            ar = are_ref[:, SLAB_STATE * mp:SLAB_STATE * (mp + 1)]
            ai = aim_ref[:, SLAB_STATE * mp:SLAB_STATE * (mp + 1)]
            nr = ar * xr - ai * xi + load_state(rows, mp, 0)
            ni = ar * xi + ai * xr + load_state(rows, mp, 1)
            store_state(rows, mp, 0, nr)
            store_state(rows, mp, 1, ni)
            new += [nr, ni]
        return tuple(new)

    init = []
    for mp in range(half):
        init += [sr_ref[:, SLAB_STATE * mp:SLAB_STATE * (mp + 1)], si_ref[:, SLAB_STATE * mp:SLAB_STATE * (mp + 1)]]
    final = lax.fori_loop(0, chunk, step, tuple(init))
    for mp in range(half):
        sr_ref[:, SLAB_STATE * mp:SLAB_STATE * (mp + 1)] = final[2 * mp]
        si_ref[:, SLAB_STATE * mp:SLAB_STATE * (mp + 1)] = final[2 * mp + 1]

    for b in range(nb):
        for m in range(N_SLABS):
            gh, mp = divmod(m, half)
            xs = jnp.concatenate([bu_ref[lane_blocks * mp + j, pl.ds(gh * nb + b, chunk, stride=seqs), :]
                                  for j in range(lane_blocks)], axis=1)
            y = jnp.dot(xs.astype(BF16), wc_ref[m], preferred_element_type=F32)
            cols = slice(LANES * m, LANES * (m + 1))
            y_ref[b * chunk:(b + 1) * chunk, cols] = y + d_ref[:, cols] * u_ref[b, :, cols]

    y = _erf_gelu(y_ref[...])
    z = jnp.dot(y.astype(BF16), gw_ref[...], preferred_element_type=F32) + gb_ref[...]
    y = y * jax.nn.sigmoid(z)
    ms = jnp.mean(y * y, axis=-1, keepdims=True)
    yn = y * lax.rsqrt(ms + EPS) * gain_ref[...]
    for b in range(nb):
        o_ref[b] = yn[b * chunk:(b + 1) * chunk].astype(o_ref.dtype)


def _s5_mixer(u, lam_re, lam_im, log_dt, b_re, b_im, c_re, c_im, d_skip, glu_w, glu_b, out_gain):
    nb, s, _ = u.shape
    a_re, a_im, bb_re, bb_im = _ssm_zoh(lam_re, lam_im, log_dt, b_re, b_im)
    eye = jnp.eye(SSM_SLAB_GROUPS, dtype=F32)

    def in_slab(bb):
        w = jnp.einsum('mgnc,gh->mgchn', bb.reshape(N_SLABS, SSM_SLAB_GROUPS, SSM_STATE, SSM_GROUP), eye)
        return w.reshape(N_SLABS, LANES, SLAB_STATE)

    def out_slab(cc):
        w = jnp.einsum('mgcn,gh->mgnhc', cc.reshape(N_SLABS, SSM_SLAB_GROUPS, SSM_GROUP, SSM_STATE), eye)
        return w.reshape(N_SLABS, SLAB_STATE, LANES)

    wb = jnp.concatenate([in_slab(bb_re), in_slab(bb_im)], axis=2).astype(BF16)
    wc = jnp.concatenate([out_slab(c_re.astype(F32)), -out_slab(c_im.astype(F32))], axis=1).astype(BF16)
    half_states = (N_SSM_GROUPS // 2) * SSM_STATE

    def seq_rows(a):
        return jnp.repeat(a.reshape(2, half_states), nb, axis=0)

    fixed2 = lambda c: (0, 0)
    fixed3 = lambda c: (0, 0, 0)
    chunk = SSM_CHUNK
    return pl.pallas_call(
        functools.partial(_ssm_kernel, nb=nb, chunk=chunk),
        grid=(s // chunk,),
        in_specs=[pl.BlockSpec((nb, chunk, D_SSM), lambda c: (0, c, 0)),
                  pl.BlockSpec((N_SLABS, LANES, 2 * SLAB_STATE), fixed3),
                  pl.BlockSpec((N_SLABS, 2 * SLAB_STATE, LANES), fixed3),
                  pl.BlockSpec((2 * nb, half_states), fixed2),
                  pl.BlockSpec((2 * nb, half_states), fixed2),
                  pl.BlockSpec((1, D_SSM), fixed2),
                  pl.BlockSpec((D_SSM, D_SSM), fixed2),
                  pl.BlockSpec((1, D_SSM), fixed2),
                  pl.BlockSpec((1, D_SSM), fixed2)],
        out_specs=pl.BlockSpec((nb, chunk, D_SSM), lambda c: (0, c, 0)),
        out_shape=jax.ShapeDtypeStruct((nb, s, D_SSM), BF16),
        scratch_shapes=[pltpu.VMEM((2 * half_states // LANES, 2 * nb * chunk, LANES), F32),
                        pltpu.VMEM((nb * chunk, D_SSM), F32),
                        pltpu.VMEM((2 * nb, half_states), F32),
                        pltpu.VMEM((2 * nb, half_states), F32)],
        compiler_params=_params(("arbitrary",)),
        name="ssm",
    )(u, wb, wc, seq_rows(a_re), seq_rows(a_im), d_skip.astype(F32).reshape(1, D_SSM),
      glu_w.astype(BF16), glu_b.astype(F32)[None, :], out_gain.astype(F32)[None, :])


def _outproj_kernel(a_ref, s_ref, x_ref, ag_ref, wa_ref, ws_ref, fg_ref, x1_ref, hn_ref):
    a = a_ref[...]
    ms = jnp.mean(a * a, axis=-1, keepdims=True)
    an = (a * lax.rsqrt(ms + EPS) * ag_ref[...]).astype(BF16)
    mixed = (jnp.dot(an, wa_ref[...], preferred_element_type=F32)
             + jnp.dot(s_ref[...], ws_ref[...], preferred_element_type=F32))
    x1 = x_ref[...] + mixed
    x1_ref[...] = x1
    ms1 = jnp.mean(x1 * x1, axis=-1, keepdims=True)
    hn_ref[...] = (x1 * lax.rsqrt(ms1 + EPS) * fg_ref[...]).astype(BF16)


def _out_proj(attn, ssm_n, x2, attn_g, w_out, ffn_g):
    t = x2.shape[0]
    row = lambda i: (i, 0)
    fixed = lambda i: (0, 0)
    w = w_out.astype(BF16)
    return pl.pallas_call(
        _outproj_kernel,
        grid=(t // ROW_TILE,),
        in_specs=[pl.BlockSpec((ROW_TILE, D_ATTN), row),
                  pl.BlockSpec((ROW_TILE, D_SSM), row),
                  pl.BlockSpec((ROW_TILE, D_MODEL), row),
                  pl.BlockSpec((1, D_ATTN), fixed),
                  pl.BlockSpec((D_ATTN, D_MODEL), fixed),
                  pl.BlockSpec((D_SSM, D_MODEL), fixed),
                  pl.BlockSpec((1, D_MODEL), fixed)],
        out_specs=[pl.BlockSpec((ROW_TILE, D_MODEL), row)] * 2,
        out_shape=[jax.ShapeDtypeStruct((t, D_MODEL), F32), jax.ShapeDtypeStruct((t, D_MODEL), BF16)],
        compiler_params=_params(("arbitrary",)),
        name="out_proj",
    )(attn, ssm_n, x2, attn_g.astype(F32)[None, :], w[:D_ATTN], w[D_ATTN:], ffn_g.astype(F32)[None, :])


def _top16(s, exact):
    iota = lax.broadcasted_iota(jnp.int32, s.shape, 0)
    rank = jnp.full(s.shape, NO_RANK, F32)
    vals = []
    for it in range(PEER_TOPK):
        m = jnp.max(s, axis=0, keepdims=True)
        if exact:
            idx = jnp.min(jnp.where(s == m, iota, s.shape[0]), axis=0, keepdims=True)
            hit = iota == idx
        else:
            hit = s == m
        rank = jnp.where(hit, float(it), rank)
        s = jnp.where(hit, -jnp.inf, s)
        vals.append(m)
    return vals, rank


def _stack_rows(rows, n):
    iota = lax.broadcasted_iota(jnp.int32, (n, rows[0].shape[1]), 0)
    out = jnp.zeros((n, rows[0].shape[1]), F32)
    for i, r in enumerate(rows):
        out = jnp.where(iota == i, r, out)
    return out


def _count(mask):
    return jnp.sum(jnp.where(mask, 1.0, 0.0), axis=0, keepdims=True)


def _route_head(s1, s2, exact):
    tb = s1.shape[1]
    v1, rank1 = _top16(s1, exact)
    v2, rank2 = _top16(s2, exact)
    v2_all = _stack_rows(v2, PEER_TOPK)
    v1_hi = _stack_rows(v1[SUBLANES:], SUBLANES)
    sub = lax.broadcasted_iota(jnp.int32, (SUBLANES, tb), 0)
    pieces = [v1[0] + v2_all]
    flats = [lax.broadcasted_iota(jnp.int32, (PEER_TOPK, tb), 0)]
    for a in range(1, SUBLANES):
        limit = PEER_TOPK // (a + 1)
        pieces.append(jnp.where(sub < limit, v1[a] + v2_all[:SUBLANES], -jnp.inf))
        flats.append(a * PEER_TOPK + sub)
    pieces.append(v1_hi + v2[0])
    flats.append((sub + SUBLANES) * PEER_TOPK)
    cand0 = jnp.concatenate(pieces, axis=0)
    flat = jnp.concatenate(flats, axis=0)
    cand = cand0
    sel = jnp.zeros(cand.shape, F32)
    for _ in range(PEER_TOPK):
        m = jnp.max(cand, axis=0, keepdims=True)
        if exact:
            idx = jnp.min(jnp.where(cand == m, flat, PEER_TOPK * PEER_TOPK), axis=0, keepdims=True)
            hit = flat == idx
        else:
            hit = cand == m
        sel = jnp.where(hit, 1.0, sel)
        cand = jnp.where(hit, -jnp.inf, cand)
    top = v1[0] + v2[0]
    z = jnp.sum(sel * jnp.exp(jnp.where(sel > 0, cand0, top) - top), axis=0, keepdims=True)
    cnt = [jnp.sum(sel[:PEER_TOPK], axis=0, keepdims=True)]
    for a in range(1, SUBLANES):
        lo = PEER_TOPK + SUBLANES * (a - 1)
        cnt.append(jnp.sum(sel[lo:lo + SUBLANES], axis=0, keepdims=True))
    lo = PEER_TOPK + SUBLANES * (SUBLANES - 1)
    for i in range(SUBLANES):
        cnt.append(sel[lo + i:lo + i + 1])
    c1 = jnp.zeros(rank1.shape, F32)
    for a in range(PEER_TOPK):
        c1 = jnp.where(rank1 == float(a), cnt[a], c1)
    e1 = jnp.exp(s1 - v1[0])
    e2n = jnp.exp(s2 - v2[0]) / z
    k = float(PEER_TOPK)
    taken = jnp.maximum(jnp.maximum(_count(rank1 < k), _count(rank2 < k)), jnp.sum(sel, axis=0, keepdims=True))
    return c1, e1, rank2, e2n, taken


def _route_heads(qt_ref, k1_ref, k2_ref, c1_ref, e1_ref, r2_ref, e2_ref, exact):
    half = PEER_QDIM // 2
    taken = None
    for h in range(PEER_HEADS):
        q1 = qt_ref[PEER_QDIM * h:PEER_QDIM * h + half, :].astype(BF16)
        q2 = qt_ref[PEER_QDIM * h + half:PEER_QDIM * (h + 1), :].astype(BF16)
        s1 = jnp.dot(k1_ref[h], q1, preferred_element_type=F32)
        s2 = jnp.dot(k2_ref[h], q2, preferred_element_type=F32)
        c1, e1, r2, e2n, t = _route_head(s1, s2, exact)
        c1_ref[h] = c1
        e1_ref[h] = e1
        r2_ref[h] = r2.astype(r2_ref.dtype)
        e2_ref[h] = e2n.astype(e2_ref.dtype)
        taken = t if taken is None else jnp.maximum(taken, t)
    return taken


def _route_kernel(hn_ref, wq_ref, k1_ref, k2_ref, c1_ref, e1_ref, r2_ref, e2_ref, qt_ref):
    qt_ref[...] = lax.dot_general(wq_ref[...], hn_ref[...], (((1,), (1,)), ((), ())), preferred_element_type=F32)
    refs = (qt_ref, k1_ref, k2_ref, c1_ref, e1_ref, r2_ref, e2_ref)
    taken = _route_heads(*refs, exact=False)

    @pl.when(jnp.max(taken) > float(PEER_TOPK))
    def _():
        _route_heads(*refs, exact=True)


def _peer_route(hn, w_q, keys1, keys2):
    t = hn.shape[0]
    blk = pl.BlockSpec((PEER_HEADS, PEER_KEYS, ROUTE_TILE), lambda i: (0, 0, i))
    fixed3 = lambda i: (0, 0, 0)
    shape = (PEER_HEADS, PEER_KEYS, t)
    return pl.pallas_call(
        _route_kernel,
        grid=(t // ROUTE_TILE,),
        in_specs=[pl.BlockSpec((ROUTE_TILE, D_MODEL), lambda i: (i, 0)),
                  pl.BlockSpec((PEER_HEADS * PEER_QDIM, D_MODEL), lambda i: (0, 0)),
                  pl.BlockSpec((PEER_HEADS, PEER_KEYS, PEER_QDIM // 2), fixed3),
                  pl.BlockSpec((PEER_HEADS, PEER_KEYS, PEER_QDIM // 2), fixed3)],
        out_specs=[blk] * 4,
        out_shape=[jax.ShapeDtypeStruct(shape, F32), jax.ShapeDtypeStruct(shape, F32),
                   jax.ShapeDtypeStruct(shape, BF16), jax.ShapeDtypeStruct(shape, BF16)],
        scratch_shapes=[pltpu.VMEM((PEER_HEADS * PEER_QDIM, ROUTE_TILE), F32)],
        compiler_params=_params(("arbitrary",)),
        name="peer_route",
    )(hn, w_q.T.astype(BF16), keys1.astype(BF16), keys2.astype(BF16))


def _peer_gated(c1_ref, e1_ref, r2_ref, e2_ref, at_ref, chunk, valid):
    packed = 2 * SUBLANES
    tiles = PEER_SUB // PEER_KEYS
    gs = []
    for tl in range(tiles):
        tile = chunk * tiles + tl
        gate = jnp.zeros((PEER_KEYS // packed, packed, PEER_TOKENS), BF16)
        for h in range(PEER_HEADS):
            c1 = jnp.broadcast_to(c1_ref[h, pl.ds(tile, 1), :], (packed, PEER_TOKENS)).astype(BF16)
            e1 = jnp.broadcast_to(e1_ref[h, pl.ds(tile, 1), :], (packed, PEER_TOKENS)).astype(BF16)
            r2 = r2_ref[h].reshape(gate.shape)
            e2 = e2_ref[h].reshape(gate.shape)
            gate = gate + jnp.where(r2 < c1[None], e1[None] * e2, jnp.zeros_like(e2))
        a = at_ref[PEER_KEYS * tl:PEER_KEYS * (tl + 1), :].astype(BF16)
        gs.append(gate.reshape(PEER_KEYS, PEER_TOKENS) * _erf_gelu(a))
    g = jnp.concatenate(gs, axis=0)
    return jnp.where(valid, g, jnp.zeros_like(g))


def _peer_kernel(hn_ref, u_ref, vta_ref, vtb_ref, c1_ref, e1_ref, r2_ref, e2_ref, x1_ref, o_ref,
                 acc_ref, at0_ref, at1_ref):
    e = pl.program_id(1)
    last = pl.num_programs(1) - 1
    nt = (((1,), (1,)), ((), ()))

    @pl.when(e == 0)
    def _():
        acc_ref[...] = jnp.zeros_like(acc_ref)
        at1_ref[...] = jnp.zeros_like(at1_ref)

    hn = hn_ref[...]
    route = (c1_ref, e1_ref, r2_ref, e2_ref)
    at0_ref[...] = lax.dot_general(u_ref[:PEER_SUB, :], hn, nt, preferred_element_type=F32)
    g_prev = _peer_gated(*route, at1_ref, jnp.maximum(2 * e - 1, 0), e > 0)
    total = jnp.dot(vta_ref[...], g_prev, preferred_element_type=F32)
    at1_ref[...] = lax.dot_general(u_ref[PEER_SUB:, :], hn, nt, preferred_element_type=F32)
    g_cur = _peer_gated(*route, at0_ref, jnp.minimum(2 * e, 2 * last - 1), e < last)
    total = total + jnp.dot(vtb_ref[...], g_cur, preferred_element_type=F32)
    acc_ref[...] += total

    @pl.when(e == last)
    def _():
        o_ref[...] = x1_ref[...] + acc_ref[...].T


def _peer_mix(hn, x1, u_tab, v_tab, c1, e1, r2, e2n):
    t = hn.shape[0]
    assert PEER_EXPERTS_STEP == 2 * PEER_SUB
    n_steps = u_tab.shape[0] // PEER_EXPERTS_STEP
    n_chunks = 2 * n_steps
    route = pl.BlockSpec((PEER_HEADS, PEER_KEYS, PEER_TOKENS), lambda i, e: (0, 0, i))
    tok = lambda i, e: (i, 0)
    vt = v_tab.T.astype(BF16)
    return pl.pallas_call(
        _peer_kernel,
        grid=(t // PEER_TOKENS, n_steps + 1),
        in_specs=[pl.BlockSpec((PEER_TOKENS, D_MODEL), tok),
                  pl.BlockSpec((PEER_EXPERTS_STEP, D_MODEL), lambda i, e: (jnp.minimum(e, n_steps - 1), 0)),
                  pl.BlockSpec((D_MODEL, PEER_SUB), lambda i, e: (0, jnp.maximum(2 * e - 1, 0))),
                  pl.BlockSpec((D_MODEL, PEER_SUB), lambda i, e: (0, jnp.minimum(2 * e, n_chunks - 1))),
                  route, route, route, route,
                  pl.BlockSpec((PEER_TOKENS, D_MODEL), tok)],
        out_specs=pl.BlockSpec((PEER_TOKENS, D_MODEL), tok),
        out_shape=jax.ShapeDtypeStruct((t, D_MODEL), F32),
        scratch_shapes=[pltpu.VMEM((D_MODEL, PEER_TOKENS), F32),
                        pltpu.VMEM((PEER_SUB, PEER_TOKENS), F32),
                        pltpu.VMEM((PEER_SUB, PEER_TOKENS), F32)],
        compiler_params=_params(("arbitrary", "arbitrary")),
        name="peer_mix",
    )(hn, u_tab.astype(BF16), vt, vt, c1, e1, r2, e2n, x1)


def kernel(x, norm_mix_g, w_in, q_norm_g, k_norm_g, rel_bias, ssm_lambda_re, ssm_lambda_im, ssm_log_dt,
           ssm_b_re, ssm_b_im, ssm_c_re, ssm_c_im, ssm_d, ssm_glu_w, ssm_glu_b, attn_out_g, ssm_out_g,
           w_out, norm_ffn_g, peer_w_q, peer_keys1, peer_keys2, peer_u, peer_v):
    b, s, d = x.shape
    x2 = x.reshape(b * s, d)
    q, k, v, u = _in_proj(x2, norm_mix_g, w_in, q_norm_g, k_norm_g)
    bias = _bias_tables(rel_bias)
    attn = _attention(q.reshape(b, s, D_ATTN), k.reshape(b, s, D_ATTN), v.reshape(b, s, D_ATTN), bias)
    ssm_n = _s5_mixer(u.reshape(b, s, D_SSM), ssm_lambda_re, ssm_lambda_im, ssm_log_dt, ssm_b_re, ssm_b_im,
                      ssm_c_re, ssm_c_im, ssm_d, ssm_glu_w, ssm_glu_b, ssm_out_g)
    x1, hn = _out_proj(attn.reshape(b * s, D_ATTN), ssm_n.reshape(b * s, D_SSM), x2, attn_out_g, w_out, norm_ffn_g)
    c1, e1, r2, e2n = _peer_route(hn, peer_w_q, peer_keys1, peer_keys2)
    out = _peer_mix(hn, x1, peer_u, peer_v, c1, e1, r2, e2n)
    return out.reshape(b, s, d).astype(x.dtype)
```

```python
import functools
import math

import jax
import jax.numpy as jnp
import numpy as np
from jax import lax
from jax.experimental import pallas as pl
from jax.experimental.pallas import tpu as pltpu

F32 = jnp.float32
BF16 = jnp.bfloat16

D_MODEL = 2048
HEAD_DIM = 64
N_ATTN_HEADS = 16
D_ATTN = N_ATTN_HEADS * HEAD_DIM
SSM_GROUP = 16
N_SSM_GROUPS = 64
D_SSM = N_SSM_GROUPS * SSM_GROUP
SSM_STATE = 64
D_IN_PROJ = 3 * D_ATTN + D_SSM
DILATED_BRANCHES = ((128, 1), (512, 4), (2048, 16))
BLK = 128
N_BUCKETS = 32
MAX_DISTANCE = 2048
PEER_HEADS = 8
PEER_KEYS = 128
PEER_QDIM = 256
PEER_TOPK = 16
EPS = 1e-6
NEG = -1e30

LANES = 128
SUBLANES = 8
VMEM_LIMIT = 56 * 1024 * 1024

ROW_TILE = 256
SSM_CHUNK = 128
SSM_SLAB_GROUPS = LANES // SSM_GROUP
N_SLABS = N_SSM_GROUPS // SSM_SLAB_GROUPS
SLAB_STATE = SSM_SLAB_GROUPS * SSM_STATE
ATTN_LOOKAHEAD = 2
ROUTE_TILE = 256
PEER_TOKENS = 512
PEER_EXPERTS_STEP = 1024
PEER_SUB = 256
NO_RANK = 99.0


def _params(sem, vmem=VMEM_LIMIT):
    return pltpu.CompilerParams(dimension_semantics=sem, vmem_limit_bytes=vmem)


def _erf_gelu(x):
    return 0.5 * x * (1.0 + lax.erf(x * math.sqrt(0.5)))


def _t5_bucket(dist):
    max_exact = N_BUCKETS // 2
    n = np.maximum(dist, 0)
    nf = np.maximum(n, 1).astype(np.float32)
    large = max_exact + (np.log(nf / np.float32(max_exact)) / np.float32(math.log(MAX_DISTANCE / max_exact))
                         * np.float32(N_BUCKETS - max_exact)).astype(np.int32)
    large = np.minimum(large, N_BUCKETS - 1)
    return np.where(n < max_exact, n, large)


def _bias_kernel(bkt_ref, rb_ref, out_ref):
    bkt = bkt_ref[0]
    for h in range(N_ATTN_HEADS):
        acc = jnp.zeros((BLK, 2 * BLK), F32)
        for b in range(N_BUCKETS):
            acc = jnp.where(bkt == b, rb_ref[b * N_ATTN_HEADS + h], acc)
        out_ref[0, h] = acc


def _bias_tables(rel_bias):
    qi = np.arange(BLK)[:, None]
    kj = np.arange(2 * BLK)[None, :]
    rel = qi - kj + BLK
    buckets = jnp.asarray(np.stack([_t5_bucket(rel * dil) for _, dil in DILATED_BRANCHES]).astype(np.int32))
    nbr = len(DILATED_BRANCHES)
    return pl.pallas_call(
        _bias_kernel,
        grid=(nbr,),
        in_specs=[pl.BlockSpec((1, BLK, 2 * BLK), lambda i: (i, 0, 0)),
                  pl.BlockSpec(memory_space=pltpu.SMEM)],
        out_specs=pl.BlockSpec((1, N_ATTN_HEADS, BLK, 2 * BLK), lambda i: (i, 0, 0, 0)),
        out_shape=jax.ShapeDtypeStruct((nbr, N_ATTN_HEADS, BLK, 2 * BLK), F32),
        compiler_params=_params(("arbitrary",)),
        name="bias_table",
    )(buckets, rel_bias.astype(F32).reshape(N_BUCKETS * N_ATTN_HEADS))


def _head_rmsnorm(z, gain, ones, scale):
    outs = []
    for c in range(z.shape[1] // LANES):
        zc = z[:, LANES * c:LANES * (c + 1)]
        sq = zc * zc
        hi = sq.astype(BF16)
        lo = (sq - hi.astype(F32)).astype(BF16)
        msq = (jnp.dot(hi, ones, preferred_element_type=F32)
               + jnp.dot(lo, ones, preferred_element_type=F32))
        y = zc * lax.rsqrt(msq + EPS)
        outs.append(y * gain[:, LANES * c:LANES * (c + 1)] * scale)
    return jnp.concatenate(outs, axis=1)


def _inproj_kernel(x_ref, g_ref, w_ref, qg_ref, kg_ref, ones_ref, q_ref, k_ref, v_ref, u_ref):
    x = x_ref[...]
    ms = jnp.mean(x * x, axis=-1, keepdims=True)
    h = (x * lax.rsqrt(ms + EPS) * g_ref[...]).astype(BF16)
    proj = jnp.dot(h, w_ref[...], preferred_element_type=F32)
    ones = ones_ref[...]
    q_ref[...] = _head_rmsnorm(proj[:, :D_ATTN], qg_ref[...], ones, 1.0 / math.sqrt(HEAD_DIM))
    k_ref[...] = _head_rmsnorm(proj[:, D_ATTN:2 * D_ATTN], kg_ref[...], ones, 1.0)
    v_ref[...] = proj[:, 2 * D_ATTN:3 * D_ATTN]
    u_ref[...] = proj[:, 3 * D_ATTN:]


def _in_proj(x2, norm_g, w_in, q_g, k_g):
    t = x2.shape[0]
    head_of_lane = jnp.arange(LANES) // HEAD_DIM
    ones = jnp.where(head_of_lane[:, None] == head_of_lane[None, :], 1.0 / HEAD_DIM, 0.0).astype(BF16)
    qg = jnp.tile(q_g.astype(F32), N_ATTN_HEADS)[None, :]
    kg = jnp.tile(k_g.astype(F32), N_ATTN_HEADS)[None, :]
    row = lambda i: (i, 0)
    fixed = lambda i: (0, 0)
    outs = pl.pallas_call(
        _inproj_kernel,
        grid=(t // ROW_TILE,),
        in_specs=[pl.BlockSpec((ROW_TILE, D_MODEL), row),
                  pl.BlockSpec((1, D_MODEL), fixed),
                  pl.BlockSpec((D_MODEL, D_IN_PROJ), fixed),
                  pl.BlockSpec((1, D_ATTN), fixed),
                  pl.BlockSpec((1, D_ATTN), fixed),
                  pl.BlockSpec((LANES, LANES), fixed)],
        out_specs=[pl.BlockSpec((ROW_TILE, D_ATTN), row)] * 3 + [pl.BlockSpec((ROW_TILE, D_SSM), row)],
        out_shape=[jax.ShapeDtypeStruct((t, D_ATTN), F32)] * 3 + [jax.ShapeDtypeStruct((t, D_SSM), F32)],
        compiler_params=_params(("arbitrary",)),
        name="in_proj",
    )(x2, norm_g.astype(F32)[None, :], w_in.astype(BF16), qg, kg, ones)
    return outs


def _rows(start, size, stride):
    return pl.ds(start, size, stride=stride) if stride > 1 else pl.ds(start, size)


def _attn_scores(q_ref, k_ref, v_ref, bias_ref, blk, tri_cur, tri_prev, head0):
    br, dil, n, r = blk
    qrows = _rows(r + dil * BLK * n, BLK, dil)
    qb = q_ref[qrows, :].astype(BF16)
    if n == 0:
        krows = qrows
        valid = tri_cur
    else:
        krows = _rows(r + dil * BLK * (n - 1), 2 * BLK, dil)
        valid = jnp.concatenate([tri_prev, tri_cur], axis=1)
    kb = k_ref[krows, :].astype(BF16)
    vb = v_ref[krows, :].astype(BF16)
    scores = []
    for h in range(2):
        mine = head0 if h == 0 else jnp.logical_not(head0)
        qh = jnp.where(mine, qb, jnp.zeros_like(qb))
        s = lax.dot_general(qh, kb, (((1,), (1,)), ((), ())), preferred_element_type=F32)
        bias = bias_ref[br, h, :, BLK:] if n == 0 else bias_ref[br, h]
        scores.append(jnp.where(valid, s + bias, NEG))
    return scores, vb, qrows


def _attn_values(scores, vb, qrows, br, pv_ref, den_ref, m_ref, head0):
    ones = jnp.ones_like(vb)
    pv, mx = [], []
    for h in range(2):
        mine = head0 if h == 0 else jnp.logical_not(head0)
        m = jnp.max(scores[h], axis=-1, keepdims=True)
        p = jnp.exp(scores[h] - m).astype(BF16)
        pv.append(jnp.dot(p, jnp.where(mine, vb, ones), preferred_element_type=F32))
        mx.append(m)
    pv_ref[br, qrows, :] = jnp.where(head0, pv[0], pv[1])
    den_ref[br, qrows, :] = jnp.where(head0, pv[1], pv[0])
    m_ref[br, qrows, :] = jnp.where(head0, mx[0], mx[1])


def _attn_kernel(q_ref, k_ref, v_ref, bias_ref, o_ref, pv_ref, den_ref, m_ref, *, seq):
    qi = lax.broadcasted_iota(jnp.int32, (BLK, BLK), 0)
    kj = lax.broadcasted_iota(jnp.int32, (BLK, BLK), 1)
    tri_cur = qi >= kj
    tri_prev = kj >= qi
    head0 = lax.broadcasted_iota(jnp.int32, (1, LANES), 1) < HEAD_DIM
    nbr = len(DILATED_BRANCHES)
    blocks = []
    for br, (window, dil) in enumerate(DILATED_BRANCHES):
        assert window // dil == BLK
        blocks += [(br, dil, n, r) for r in range(dil) for n in range(seq // dil // BLK)]
    ahead = [_attn_scores(q_ref, k_ref, v_ref, bias_ref, b, tri_cur, tri_prev, head0)
             for b in blocks[:ATTN_LOOKAHEAD]]
    for i, blk in enumerate(blocks):
        if i + ATTN_LOOKAHEAD < len(blocks):
            ahead.append(_attn_scores(q_ref, k_ref, v_ref, bias_ref, blocks[i + ATTN_LOOKAHEAD],
                                      tri_cur, tri_prev, head0))
        _attn_values(*ahead.pop(0), blk[0], pv_ref, den_ref, m_ref, head0)
    m_all = [m_ref[br] for br in range(nbr)]
    m_top = functools.reduce(jnp.maximum, m_all)
    num = jnp.zeros((seq, LANES), F32)
    den = jnp.zeros((seq, LANES), F32)
    for br in range(nbr):
        w = jnp.exp(m_all[br] - m_top)
        num = num + w * pv_ref[br]
        den = den + w * pltpu.roll(den_ref[br], HEAD_DIM, axis=1)
    o_ref[...] = num / den


def _attention(q, k, v, bias):
    b, s, _ = q.shape
    blk = pl.BlockSpec((None, s, LANES), lambda i, p: (i, 0, p))
    nbr = len(DILATED_BRANCHES)
    return pl.pallas_call(
        functools.partial(_attn_kernel, seq=s),
        grid=(b, D_ATTN // LANES),
        in_specs=[blk, blk, blk,
                  pl.BlockSpec((nbr, 2, BLK, 2 * BLK), lambda i, p: (0, p, 0, 0))],
        out_specs=blk,
        out_shape=jax.ShapeDtypeStruct((b, s, D_ATTN), F32),
        scratch_shapes=[pltpu.VMEM((nbr, s, LANES), F32)] * 3,
        compiler_params=_params(("arbitrary", "arbitrary")),
        name="attention",
    )(q, k, v, bias)


def _zoh_kernel(lr_ref, li_ref, dt_ref, lrr_ref, lir_ref, br_ref, bi_ref,
                are_ref, aim_ref, bbr_ref, bbi_ref):
    dt = jnp.exp(dt_ref[...])

    def zoh(lr, li):
        mag = jnp.exp(lr * dt)
        a_re, a_im = mag * jnp.cos(li * dt), mag * jnp.sin(li * dt)
        den = lr * lr + li * li
        f_re = ((a_re - 1.0) * lr + a_im * li) / den
        f_im = (a_im * lr - (a_re - 1.0) * li) / den
        return a_re, a_im, f_re, f_im

    a_re, a_im, _, _ = zoh(lr_ref[...], li_ref[...])
    are_ref[...] = a_re
    aim_ref[...] = a_im
    _, _, f_re, f_im = zoh(lrr_ref[...], lir_ref[...])
    br, bi = br_ref[...], bi_ref[...]
    bbr_ref[...] = f_re * br - f_im * bi
    bbi_ref[...] = f_re * bi + f_im * br


def _ssm_zoh(lam_re, lam_im, log_dt, b_re, b_im):
    g, n, c = b_re.shape
    rep = lambda a: jnp.repeat(a.astype(F32), c, axis=1)
    a_re, a_im, bb_re, bb_im = pl.pallas_call(
        _zoh_kernel,
        out_shape=[jax.ShapeDtypeStruct((g, n), F32)] * 2 + [jax.ShapeDtypeStruct((g, n * c), F32)] * 2,
        name="ssm_zoh",
    )(lam_re.astype(F32), lam_im.astype(F32), log_dt.astype(F32)[:, None], rep(lam_re), rep(lam_im),
      b_re.astype(F32).reshape(g, n * c), b_im.astype(F32).reshape(g, n * c))
    return a_re, a_im, bb_re.reshape(g, n, c), bb_im.reshape(g, n, c)


def _ssm_kernel(u_ref, wb_ref, wc_ref, are_ref, aim_ref, d_ref, gw_ref, gb_ref, gain_ref,
                o_ref, bu_ref, y_ref, sr_ref, si_ref, *, nb, chunk):
    half = N_SLABS // 2
    seqs = 2 * nb
    lane_blocks = 2 * SLAB_STATE // LANES

    @pl.when(pl.program_id(0) == 0)
    def _():
        sr_ref[...] = jnp.zeros_like(sr_ref)
        si_ref[...] = jnp.zeros_like(si_ref)

    for b in range(nb):
        for m in range(N_SLABS):
            gh, mp = divmod(m, half)
            ub = u_ref[b, :, LANES * m:LANES * (m + 1)].astype(BF16)
            bu = jnp.dot(ub, wb_ref[m], preferred_element_type=F32)
            for j in range(lane_blocks):
                bu_ref[lane_blocks * mp + j, pl.ds(gh * nb + b, chunk, stride=seqs), :] = (
                    bu[:, LANES * j:LANES * (j + 1)])

    def load_state(rows, mp, part):
        j0 = lane_blocks * mp + part * (lane_blocks // 2)
        return jnp.concatenate([bu_ref[j0 + j, rows, :] for j in range(lane_blocks // 2)], axis=1)

    def store_state(rows, mp, part, val):
        j0 = lane_blocks * mp + part * (lane_blocks // 2)
        for j in range(lane_blocks // 2):
            bu_ref[j0 + j, rows, :] = val[:, LANES * j:LANES * (j + 1)]

    def step(t, carry):
        base = pl.multiple_of(t * seqs, seqs)
        rows = pl.ds(base, seqs)
        new = []
        for mp in range(half):
            xr, xi = carry[2 * mp], carry[2 * mp + 1]
            ar = are_ref[:, SLAB_STATE * mp:SLAB_STATE * (mp + 1)]
            ai = aim_ref[:, SLAB_STATE * mp:SLAB_STATE * (mp + 1)]
            nr = ar * xr - ai * xi + load_state(rows, mp, 0)
            ni = ar * xi + ai * xr + load_state(rows, mp, 1)
            store_state(rows, mp, 0, nr)
            store_state(rows, mp, 1, ni)
            new += [nr, ni]
        return tuple(new)

    init = []
    for mp in range(half):
        init += [sr_ref[:, SLAB_STATE * mp:SLAB_STATE * (mp + 1)], si_ref[:, SLAB_STATE * mp:SLAB_STATE * (mp + 1)]]
    final = lax.fori_loop(0, chunk, step, tuple(init))
    for mp in range(half):
        sr_ref[:, SLAB_STATE * mp:SLAB_STATE * (mp + 1)] = final[2 * mp]
        si_ref[:, SLAB_STATE * mp:SLAB_STATE * (mp + 1)] = final[2 * mp + 1]

    for b in range(nb):
        for m in range(N_SLABS):
            gh, mp = divmod(m, half)
            xs = jnp.concatenate([bu_ref[lane_blocks * mp + j, pl.ds(gh * nb + b, chunk, stride=seqs), :]
                                  for j in range(lane_blocks)], axis=1)
            y = jnp.dot(xs.astype(BF16), wc_ref[m], preferred_element_type=F32)
            cols = slice(LANES * m, LANES * (m + 1))
            y_ref[b * chunk:(b + 1) * chunk, cols] = y + d_ref[:, cols] * u_ref[b, :, cols]

    y = _erf_gelu(y_ref[...])
    z = jnp.dot(y.astype(BF16), gw_ref[...], preferred_element_type=F32) + gb_ref[...]
    y = y * jax.nn.sigmoid(z)
    ms = jnp.mean(y * y, axis=-1, keepdims=True)
    yn = y * lax.rsqrt(ms + EPS) * gain_ref[...]
    for b in range(nb):
        o_ref[b] = yn[b * chunk:(b + 1) * chunk].astype(o_ref.dtype)


def _s5_mixer(u, lam_re, lam_im, log_dt, b_re, b_im, c_re, c_im, d_skip, glu_w, glu_b, out_gain):
    nb, s, _ = u.shape
    a_re, a_im, bb_re, bb_im = _ssm_zoh(lam_re, lam_im, log_dt, b_re, b_im)
    eye = jnp.eye(SSM_SLAB_GROUPS, dtype=F32)

    def in_slab(bb):
        w = jnp.einsum('mgnc,gh->mgchn', bb.reshape(N_SLABS, SSM_SLAB_GROUPS, SSM_STATE, SSM_GROUP), eye)
        return w.reshape(N_SLABS, LANES, SLAB_STATE)

    def out_slab(cc):
        w = jnp.einsum('mgcn,gh->mgnhc', cc.reshape(N_SLABS, SSM_SLAB_GROUPS, SSM_GROUP, SSM_STATE), eye)
        return w.reshape(N_SLABS, SLAB_STATE, LANES)

    wb = jnp.concatenate([in_slab(bb_re), in_slab(bb_im)], axis=2).astype(BF16)
    wc = jnp.concatenate([out_slab(c_re.astype(F32)), -out_slab(c_im.astype(F32))], axis=1).astype(BF16)
    half_states = (N_SSM_GROUPS // 2) * SSM_STATE

    def seq_rows(a):
        return jnp.repeat(a.reshape(2, half_states), nb, axis=0)

    fixed2 = lambda c: (0, 0)
    fixed3 = lambda c: (0, 0, 0)
    chunk = SSM_CHUNK
    return pl.pallas_call(
        functools.partial(_ssm_kernel, nb=nb, chunk=chunk),
        grid=(s // chunk,),
        in_specs=[pl.BlockSpec((nb, chunk, D_SSM), lambda c: (0, c, 0)),
                  pl.BlockSpec((N_SLABS, LANES, 2 * SLAB_STATE), fixed3),
                  pl.BlockSpec((N_SLABS, 2 * SLAB_STATE, LANES), fixed3),
                  pl.BlockSpec((2 * nb, half_states), fixed2),
                  pl.BlockSpec((2 * nb, half_states), fixed2),
                  pl.BlockSpec((1, D_SSM), fixed2),
                  pl.BlockSpec((D_SSM, D_SSM), fixed2),
                  pl.BlockSpec((1, D_SSM), fixed2),
                  pl.BlockSpec((1, D_SSM), fixed2)],
        out_specs=pl.BlockSpec((nb, chunk, D_SSM), lambda c: (0, c, 0)),
        out_shape=jax.ShapeDtypeStruct((nb, s, D_SSM), BF16),
        scratch_shapes=[pltpu.VMEM((2 * half_states // LANES, 2 * nb * chunk, LANES), F32),
                        pltpu.VMEM((nb * chunk, D_SSM), F32),
                        pltpu.VMEM((2 * nb, half_states), F32),
                        pltpu.VMEM((2 * nb, half_states), F32)],
        compiler_params=_params(("arbitrary",)),
        name="ssm",
    )(u, wb, wc, seq_rows(a_re), seq_rows(a_im), d_skip.astype(F32).reshape(1, D_SSM),
      glu_w.astype(BF16), glu_b.astype(F32)[None, :], out_gain.astype(F32)[None, :])


def _outproj_kernel(a_ref, s_ref, x_ref, ag_ref, wa_ref, ws_ref, fg_ref, x1_ref, hn_ref):
    a = a_ref[...]
    ms = jnp.mean(a * a, axis=-1, keepdims=True)
    an = (a * lax.rsqrt(ms + EPS) * ag_ref[...]).astype(BF16)
    mixed = (jnp.dot(an, wa_ref[...], preferred_element_type=F32)
             + jnp.dot(s_ref[...], ws_ref[...], preferred_element_type=F32))
    x1 = x_ref[...] + mixed
    x1_ref[...] = x1
    ms1 = jnp.mean(x1 * x1, axis=-1, keepdims=True)
    hn_ref[...] = (x1 * lax.rsqrt(ms1 + EPS) * fg_ref[...]).astype(BF16)


def _out_proj(attn, ssm_n, x2, attn_g, w_out, ffn_g):
    t = x2.shape[0]
    row = lambda i: (i, 0)
    fixed = lambda i: (0, 0)
    w = w_out.astype(BF16)
    return pl.pallas_call(
        _outproj_kernel,
        grid=(t // ROW_TILE,),
        in_specs=[pl.BlockSpec((ROW_TILE, D_ATTN), row),
                  pl.BlockSpec((ROW_TILE, D_SSM), row),
                  pl.BlockSpec((ROW_TILE, D_MODEL), row),
                  pl.BlockSpec((1, D_ATTN), fixed),
                  pl.BlockSpec((D_ATTN, D_MODEL), fixed),
                  pl.BlockSpec((D_SSM, D_MODEL), fixed),
                  pl.BlockSpec((1, D_MODEL), fixed)],
        out_specs=[pl.BlockSpec((ROW_TILE, D_MODEL), row)] * 2,
        out_shape=[jax.ShapeDtypeStruct((t, D_MODEL), F32), jax.ShapeDtypeStruct((t, D_MODEL), BF16)],
        compiler_params=_params(("arbitrary",)),
        name="out_proj",
    )(attn, ssm_n, x2, attn_g.astype(F32)[None, :], w[:D_ATTN], w[D_ATTN:], ffn_g.astype(F32)[None, :])


def _top16(s, exact):
    iota = lax.broadcasted_iota(jnp.int32, s.shape, 0)
    rank = jnp.full(s.shape, NO_RANK, F32)
    vals = []
    for it in range(PEER_TOPK):
        m = jnp.max(s, axis=0, keepdims=True)
        if exact:
            idx = jnp.min(jnp.where(s == m, iota, s.shape[0]), axis=0, keepdims=True)
            hit = iota == idx
        else:
            hit = s == m
        rank = jnp.where(hit, float(it), rank)
        s = jnp.where(hit, -jnp.inf, s)
        vals.append(m)
    return vals, rank


def _stack_rows(rows, n):
    iota = lax.broadcasted_iota(jnp.int32, (n, rows[0].shape[1]), 0)
    out = jnp.zeros((n, rows[0].shape[1]), F32)
    for i, r in enumerate(rows):
        out = jnp.where(iota == i, r, out)
    return out


def _count(mask):
    return jnp.sum(jnp.where(mask, 1.0, 0.0), axis=0, keepdims=True)


def _route_head(s1, s2, exact):
    tb = s1.shape[1]
    v1, rank1 = _top16(s1, exact)
    v2, rank2 = _top16(s2, exact)
    v2_all = _stack_rows(v2, PEER_TOPK)
    v1_hi = _stack_rows(v1[SUBLANES:], SUBLANES)
    sub = lax.broadcasted_iota(jnp.int32, (SUBLANES, tb), 0)
    pieces = [v1[0] + v2_all]
    flats = [lax.broadcasted_iota(jnp.int32, (PEER_TOPK, tb), 0)]
    for a in range(1, SUBLANES):
        limit = PEER_TOPK // (a + 1)
        pieces.append(jnp.where(sub < limit, v1[a] + v2_all[:SUBLANES], -jnp.inf))
        flats.append(a * PEER_TOPK + sub)
    pieces.append(v1_hi + v2[0])
    flats.append((sub + SUBLANES) * PEER_TOPK)
    cand0 = jnp.concatenate(pieces, axis=0)
    flat = jnp.concatenate(flats, axis=0)
    cand = cand0
    sel = jnp.zeros(cand.shape, F32)
    for _ in range(PEER_TOPK):
        m = jnp.max(cand, axis=0, keepdims=True)
        if exact:
            idx = jnp.min(jnp.where(cand == m, flat, PEER_TOPK * PEER_TOPK), axis=0, keepdims=True)
            hit = flat == idx
        else:
            hit = cand == m
        sel = jnp.where(hit, 1.0, sel)
        cand = jnp.where(hit, -jnp.inf, cand)
    top = v1[0] + v2[0]
    z = jnp.sum(sel * jnp.exp(jnp.where(sel > 0, cand0, top) - top), axis=0, keepdims=True)
    cnt = [jnp.sum(sel[:PEER_TOPK], axis=0, keepdims=True)]
    for a in range(1, SUBLANES):
        lo = PEER_TOPK + SUBLANES * (a - 1)
        cnt.append(jnp.sum(sel[lo:lo + SUBLANES], axis=0, keepdims=True))
    lo = PEER_TOPK + SUBLANES * (SUBLANES - 1)
    for i in range(SUBLANES):
        cnt.append(sel[lo + i:lo + i + 1])
    c1 = jnp.zeros(rank1.shape, F32)
    for a in range(PEER_TOPK):
        c1 = jnp.where(rank1 == float(a), cnt[a], c1)
    e1 = jnp.exp(s1 - v1[0])
    e2n = jnp.exp(s2 - v2[0]) / z
    k = float(PEER_TOPK)
    taken = jnp.maximum(jnp.maximum(_count(rank1 < k), _count(rank2 < k)), jnp.sum(sel, axis=0, keepdims=True))
    return c1, e1, rank2, e2n, taken


def _route_one(h, qt_ref, k1_ref, k2_ref, c1_ref, e1_ref, r2_ref, e2_ref, exact):
    half = PEER_QDIM // 2
    q1 = qt_ref[PEER_QDIM * h:PEER_QDIM * h + half, :].astype(BF16)
    q2 = qt_ref[PEER_QDIM * h + half:PEER_QDIM * (h + 1), :].astype(BF16)
    s1 = jnp.dot(k1_ref[h], q1, preferred_element_type=F32)
    s2 = jnp.dot(k2_ref[h], q2, preferred_element_type=F32)
    c1, e1, r2, e2n, taken = _route_head(s1, s2, exact)
    c1_ref[h] = c1
    e1_ref[h] = e1
    r2_ref[h] = r2.astype(r2_ref.dtype)
    e2_ref[h] = e2n.astype(e2_ref.dtype)
    return taken


def _route_kernel(hn_ref, wq_ref, k1_ref, k2_ref, c1_ref, e1_ref, r2_ref, e2_ref, qt_ref):
    qt_ref[...] = lax.dot_general(wq_ref[...], hn_ref[...], (((1,), (1,)), ((), ())), preferred_element_type=F32)
    refs = (qt_ref, k1_ref, k2_ref, c1_ref, e1_ref, r2_ref, e2_ref)
    taken = [_route_one(h, *refs, exact=False) for h in range(PEER_HEADS)]
    for h in range(PEER_HEADS):
        @pl.when(jnp.max(taken[h]) > float(PEER_TOPK))
        def _():
            _route_one(h, *refs, exact=True)


def _peer_route(hn, w_q, keys1, keys2):
    t = hn.shape[0]
    blk = pl.BlockSpec((PEER_HEADS, PEER_KEYS, ROUTE_TILE), lambda i: (0, 0, i))
    fixed3 = lambda i: (0, 0, 0)
    shape = (PEER_HEADS, PEER_KEYS, t)
    return pl.pallas_call(
        _route_kernel,
        grid=(t // ROUTE_TILE,),
        in_specs=[pl.BlockSpec((ROUTE_TILE, D_MODEL), lambda i: (i, 0)),
                  pl.BlockSpec((PEER_HEADS * PEER_QDIM, D_MODEL), lambda i: (0, 0)),
                  pl.BlockSpec((PEER_HEADS, PEER_KEYS, PEER_QDIM // 2), fixed3),
                  pl.BlockSpec((PEER_HEADS, PEER_KEYS, PEER_QDIM // 2), fixed3)],
        out_specs=[blk] * 4,
        out_shape=[jax.ShapeDtypeStruct(shape, F32), jax.ShapeDtypeStruct(shape, F32),
                   jax.ShapeDtypeStruct(shape, BF16), jax.ShapeDtypeStruct(shape, BF16)],
        scratch_shapes=[pltpu.VMEM((PEER_HEADS * PEER_QDIM, ROUTE_TILE), F32)],
        compiler_params=_params(("arbitrary",)),
        name="peer_route",
    )(hn, w_q.T.astype(BF16), keys1.astype(BF16), keys2.astype(BF16))


def _peer_gated(c1_ref, e1_ref, r2_ref, e2_ref, at_ref, chunk, valid):
    packed = 2 * SUBLANES
    tiles = PEER_SUB // PEER_KEYS
    gs = []
    for tl in range(tiles):
        tile = chunk * tiles + tl
        gate = jnp.zeros((PEER_KEYS // packed, packed, PEER_TOKENS), BF16)
        for h in range(PEER_HEADS):
            c1 = jnp.broadcast_to(c1_ref[h, pl.ds(tile, 1), :], (packed, PEER_TOKENS)).astype(BF16)
            e1 = jnp.broadcast_to(e1_ref[h, pl.ds(tile, 1), :], (packed, PEER_TOKENS)).astype(BF16)
            r2 = r2_ref[h].reshape(gate.shape)
            e2 = e2_ref[h].reshape(gate.shape)
            gate = gate + jnp.where(r2 < c1[None], e1[None] * e2, jnp.zeros_like(e2))
        a = at_ref[PEER_KEYS * tl:PEER_KEYS * (tl + 1), :].astype(BF16)
        gs.append(gate.reshape(PEER_KEYS, PEER_TOKENS) * _erf_gelu(a))
    g = jnp.concatenate(gs, axis=0)
    return jnp.where(valid, g, jnp.zeros_like(g))


def _peer_kernel(hn_ref, u_ref, *refs, n_chunks):
    n = PEER_EXPERTS_STEP // PEER_SUB
    vt_refs, route = refs[:n], refs[n:n + 4]
    x1_ref, o_ref, acc_ref, at_ref = refs[n + 4:]
    e = pl.program_id(1)
    last = pl.num_programs(1) - 1
    nt = (((1,), (1,)), ((), ()))

    @pl.when(e == 0)
    def _():
        acc_ref[...] = jnp.zeros_like(acc_ref)
        at_ref[n - 1] = jnp.zeros((PEER_SUB, PEER_TOKENS), F32)

    hn = hn_ref[...]
    total = None
    for c in range(n):
        at_new = lax.dot_general(u_ref[PEER_SUB * c:PEER_SUB * (c + 1), :], hn, nt, preferred_element_type=F32)
        prev = n * e + c - 1
        valid = (e > 0) if c == 0 else (e < last)
        g = _peer_gated(*route, at_ref.at[(c - 1) % n], jnp.clip(prev, 0, n_chunks - 1), valid)
        part = jnp.dot(vt_refs[c][...], g, preferred_element_type=F32)
        total = part if total is None else total + part
        at_ref[c] = at_new
    acc_ref[...] += total

    @pl.when(e == last)
    def _():
        o_ref[...] = x1_ref[...] + acc_ref[...].T


def _peer_mix(hn, x1, u_tab, v_tab, c1, e1, r2, e2n):
    t = hn.shape[0]
    n = PEER_EXPERTS_STEP // PEER_SUB
    n_steps = u_tab.shape[0] // PEER_EXPERTS_STEP
    n_chunks = n * n_steps
    route = pl.BlockSpec((PEER_HEADS, PEER_KEYS, PEER_TOKENS), lambda i, e: (0, 0, i))
    tok = lambda i, e: (i, 0)
    once = pl.Buffered(1)
    vt = v_tab.T.astype(BF16)
    vt_specs = [pl.BlockSpec((D_MODEL, PEER_SUB),
                             functools.partial(lambda i, e, c: (0, jnp.clip(n * e + c - 1, 0, n_chunks - 1)), c=c))
                for c in range(n)]
    return pl.pallas_call(
        functools.partial(_peer_kernel, n_chunks=n_chunks),
        grid=(t // PEER_TOKENS, n_steps + 1),
        in_specs=[pl.BlockSpec((PEER_TOKENS, D_MODEL), tok, pipeline_mode=once),
                  pl.BlockSpec((PEER_EXPERTS_STEP, D_MODEL), lambda i, e: (jnp.minimum(e, n_steps - 1), 0)),
                  *vt_specs,
                  route, route, route, route,
                  pl.BlockSpec((PEER_TOKENS, D_MODEL), tok, pipeline_mode=once)],
        out_specs=pl.BlockSpec((PEER_TOKENS, D_MODEL), tok),
        out_shape=jax.ShapeDtypeStruct((t, D_MODEL), F32),
        scratch_shapes=[pltpu.VMEM((D_MODEL, PEER_TOKENS), F32),
                        pltpu.VMEM((n, PEER_SUB, PEER_TOKENS), F32)],
        compiler_params=_params(("arbitrary", "arbitrary")),
        name="peer_mix",
    )(hn, u_tab.astype(BF16), *([vt] * n), c1, e1, r2, e2n, x1)


def kernel(x, norm_mix_g, w_in, q_norm_g, k_norm_g, rel_bias, ssm_lambda_re, ssm_lambda_im, ssm_log_dt,
           ssm_b_re, ssm_b_im, ssm_c_re, ssm_c_im, ssm_d, ssm_glu_w, ssm_glu_b, attn_out_g, ssm_out_g,
           w_out, norm_ffn_g, peer_w_q, peer_keys1, peer_keys2, peer_u, peer_v):
    b, s, d = x.shape
    x2 = x.reshape(b * s, d)
    q, k, v, u = _in_proj(x2, norm_mix_g, w_in, q_norm_g, k_norm_g)
    bias = _bias_tables(rel_bias)
    attn = _attention(q.reshape(b, s, D_ATTN), k.reshape(b, s, D_ATTN), v.reshape(b, s, D_ATTN), bias)
    ssm_n = _s5_mixer(u.reshape(b, s, D_SSM), ssm_lambda_re, ssm_lambda_im, ssm_log_dt, ssm_b_re, ssm_b_im,
                      ssm_c_re, ssm_c_im, ssm_d, ssm_glu_w, ssm_glu_b, ssm_out_g)
    x1, hn = _out_proj(attn.reshape(b * s, D_ATTN), ssm_n.reshape(b * s, D_SSM), x2, attn_out_g, w_out, norm_ffn_g)
    c1, e1, r2, e2n = _peer_route(hn, peer_w_q, peer_keys1, peer_keys2)
    out = _peer_mix(hn, x1, peer_u, peer_v, c1, e1, r2, e2n)
    return out.reshape(b, s, d).astype(x.dtype)
```

```python
import functools
import math

import jax
import jax.numpy as jnp
import numpy as np
from jax import lax
from jax.experimental import pallas as pl
from jax.experimental.pallas import tpu as pltpu

F32 = jnp.float32
BF16 = jnp.bfloat16

D_MODEL = 2048
HEAD_DIM = 64
N_ATTN_HEADS = 16
D_ATTN = N_ATTN_HEADS * HEAD_DIM
SSM_GROUP = 16
N_SSM_GROUPS = 64
D_SSM = N_SSM_GROUPS * SSM_GROUP
SSM_STATE = 64
D_IN_PROJ = 3 * D_ATTN + D_SSM
DILATED_BRANCHES = ((128, 1), (512, 4), (2048, 16))
BLK = 128
N_BUCKETS = 32
MAX_DISTANCE = 2048
PEER_HEADS = 8
PEER_KEYS = 128
PEER_QDIM = 256
PEER_TOPK = 16
EPS = 1e-6
NEG = -1e30

LANES = 128
SUBLANES = 8
VMEM_LIMIT = 56 * 1024 * 1024

ROW_TILE = 256
SSM_CHUNK = 128
SSM_SLAB_GROUPS = LANES // SSM_GROUP
N_SLABS = N_SSM_GROUPS // SSM_SLAB_GROUPS
SLAB_STATE = SSM_SLAB_GROUPS * SSM_STATE
ATTN_LOOKAHEAD = 2
ROUTE_TILE = 256
PEER_TOKENS = 512
PEER_EXPERTS_STEP = 512
PEER_SUB = 256
NO_RANK = 99.0


def _params(sem, vmem=VMEM_LIMIT):
    return pltpu.CompilerParams(dimension_semantics=sem, vmem_limit_bytes=vmem)


def _erf_gelu(x):
    return 0.5 * x * (1.0 + lax.erf(x * math.sqrt(0.5)))


def _t5_bucket(dist):
    max_exact = N_BUCKETS // 2
    n = np.maximum(dist, 0)
    nf = np.maximum(n, 1).astype(np.float32)
    large = max_exact + (np.log(nf / np.float32(max_exact)) / np.float32(math.log(MAX_DISTANCE / max_exact))
                         * np.float32(N_BUCKETS - max_exact)).astype(np.int32)
    large = np.minimum(large, N_BUCKETS - 1)
    return np.where(n < max_exact, n, large)


def _bias_kernel(bkt_ref, rb_ref, out_ref):
    bkt = bkt_ref[0]
    for h in range(N_ATTN_HEADS):
        acc = jnp.zeros((BLK, 2 * BLK), F32)
        for b in range(N_BUCKETS):
            acc = jnp.where(bkt == b, rb_ref[b * N_ATTN_HEADS + h], acc)
        out_ref[0, h] = acc


def _bias_tables(rel_bias):
    qi = np.arange(BLK)[:, None]
    kj = np.arange(2 * BLK)[None, :]
    rel = qi - kj + BLK
    buckets = jnp.asarray(np.stack([_t5_bucket(rel * dil) for _, dil in DILATED_BRANCHES]).astype(np.int32))
    nbr = len(DILATED_BRANCHES)
    return pl.pallas_call(
        _bias_kernel,
        grid=(nbr,),
        in_specs=[pl.BlockSpec((1, BLK, 2 * BLK), lambda i: (i, 0, 0)),
                  pl.BlockSpec(memory_space=pltpu.SMEM)],
        out_specs=pl.BlockSpec((1, N_ATTN_HEADS, BLK, 2 * BLK), lambda i: (i, 0, 0, 0)),
        out_shape=jax.ShapeDtypeStruct((nbr, N_ATTN_HEADS, BLK, 2 * BLK), F32),
        compiler_params=_params(("arbitrary",)),
        name="bias_table",
    )(buckets, rel_bias.astype(F32).reshape(N_BUCKETS * N_ATTN_HEADS))


def _head_rmsnorm(z, gain, ones, scale):
    outs = []
    for c in range(z.shape[1] // LANES):
        zc = z[:, LANES * c:LANES * (c + 1)]
        sq = zc * zc
        hi = sq.astype(BF16)
        lo = (sq - hi.astype(F32)).astype(BF16)
        msq = (jnp.dot(hi, ones, preferred_element_type=F32)
               + jnp.dot(lo, ones, preferred_element_type=F32))
        y = zc * lax.rsqrt(msq + EPS)
        outs.append(y * gain[:, LANES * c:LANES * (c + 1)] * scale)
    return jnp.concatenate(outs, axis=1)


def _inproj_kernel(x_ref, g_ref, w_ref, qg_ref, kg_ref, ones_ref, q_ref, k_ref, v_ref, u_ref):
    x = x_ref[...]
    ms = jnp.mean(x * x, axis=-1, keepdims=True)
    h = (x * lax.rsqrt(ms + EPS) * g_ref[...]).astype(BF16)
    proj = jnp.dot(h, w_ref[...], preferred_element_type=F32)
    ones = ones_ref[...]
    q_ref[...] = _head_rmsnorm(proj[:, :D_ATTN], qg_ref[...], ones, 1.0 / math.sqrt(HEAD_DIM))
    k_ref[...] = _head_rmsnorm(proj[:, D_ATTN:2 * D_ATTN], kg_ref[...], ones, 1.0)
    v_ref[...] = proj[:, 2 * D_ATTN:3 * D_ATTN]
    u_ref[...] = proj[:, 3 * D_ATTN:]


def _in_proj(x2, norm_g, w_in, q_g, k_g):
    t = x2.shape[0]
    head_of_lane = jnp.arange(LANES) // HEAD_DIM
    ones = jnp.where(head_of_lane[:, None] == head_of_lane[None, :], 1.0 / HEAD_DIM, 0.0).astype(BF16)
    qg = jnp.tile(q_g.astype(F32), N_ATTN_HEADS)[None, :]
    kg = jnp.tile(k_g.astype(F32), N_ATTN_HEADS)[None, :]
    row = lambda i: (i, 0)
    fixed = lambda i: (0, 0)
    outs = pl.pallas_call(
        _inproj_kernel,
        grid=(t // ROW_TILE,),
        in_specs=[pl.BlockSpec((ROW_TILE, D_MODEL), row),
                  pl.BlockSpec((1, D_MODEL), fixed),
                  pl.BlockSpec((D_MODEL, D_IN_PROJ), fixed),
                  pl.BlockSpec((1, D_ATTN), fixed),
                  pl.BlockSpec((1, D_ATTN), fixed),
                  pl.BlockSpec((LANES, LANES), fixed)],
        out_specs=[pl.BlockSpec((ROW_TILE, D_ATTN), row)] * 3 + [pl.BlockSpec((ROW_TILE, D_SSM), row)],
        out_shape=[jax.ShapeDtypeStruct((t, D_ATTN), F32)] * 3 + [jax.ShapeDtypeStruct((t, D_SSM), F32)],
        compiler_params=_params(("arbitrary",)),
        name="in_proj",
    )(x2, norm_g.astype(F32)[None, :], w_in.astype(BF16), qg, kg, ones)
    return outs


def _rows(start, size, stride):
    return pl.ds(start, size, stride=stride) if stride > 1 else pl.ds(start, size)


def _attn_scores(q_ref, k_ref, v_ref, bias_ref, blk, tri_cur, tri_prev, head0):
    br, dil, n, r = blk
    qrows = _rows(r + dil * BLK * n, BLK, dil)
    qb = q_ref[qrows, :].astype(BF16)
    if n == 0:
        krows = qrows
        valid = tri_cur
    else:
        krows = _rows(r + dil * BLK * (n - 1), 2 * BLK, dil)
        valid = jnp.concatenate([tri_prev, tri_cur], axis=1)
    kb = k_ref[krows, :].astype(BF16)
    vb = v_ref[krows, :].astype(BF16)
    scores = []
    for h in range(2):
        mine = head0 if h == 0 else jnp.logical_not(head0)
        qh = jnp.where(mine, qb, jnp.zeros_like(qb))
        s = lax.dot_general(qh, kb, (((1,), (1,)), ((), ())), preferred_element_type=F32)
        bias = bias_ref[br, h, :, BLK:] if n == 0 else bias_ref[br, h]
        scores.append(jnp.where(valid, s + bias, NEG))
    return scores, vb, qrows


def _attn_values(scores, vb, qrows, br, pv_ref, den_ref, m_ref, head0):
    ones = jnp.ones_like(vb)
    pv, mx = [], []
    for h in range(2):
        mine = head0 if h == 0 else jnp.logical_not(head0)
        m = jnp.max(scores[h], axis=-1, keepdims=True)
        p = jnp.exp(scores[h] - m).astype(BF16)
        pv.append(jnp.dot(p, jnp.where(mine, vb, ones), preferred_element_type=F32))
        mx.append(m)
    pv_ref[br, qrows, :] = jnp.where(head0, pv[0], pv[1])
    den_ref[br, qrows, :] = jnp.where(head0, pv[1], pv[0])
    m_ref[br, qrows, :] = jnp.where(head0, mx[0], mx[1])


def _attn_kernel(q_ref, k_ref, v_ref, bias_ref, o_ref, pv_ref, den_ref, m_ref, *, seq):
    qi = lax.broadcasted_iota(jnp.int32, (BLK, BLK), 0)
    kj = lax.broadcasted_iota(jnp.int32, (BLK, BLK), 1)
    tri_cur = qi >= kj
    tri_prev = kj >= qi
    head0 = lax.broadcasted_iota(jnp.int32, (1, LANES), 1) < HEAD_DIM
    nbr = len(DILATED_BRANCHES)
    blocks = []
    for br, (window, dil) in enumerate(DILATED_BRANCHES):
        assert window // dil == BLK
        blocks += [(br, dil, n, r) for r in range(dil) for n in range(seq // dil // BLK)]
    ahead = [_attn_scores(q_ref, k_ref, v_ref, bias_ref, b, tri_cur, tri_prev, head0)
             for b in blocks[:ATTN_LOOKAHEAD]]
    for i, blk in enumerate(blocks):
        if i + ATTN_LOOKAHEAD < len(blocks):
            ahead.append(_attn_scores(q_ref, k_ref, v_ref, bias_ref, blocks[i + ATTN_LOOKAHEAD],
                                      tri_cur, tri_prev, head0))
        _attn_values(*ahead.pop(0), blk[0], pv_ref, den_ref, m_ref, head0)
    m_all = [m_ref[br] for br in range(nbr)]
    m_top = functools.reduce(jnp.maximum, m_all)
    num = jnp.zeros((seq, LANES), F32)
    den = jnp.zeros((seq, LANES), F32)
    for br in range(nbr):
        w = jnp.exp(m_all[br] - m_top)
        num = num + w * pv_ref[br]
        den = den + w * pltpu.roll(den_ref[br], HEAD_DIM, axis=1)
    o_ref[...] = num / den


def _attention(q, k, v, bias):
    b, s, _ = q.shape
    blk = pl.BlockSpec((None, s, LANES), lambda i, p: (i, 0, p))
    nbr = len(DILATED_BRANCHES)
    return pl.pallas_call(
        functools.partial(_attn_kernel, seq=s),
        grid=(b, D_ATTN // LANES),
        in_specs=[blk, blk, blk,
                  pl.BlockSpec((nbr, 2, BLK, 2 * BLK), lambda i, p: (0, p, 0, 0))],
        out_specs=blk,
        out_shape=jax.ShapeDtypeStruct((b, s, D_ATTN), F32),
        scratch_shapes=[pltpu.VMEM((nbr, s, LANES), F32)] * 3,
        compiler_params=_params(("arbitrary", "arbitrary")),
        name="attention",
    )(q, k, v, bias)


def _zoh_kernel(lr_ref, li_ref, dt_ref, lrr_ref, lir_ref, br_ref, bi_ref,
                are_ref, aim_ref, bbr_ref, bbi_ref):
    dt = jnp.exp(dt_ref[...])

    def zoh(lr, li):
        mag = jnp.exp(lr * dt)
        a_re, a_im = mag * jnp.cos(li * dt), mag * jnp.sin(li * dt)
        den = lr * lr + li * li
        f_re = ((a_re - 1.0) * lr + a_im * li) / den
        f_im = (a_im * lr - (a_re - 1.0) * li) / den
        return a_re, a_im, f_re, f_im

    a_re, a_im, _, _ = zoh(lr_ref[...], li_ref[...])
    are_ref[...] = a_re
    aim_ref[...] = a_im
    _, _, f_re, f_im = zoh(lrr_ref[...], lir_ref[...])
    br, bi = br_ref[...], bi_ref[...]
    bbr_ref[...] = f_re * br - f_im * bi
    bbi_ref[...] = f_re * bi + f_im * br


def _ssm_zoh(lam_re, lam_im, log_dt, b_re, b_im):
    g, n, c = b_re.shape
    rep = lambda a: jnp.repeat(a.astype(F32), c, axis=1)
    a_re, a_im, bb_re, bb_im = pl.pallas_call(
        _zoh_kernel,
        out_shape=[jax.ShapeDtypeStruct((g, n), F32)] * 2 + [jax.ShapeDtypeStruct((g, n * c), F32)] * 2,
        name="ssm_zoh",
    )(lam_re.astype(F32), lam_im.astype(F32), log_dt.astype(F32)[:, None], rep(lam_re), rep(lam_im),
      b_re.astype(F32).reshape(g, n * c), b_im.astype(F32).reshape(g, n * c))
    return a_re, a_im, bb_re.reshape(g, n, c), bb_im.reshape(g, n, c)


def _ssm_kernel(u_ref, wb_ref, wc_ref, are_ref, aim_ref, d_ref, gw_ref, gb_ref, gain_ref,
                o_ref, lhs_ref, bu_ref, ysel_ref, y_ref, sr_ref, si_ref, *, nb, chunk):
    half = N_SLABS // 2
    seqs = 2 * nb
    lane_blocks = 2 * SLAB_STATE // LANES
    rows_all = seqs * chunk

    @pl.when(pl.program_id(0) == 0)
    def _():
        sr_ref[...] = jnp.zeros_like(sr_ref)
        si_ref[...] = jnp.zeros_like(si_ref)
        lhs_ref[...] = jnp.zeros_like(lhs_ref)

    for b in range(nb):
        for m in range(N_SLABS):
            gh, mp = divmod(m, half)
            lhs_ref[2 * mp + gh, pl.ds(gh * nb + b, chunk, stride=seqs), :] = u_ref[b, :, LANES * m:LANES * (m + 1)]

    for mp in range(half):
        lhs = jnp.concatenate([lhs_ref[2 * mp], lhs_ref[2 * mp + 1]], axis=1).astype(BF16)
        bu = jnp.dot(lhs, wb_ref[mp], preferred_element_type=F32)
        for j in range(lane_blocks):
            bu_ref[lane_blocks * mp + j] = bu[:, LANES * j:LANES * (j + 1)]

    def load_state(rows, mp, part):
        j0 = lane_blocks * mp + part * (lane_blocks // 2)
        return jnp.concatenate([bu_ref[j0 + j, rows, :] for j in range(lane_blocks // 2)], axis=1)

    def store_state(rows, mp, part, val):
        j0 = lane_blocks * mp + part * (lane_blocks // 2)
        for j in range(lane_blocks // 2):
            bu_ref[j0 + j, rows, :] = val[:, LANES * j:LANES * (j + 1)]

    def step(t, carry):
        base = pl.multiple_of(t * seqs, seqs)
        rows = pl.ds(base, seqs)
        new = []
        for mp in range(half):
            xr, xi = carry[2 * mp], carry[2 * mp + 1]
            ar = are_ref[:, SLAB_STATE * mp:SLAB_STATE * (mp + 1)]
            ai = aim_ref[:, SLAB_STATE * mp:SLAB_STATE * (mp + 1)]
            nr = ar * xr - ai * xi + load_state(rows, mp, 0)
            ni = ar * xi + ai * xr + load_state(rows, mp, 1)
            store_state(rows, mp, 0, nr)
            store_state(rows, mp, 1, ni)
            new += [nr, ni]
        return tuple(new)

    init = []
    for mp in range(half):
        init += [sr_ref[:, SLAB_STATE * mp:SLAB_STATE * (mp + 1)], si_ref[:, SLAB_STATE * mp:SLAB_STATE * (mp + 1)]]
    final = lax.fori_loop(0, chunk, step, tuple(init))
    for mp in range(half):
        sr_ref[:, SLAB_STATE * mp:SLAB_STATE * (mp + 1)] = final[2 * mp]
        si_ref[:, SLAB_STATE * mp:SLAB_STATE * (mp + 1)] = final[2 * mp + 1]

    first_half = (lax.broadcasted_iota(jnp.int32, (rows_all, LANES), 0) & nb) == 0
    for mp in range(half):
        xs = jnp.concatenate([bu_ref[lane_blocks * mp + j] for j in range(lane_blocks)], axis=1).astype(BF16)
        yy = jnp.dot(xs, wc_ref[mp], preferred_element_type=F32)
        ysel_ref[mp] = jnp.where(first_half, yy[:, :LANES], yy[:, LANES:])
    for b in range(nb):
        for m in range(N_SLABS):
            gh, mp = divmod(m, half)
            cols = slice(LANES * m, LANES * (m + 1))
            y = ysel_ref[mp, pl.ds(gh * nb + b, chunk, stride=seqs), :]
            y_ref[b * chunk:(b + 1) * chunk, cols] = y + d_ref[:, cols] * u_ref[b, :, cols]

    y = _erf_gelu(y_ref[...])
    z = jnp.dot(y.astype(BF16), gw_ref[...], preferred_element_type=F32) + gb_ref[...]
    y = y * jax.nn.sigmoid(z)
    ms = jnp.mean(y * y, axis=-1, keepdims=True)
    yn = y * lax.rsqrt(ms + EPS) * gain_ref[...]
    for b in range(nb):
        o_ref[b] = yn[b * chunk:(b + 1) * chunk].astype(o_ref.dtype)


def _s5_mixer(u, lam_re, lam_im, log_dt, b_re, b_im, c_re, c_im, d_skip, glu_w, glu_b, out_gain):
    nb, s, _ = u.shape
    a_re, a_im, bb_re, bb_im = _ssm_zoh(lam_re, lam_im, log_dt, b_re, b_im)
    eye = jnp.eye(SSM_SLAB_GROUPS, dtype=F32)

    def in_slab(bb):
        w = jnp.einsum('mgnc,gh->mgchn', bb.reshape(N_SLABS, SSM_SLAB_GROUPS, SSM_STATE, SSM_GROUP), eye)
        return w.reshape(N_SLABS, LANES, SLAB_STATE)

    def out_slab(cc):
        w = jnp.einsum('mgcn,gh->mgnhc', cc.reshape(N_SLABS, SSM_SLAB_GROUPS, SSM_GROUP, SSM_STATE), eye)
        return w.reshape(N_SLABS, SLAB_STATE, LANES)

    wb = jnp.concatenate([in_slab(bb_re), in_slab(bb_im)], axis=2)
    wc = jnp.concatenate([out_slab(c_re.astype(F32)), -out_slab(c_im.astype(F32))], axis=1)
    hs = N_SLABS // 2
    wb = jnp.concatenate([wb[:hs], wb[hs:]], axis=1).astype(BF16)
    wc = jnp.concatenate([wc[:hs], wc[hs:]], axis=2).astype(BF16)
    half_states = (N_SSM_GROUPS // 2) * SSM_STATE

    def seq_rows(a):
        return jnp.repeat(a.reshape(2, half_states), nb, axis=0)

    fixed2 = lambda c: (0, 0)
    fixed3 = lambda c: (0, 0, 0)
    chunk = SSM_CHUNK
    return pl.pallas_call(
        functools.partial(_ssm_kernel, nb=nb, chunk=chunk),
        grid=(s // chunk,),
        in_specs=[pl.BlockSpec((nb, chunk, D_SSM), lambda c: (0, c, 0)),
                  pl.BlockSpec((N_SLABS // 2, 2 * LANES, 2 * SLAB_STATE), fixed3),
                  pl.BlockSpec((N_SLABS // 2, 2 * SLAB_STATE, 2 * LANES), fixed3),
                  pl.BlockSpec((2 * nb, half_states), fixed2),
                  pl.BlockSpec((2 * nb, half_states), fixed2),
                  pl.BlockSpec((1, D_SSM), fixed2),
                  pl.BlockSpec((D_SSM, D_SSM), fixed2),
                  pl.BlockSpec((1, D_SSM), fixed2),
                  pl.BlockSpec((1, D_SSM), fixed2)],
        out_specs=pl.BlockSpec((nb, chunk, D_SSM), lambda c: (0, c, 0)),
        out_shape=jax.ShapeDtypeStruct((nb, s, D_SSM), BF16),
        scratch_shapes=[pltpu.VMEM((N_SLABS, 2 * nb * chunk, LANES), F32),
                        pltpu.VMEM((2 * half_states // LANES, 2 * nb * chunk, LANES), F32),
                        pltpu.VMEM((N_SLABS // 2, 2 * nb * chunk, LANES), F32),
                        pltpu.VMEM((nb * chunk, D_SSM), F32),
                        pltpu.VMEM((2 * nb, half_states), F32),
                        pltpu.VMEM((2 * nb, half_states), F32)],
        compiler_params=_params(("arbitrary",)),
        name="ssm",
    )(u, wb, wc, seq_rows(a_re), seq_rows(a_im), d_skip.astype(F32).reshape(1, D_SSM),
      glu_w.astype(BF16), glu_b.astype(F32)[None, :], out_gain.astype(F32)[None, :])


def _outproj_kernel(a_ref, s_ref, x_ref, ag_ref, wa_ref, ws_ref, fg_ref, x1_ref, hn_ref):
    a = a_ref[...]
    ms = jnp.mean(a * a, axis=-1, keepdims=True)
    an = (a * lax.rsqrt(ms + EPS) * ag_ref[...]).astype(BF16)
    mixed = (jnp.dot(an, wa_ref[...], preferred_element_type=F32)
             + jnp.dot(s_ref[...], ws_ref[...], preferred_element_type=F32))
    x1 = x_ref[...] + mixed
    x1_ref[...] = x1
    ms1 = jnp.mean(x1 * x1, axis=-1, keepdims=True)
    hn_ref[...] = (x1 * lax.rsqrt(ms1 + EPS) * fg_ref[...]).astype(BF16)


def _out_proj(attn, ssm_n, x2, attn_g, w_out, ffn_g):
    t = x2.shape[0]
    row = lambda i: (i, 0)
    fixed = lambda i: (0, 0)
    w = w_out.astype(BF16)
    return pl.pallas_call(
        _outproj_kernel,
        grid=(t // ROW_TILE,),
        in_specs=[pl.BlockSpec((ROW_TILE, D_ATTN), row),
                  pl.BlockSpec((ROW_TILE, D_SSM), row),
                  pl.BlockSpec((ROW_TILE, D_MODEL), row),
                  pl.BlockSpec((1, D_ATTN), fixed),
                  pl.BlockSpec((D_ATTN, D_MODEL), fixed),
                  pl.BlockSpec((D_SSM, D_MODEL), fixed),
                  pl.BlockSpec((1, D_MODEL), fixed)],
        out_specs=[pl.BlockSpec((ROW_TILE, D_MODEL), row)] * 2,
        out_shape=[jax.ShapeDtypeStruct((t, D_MODEL), F32), jax.ShapeDtypeStruct((t, D_MODEL), BF16)],
        compiler_params=_params(("arbitrary",)),
        name="out_proj",
    )(attn, ssm_n, x2, attn_g.astype(F32)[None, :], w[:D_ATTN], w[D_ATTN:], ffn_g.astype(F32)[None, :])


def _top16(s, exact):
    iota = lax.broadcasted_iota(jnp.int32, s.shape, 0)
    rank = jnp.full(s.shape, NO_RANK, F32)
    vals = []
    for it in range(PEER_TOPK):
        m = jnp.max(s, axis=0, keepdims=True)
        if exact:
            idx = jnp.min(jnp.where(s == m, iota, s.shape[0]), axis=0, keepdims=True)
            hit = iota == idx
        else:
            hit = s == m
        rank = jnp.where(hit, float(it), rank)
        s = jnp.where(hit, -jnp.inf, s)
        vals.append(m)
    return vals, rank


def _stack_rows(rows, n):
    iota = lax.broadcasted_iota(jnp.int32, (n, rows[0].shape[1]), 0)
    out = jnp.zeros((n, rows[0].shape[1]), F32)
    for i, r in enumerate(rows):
        out = jnp.where(iota == i, r, out)
    return out


def _count(mask):
    return jnp.sum(jnp.where(mask, 1.0, 0.0), axis=0, keepdims=True)


def _route_head(s1, s2, exact):
    tb = s1.shape[1]
    v1, rank1 = _top16(s1, exact)
    v2, rank2 = _top16(s2, exact)
    v2_all = _stack_rows(v2, PEER_TOPK)
    v1_hi = _stack_rows(v1[SUBLANES:], SUBLANES)
    sub = lax.broadcasted_iota(jnp.int32, (SUBLANES, tb), 0)
    pieces = [v1[0] + v2_all]
    flats = [lax.broadcasted_iota(jnp.int32, (PEER_TOPK, tb), 0)]
    for a in range(1, SUBLANES):
        limit = PEER_TOPK // (a + 1)
        pieces.append(jnp.where(sub < limit, v1[a] + v2_all[:SUBLANES], -jnp.inf))
        flats.append(a * PEER_TOPK + sub)
    pieces.append(v1_hi + v2[0])
    flats.append((sub + SUBLANES) * PEER_TOPK)
    cand0 = jnp.concatenate(pieces, axis=0)
    flat = jnp.concatenate(flats, axis=0)
    cand = cand0
    sel = jnp.zeros(cand.shape, F32)
    for _ in range(PEER_TOPK):
        m = jnp.max(cand, axis=0, keepdims=True)
        if exact:
            idx = jnp.min(jnp.where(cand == m, flat, PEER_TOPK * PEER_TOPK), axis=0, keepdims=True)
            hit = flat == idx
        else:
            hit = cand == m
        sel = jnp.where(hit, 1.0, sel)
        cand = jnp.where(hit, -jnp.inf, cand)
    top = v1[0] + v2[0]
    z = jnp.sum(sel * jnp.exp(jnp.where(sel > 0, cand0, top) - top), axis=0, keepdims=True)
    cnt = [jnp.sum(sel[:PEER_TOPK], axis=0, keepdims=True)]
    for a in range(1, SUBLANES):
        lo = PEER_TOPK + SUBLANES * (a - 1)
        cnt.append(jnp.sum(sel[lo:lo + SUBLANES], axis=0, keepdims=True))
    lo = PEER_TOPK + SUBLANES * (SUBLANES - 1)
    for i in range(SUBLANES):
        cnt.append(sel[lo + i:lo + i + 1])
    c1 = jnp.zeros(rank1.shape, F32)
    for a in range(PEER_TOPK):
        c1 = jnp.where(rank1 == float(a), cnt[a], c1)
    e1 = jnp.exp(s1 - v1[0])
    e2n = jnp.exp(s2 - v2[0]) / z
    k = float(PEER_TOPK)
    taken = jnp.maximum(jnp.maximum(_count(rank1 < k), _count(rank2 < k)), jnp.sum(sel, axis=0, keepdims=True))
    return c1, e1, rank2, e2n, taken


def _route_one(h, qt_ref, k1_ref, k2_ref, c1_ref, e1_ref, r2_ref, e2_ref, exact):
    half = PEER_QDIM // 2
    q1 = qt_ref[PEER_QDIM * h:PEER_QDIM * h + half, :].astype(BF16)
    q2 = qt_ref[PEER_QDIM * h + half:PEER_QDIM * (h + 1), :].astype(BF16)
    s1 = jnp.dot(k1_ref[h], q1, preferred_element_type=F32)
    s2 = jnp.dot(k2_ref[h], q2, preferred_element_type=F32)
    c1, e1, r2, e2n, taken = _route_head(s1, s2, exact)
    c1_ref[h] = c1
    e1_ref[h] = e1
    r2_ref[h] = r2.astype(r2_ref.dtype)
    e2_ref[h] = e2n.astype(e2_ref.dtype)
    return taken


def _route_kernel(hn_ref, wq_ref, k1_ref, k2_ref, c1_ref, e1_ref, r2_ref, e2_ref, qt_ref):
    qt_ref[...] = lax.dot_general(wq_ref[...], hn_ref[...], (((1,), (1,)), ((), ())), preferred_element_type=F32)
    refs = (qt_ref, k1_ref, k2_ref, c1_ref, e1_ref, r2_ref, e2_ref)
    taken = [_route_one(h, *refs, exact=False) for h in range(PEER_HEADS)]
    for h in range(PEER_HEADS):
        @pl.when(jnp.max(taken[h]) > float(PEER_TOPK))
        def _():
            _route_one(h, *refs, exact=True)


def _peer_route(hn, w_q, keys1, keys2):
    t = hn.shape[0]
    blk = pl.BlockSpec((PEER_HEADS, PEER_KEYS, ROUTE_TILE), lambda i: (0, 0, i))
    fixed3 = lambda i: (0, 0, 0)
    shape = (PEER_HEADS, PEER_KEYS, t)
    return pl.pallas_call(
        _route_kernel,
        grid=(t // ROUTE_TILE,),
        in_specs=[pl.BlockSpec((ROUTE_TILE, D_MODEL), lambda i: (i, 0)),
                  pl.BlockSpec((PEER_HEADS * PEER_QDIM, D_MODEL), lambda i: (0, 0)),
                  pl.BlockSpec((PEER_HEADS, PEER_KEYS, PEER_QDIM // 2), fixed3),
                  pl.BlockSpec((PEER_HEADS, PEER_KEYS, PEER_QDIM // 2), fixed3)],
        out_specs=[blk] * 4,
        out_shape=[jax.ShapeDtypeStruct(shape, F32), jax.ShapeDtypeStruct(shape, F32),
                   jax.ShapeDtypeStruct(shape, BF16), jax.ShapeDtypeStruct(shape, BF16)],
        scratch_shapes=[pltpu.VMEM((PEER_HEADS * PEER_QDIM, ROUTE_TILE), F32)],
        compiler_params=_params(("arbitrary",)),
        name="peer_route",
    )(hn, w_q.T.astype(BF16), keys1.astype(BF16), keys2.astype(BF16))


def _peer_gated(c1_ref, e1_ref, r2_ref, e2_ref, at_ref, chunk, valid):
    packed = 2 * SUBLANES
    tiles = PEER_SUB // PEER_KEYS
    gs = []
    for tl in range(tiles):
        tile = chunk * tiles + tl
        gate = jnp.zeros((PEER_KEYS // packed, packed, PEER_TOKENS), BF16)
        for h in range(PEER_HEADS):
            c1 = jnp.broadcast_to(c1_ref[h, pl.ds(tile, 1), :], (packed, PEER_TOKENS)).astype(BF16)
            e1 = jnp.broadcast_to(e1_ref[h, pl.ds(tile, 1), :], (packed, PEER_TOKENS)).astype(BF16)
            r2 = r2_ref[h].reshape(gate.shape)
            e2 = e2_ref[h].reshape(gate.shape)
            gate = gate + jnp.where(r2 < c1[None], e1[None] * e2, jnp.zeros_like(e2))
        a = at_ref[PEER_KEYS * tl:PEER_KEYS * (tl + 1), :].astype(BF16)
        gs.append(gate.reshape(PEER_KEYS, PEER_TOKENS) * _erf_gelu(a))
    g = jnp.concatenate(gs, axis=0)
    return jnp.where(valid, g, jnp.zeros_like(g))


def _peer_kernel(hn_ref, u_ref, *refs, n_chunks):
    n = PEER_EXPERTS_STEP // PEER_SUB
    vt_refs, route = refs[:n], refs[n:n + 4]
    x1_ref, o_ref, acc_ref, at_ref = refs[n + 4:]
    e = pl.program_id(1)
    last = pl.num_programs(1) - 1
    nt = (((1,), (1,)), ((), ()))

    @pl.when(e == 0)
    def _():
        acc_ref[...] = jnp.zeros_like(acc_ref)
        at_ref[n - 1] = jnp.zeros((PEER_SUB, PEER_TOKENS), F32)

    hn = hn_ref[...]
    total = None
    for c in range(n):
        at_ref[c] = lax.dot_general(u_ref[PEER_SUB * c:PEER_SUB * (c + 1), :], hn, nt, preferred_element_type=F32)
        prev = n * e + c - 1
        valid = (e > 0) if c == 0 else (e < last)
        g = _peer_gated(*route, at_ref.at[(c - 1) % n], jnp.clip(prev, 0, n_chunks - 1), valid)
        part = jnp.dot(vt_refs[c][...], g, preferred_element_type=F32)
        total = part if total is None else total + part
    acc_ref[...] += total

    @pl.when(e == last)
    def _():
        o_ref[...] = x1_ref[...] + acc_ref[...].T


def _peer_mix(hn, x1, u_tab, v_tab, c1, e1, r2, e2n):
    t = hn.shape[0]
    n = PEER_EXPERTS_STEP // PEER_SUB
    n_steps = u_tab.shape[0] // PEER_EXPERTS_STEP
    n_chunks = n * n_steps
    route = pl.BlockSpec((PEER_HEADS, PEER_KEYS, PEER_TOKENS), lambda i, e: (0, 0, i))
    tok = lambda i, e: (i, 0)
    vt = v_tab.T.astype(BF16)
    vt_specs = [pl.BlockSpec((D_MODEL, PEER_SUB),
                             functools.partial(lambda i, e, c: (0, jnp.clip(n * e + c - 1, 0, n_chunks - 1)), c=c))
                for c in range(n)]
    return pl.pallas_call(
        functools.partial(_peer_kernel, n_chunks=n_chunks),
        grid=(t // PEER_TOKENS, n_steps + 1),
        in_specs=[pl.BlockSpec((PEER_TOKENS, D_MODEL), tok),
                  pl.BlockSpec((PEER_EXPERTS_STEP, D_MODEL), lambda i, e: (jnp.minimum(e, n_steps - 1), 0)),
                  *vt_specs,
                  route, route, route, route,
                  pl.BlockSpec((PEER_TOKENS, D_MODEL), tok)],
        out_specs=pl.BlockSpec((PEER_TOKENS, D_MODEL), tok),
        out_shape=jax.ShapeDtypeStruct((t, D_MODEL), F32),
        scratch_shapes=[pltpu.VMEM((D_MODEL, PEER_TOKENS), F32),
                        pltpu.VMEM((n, PEER_SUB, PEER_TOKENS), F32)],
        compiler_params=_params(("arbitrary", "arbitrary")),
        name="peer_mix",
    )(hn, u_tab.astype(BF16), *([vt] * n), c1, e1, r2, e2n, x1)


def kernel(x, norm_mix_g, w_in, q_norm_g, k_norm_g, rel_bias, ssm_lambda_re, ssm_lambda_im, ssm_log_dt,
           ssm_b_re, ssm_b_im, ssm_c_re, ssm_c_im, ssm_d, ssm_glu_w, ssm_glu_b, attn_out_g, ssm_out_g,
           w_out, norm_ffn_g, peer_w_q, peer_keys1, peer_keys2, peer_u, peer_v):
    b, s, d = x.shape
    x2 = x.reshape(b * s, d)
    q, k, v, u = _in_proj(x2, norm_mix_g, w_in, q_norm_g, k_norm_g)
    bias = _bias_tables(rel_bias)
    attn = _attention(q.reshape(b, s, D_ATTN), k.reshape(b, s, D_ATTN), v.reshape(b, s, D_ATTN), bias)
    ssm_n = _s5_mixer(u.reshape(b, s, D_SSM), ssm_lambda_re, ssm_lambda_im, ssm_log_dt, ssm_b_re, ssm_b_im,
                      ssm_c_re, ssm_c_im, ssm_d, ssm_glu_w, ssm_glu_b, ssm_out_g)
    x1, hn = _out_proj(attn.reshape(b * s, D_ATTN), ssm_n.reshape(b * s, D_SSM), x2, attn_out_g, w_out, norm_ffn_g)
    c1, e1, r2, e2n = _peer_route(hn, peer_w_q, peer_keys1, peer_keys2)
    out = _peer_mix(hn, x1, peer_u, peer_v, c1, e1, r2, e2n)
    return out.reshape(b, s, d).astype(x.dtype)
```

```python
import functools
import math

import jax
import jax.numpy as jnp
import numpy as np
from jax import lax
from jax.experimental import pallas as pl
from jax.experimental.pallas import tpu as pltpu

F32 = jnp.float32
BF16 = jnp.bfloat16

D_MODEL = 2048
HEAD_DIM = 64
N_ATTN_HEADS = 16
D_ATTN = N_ATTN_HEADS * HEAD_DIM
SSM_GROUP = 16
N_SSM_GROUPS = 64
D_SSM = N_SSM_GROUPS * SSM_GROUP
SSM_STATE = 64
D_IN_PROJ = 3 * D_ATTN + D_SSM
DILATED_BRANCHES = ((128, 1), (512, 4), (2048, 16))
BLK = 128
N_BUCKETS = 32
MAX_DISTANCE = 2048
PEER_HEADS = 8
PEER_KEYS = 128
PEER_QDIM = 256
PEER_TOPK = 16
EPS = 1e-6
NEG = -1e30
LOG2E = math.log2(math.e)

LANES = 128
SUBLANES = 8
VMEM_LIMIT = 56 * 1024 * 1024

ROW_TILE = 256
SSM_CHUNK = 128
SSM_SLAB_GROUPS = LANES // SSM_GROUP
N_SLABS = N_SSM_GROUPS // SSM_SLAB_GROUPS
SLAB_STATE = SSM_SLAB_GROUPS * SSM_STATE
ATTN_LOOKAHEAD = 2
ROUTE_TILE = 256
PEER_TOKENS = 512
PEER_EXPERTS_STEP = 512
PEER_SUB = 256
NO_RANK = 99.0
RANK_BASE = -2.0 ** 100


def _params(sem, vmem=VMEM_LIMIT):
    return pltpu.CompilerParams(dimension_semantics=sem, vmem_limit_bytes=vmem)


def _erf_gelu(x):
    return 0.5 * x * (1.0 + lax.erf(x * math.sqrt(0.5)))


def _t5_bucket(dist):
    max_exact = N_BUCKETS // 2
    n = np.maximum(dist, 0)
    nf = np.maximum(n, 1).astype(np.float32)
    large = max_exact + (np.log(nf / np.float32(max_exact)) / np.float32(math.log(MAX_DISTANCE / max_exact))
                         * np.float32(N_BUCKETS - max_exact)).astype(np.int32)
    large = np.minimum(large, N_BUCKETS - 1)
    return np.where(n < max_exact, n, large)


def _bias_kernel(bkt_ref, rb_ref, out_ref):
    bkt = bkt_ref[0]
    for h in range(N_ATTN_HEADS):
        acc = jnp.zeros((BLK, 2 * BLK), F32)
        for b in range(N_BUCKETS):
            acc = jnp.where(bkt == b, rb_ref[b * N_ATTN_HEADS + h], acc)
        out_ref[0, h] = acc * LOG2E


def _bias_tables(rel_bias):
    qi = np.arange(BLK)[:, None]
    kj = np.arange(2 * BLK)[None, :]
    rel = qi - kj + BLK
    buckets = jnp.asarray(np.stack([_t5_bucket(rel * dil) for _, dil in DILATED_BRANCHES]).astype(np.int32))
    nbr = len(DILATED_BRANCHES)
    return pl.pallas_call(
        _bias_kernel,
        grid=(nbr,),
        in_specs=[pl.BlockSpec((1, BLK, 2 * BLK), lambda i: (i, 0, 0)),
                  pl.BlockSpec(memory_space=pltpu.SMEM)],
        out_specs=pl.BlockSpec((1, N_ATTN_HEADS, BLK, 2 * BLK), lambda i: (i, 0, 0, 0)),
        out_shape=jax.ShapeDtypeStruct((nbr, N_ATTN_HEADS, BLK, 2 * BLK), F32),
        compiler_params=_params(("arbitrary",)),
        name="bias_table",
    )(buckets, rel_bias.astype(F32).reshape(N_BUCKETS * N_ATTN_HEADS))


def _head_rmsnorm(z, gain, ones, scale):
    outs = []
    for c in range(z.shape[1] // LANES):
        zc = z[:, LANES * c:LANES * (c + 1)]
        sq = zc * zc
        hi = sq.astype(BF16)
        lo = (sq - hi.astype(F32)).astype(BF16)
        msq = (jnp.dot(hi, ones, preferred_element_type=F32)
               + jnp.dot(lo, ones, preferred_element_type=F32))
        y = zc * lax.rsqrt(msq + EPS)
        outs.append(y * gain[:, LANES * c:LANES * (c + 1)] * scale)
    return jnp.concatenate(outs, axis=1)


def _inproj_kernel(x_ref, g_ref, w_ref, qg_ref, kg_ref, ones_ref, q_ref, k_ref, v_ref, u_ref):
    x = x_ref[...]
    ms = jnp.mean(x * x, axis=-1, keepdims=True)
    h = (x * lax.rsqrt(ms + EPS) * g_ref[...]).astype(BF16)
    proj = jnp.dot(h, w_ref[...], preferred_element_type=F32)
    ones = ones_ref[...]
    q_ref[...] = _head_rmsnorm(proj[:, :D_ATTN], qg_ref[...], ones, LOG2E / math.sqrt(HEAD_DIM))
    k_ref[...] = _head_rmsnorm(proj[:, D_ATTN:2 * D_ATTN], kg_ref[...], ones, 1.0)
    v_ref[...] = proj[:, 2 * D_ATTN:3 * D_ATTN]
    u_ref[...] = proj[:, 3 * D_ATTN:]


def _in_proj(x2, norm_g, w_in, q_g, k_g):
    t = x2.shape[0]
    head_of_lane = jnp.arange(LANES) // HEAD_DIM
    ones = jnp.where(head_of_lane[:, None] == head_of_lane[None, :], 1.0 / HEAD_DIM, 0.0).astype(BF16)
    qg = jnp.tile(q_g.astype(F32), N_ATTN_HEADS)[None, :]
    kg = jnp.tile(k_g.astype(F32), N_ATTN_HEADS)[None, :]
    row = lambda i: (i, 0)
    fixed = lambda i: (0, 0)
    outs = pl.pallas_call(
        _inproj_kernel,
        grid=(t // ROW_TILE,),
        in_specs=[pl.BlockSpec((ROW_TILE, D_MODEL), row),
                  pl.BlockSpec((1, D_MODEL), fixed),
                  pl.BlockSpec((D_MODEL, D_IN_PROJ), fixed),
                  pl.BlockSpec((1, D_ATTN), fixed),
                  pl.BlockSpec((1, D_ATTN), fixed),
                  pl.BlockSpec((LANES, LANES), fixed)],
        out_specs=[pl.BlockSpec((ROW_TILE, D_ATTN), row)] * 3 + [pl.BlockSpec((ROW_TILE, D_SSM), row)],
        out_shape=[jax.ShapeDtypeStruct((t, D_ATTN), F32)] * 3 + [jax.ShapeDtypeStruct((t, D_SSM), F32)],
        compiler_params=_params(("arbitrary",)),
        name="in_proj",
    )(x2, norm_g.astype(F32)[None, :], w_in.astype(BF16), qg, kg, ones)
    return outs


def _rows(start, size, stride):
    return pl.ds(start, size, stride=stride) if stride > 1 else pl.ds(start, size)


def _attn_scores(q_ref, k_ref, v_ref, bias_ref, blk, tri_cur, tri_prev, head0):
    br, dil, n, r = blk
    qrows = _rows(r + dil * BLK * n, BLK, dil)
    qb = q_ref[qrows, :].astype(BF16)
    if n == 0:
        krows = qrows
        valid = tri_cur
    else:
        krows = _rows(r + dil * BLK * (n - 1), 2 * BLK, dil)
        valid = jnp.concatenate([tri_prev, tri_cur], axis=1)
    kb = k_ref[krows, :].astype(BF16)
    vb = v_ref[krows, :].astype(BF16)
    scores = []
    for h in range(2):
        mine = head0 if h == 0 else jnp.logical_not(head0)
        qh = jnp.where(mine, qb, jnp.zeros_like(qb))
        s = lax.dot_general(qh, kb, (((1,), (1,)), ((), ())), preferred_element_type=F32)
        bias = bias_ref[br, h, :, BLK:] if n == 0 else bias_ref[br, h]
        scores.append(jnp.where(valid, s + bias, NEG))
    return scores, vb, qrows


def _attn_values(scores, vb, qrows, br, pv_ref, den_ref, m_ref, head0):
    ones = jnp.ones_like(vb)
    pv, mx = [], []
    for h in range(2):
        mine = head0 if h == 0 else jnp.logical_not(head0)
        m = jnp.max(scores[h], axis=-1, keepdims=True)
        p = jnp.exp2(scores[h] - m).astype(BF16)
        pv.append(jnp.dot(p, jnp.where(mine, vb, ones), preferred_element_type=F32))
        mx.append(m)
    pv_ref[br, qrows, :] = jnp.where(head0, pv[0], pv[1])
    den_ref[br, qrows, :] = jnp.where(head0, pv[1], pv[0])
    m_ref[br, qrows, :] = jnp.where(head0, mx[0], mx[1])


def _attn_kernel(q_ref, k_ref, v_ref, bias_ref, o_ref, pv_ref, den_ref, m_ref, *, seq):
    qi = lax.broadcasted_iota(jnp.int32, (BLK, BLK), 0)
    kj = lax.broadcasted_iota(jnp.int32, (BLK, BLK), 1)
    tri_cur = qi >= kj
    tri_prev = kj >= qi
    head0 = lax.broadcasted_iota(jnp.int32, (1, LANES), 1) < HEAD_DIM
    nbr = len(DILATED_BRANCHES)
    blocks = []
    for br, (window, dil) in enumerate(DILATED_BRANCHES):
        assert window // dil == BLK
        blocks += [(br, dil, n, r) for r in range(dil) for n in range(seq // dil // BLK)]
    ahead = [_attn_scores(q_ref, k_ref, v_ref, bias_ref, b, tri_cur, tri_prev, head0)
             for b in blocks[:ATTN_LOOKAHEAD]]
    for i, blk in enumerate(blocks):
        if i + ATTN_LOOKAHEAD < len(blocks):
            ahead.append(_attn_scores(q_ref, k_ref, v_ref, bias_ref, blocks[i + ATTN_LOOKAHEAD],
                                      tri_cur, tri_prev, head0))
        _attn_values(*ahead.pop(0), blk[0], pv_ref, den_ref, m_ref, head0)
    m_all = [m_ref[br] for br in range(nbr)]
    m_top = functools.reduce(jnp.maximum, m_all)
    num = jnp.zeros((seq, LANES), F32)
    den = jnp.zeros((seq, LANES), F32)
    for br in range(nbr):
        w = jnp.exp2(m_all[br] - m_top)
        num = num + w * pv_ref[br]
        den = den + w * pltpu.roll(den_ref[br], HEAD_DIM, axis=1)
    o_ref[...] = num / den


def _attention(q, k, v, bias):
    b, s, _ = q.shape
    blk = pl.BlockSpec((None, s, LANES), lambda i, p: (i, 0, p))
    nbr = len(DILATED_BRANCHES)
    return pl.pallas_call(
        functools.partial(_attn_kernel, seq=s),
        grid=(b, D_ATTN // LANES),
        in_specs=[blk, blk, blk,
                  pl.BlockSpec((nbr, 2, BLK, 2 * BLK), lambda i, p: (0, p, 0, 0))],
        out_specs=blk,
        out_shape=jax.ShapeDtypeStruct((b, s, D_ATTN), F32),
        scratch_shapes=[pltpu.VMEM((nbr, s, LANES), F32)] * 3,
        compiler_params=_params(("arbitrary", "arbitrary")),
        name="attention",
    )(q, k, v, bias)


def _zoh_kernel(lr_ref, li_ref, dt_ref, lrr_ref, lir_ref, br_ref, bi_ref,
                are_ref, aim_ref, bbr_ref, bbi_ref):
    dt = jnp.exp(dt_ref[...])

    def zoh(lr, li):
        mag = jnp.exp(lr * dt)
        a_re, a_im = mag * jnp.cos(li * dt), mag * jnp.sin(li * dt)
        den = lr * lr + li * li
        f_re = ((a_re - 1.0) * lr + a_im * li) / den
        f_im = (a_im * lr - (a_re - 1.0) * li) / den
        return a_re, a_im, f_re, f_im

    a_re, a_im, _, _ = zoh(lr_ref[...], li_ref[...])
    are_ref[...] = a_re
    aim_ref[...] = a_im
    _, _, f_re, f_im = zoh(lrr_ref[...], lir_ref[...])
    br, bi = br_ref[...], bi_ref[...]
    bbr_ref[...] = f_re * br - f_im * bi
    bbi_ref[...] = f_re * bi + f_im * br


def _ssm_zoh(lam_re, lam_im, log_dt, b_re, b_im):
    g, n, c = b_re.shape
    rep = lambda a: jnp.repeat(a.astype(F32), c, axis=1)
    a_re, a_im, bb_re, bb_im = pl.pallas_call(
        _zoh_kernel,
        out_shape=[jax.ShapeDtypeStruct((g, n), F32)] * 2 + [jax.ShapeDtypeStruct((g, n * c), F32)] * 2,
        name="ssm_zoh",
    )(lam_re.astype(F32), lam_im.astype(F32), log_dt.astype(F32)[:, None], rep(lam_re), rep(lam_im),
      b_re.astype(F32).reshape(g, n * c), b_im.astype(F32).reshape(g, n * c))
    return a_re, a_im, bb_re.reshape(g, n, c), bb_im.reshape(g, n, c)


def _ssm_kernel(u_ref, wb_ref, wc_ref, are_ref, aim_ref, d_ref, gw_ref, gb_ref, gain_ref,
                o_ref, lhs_ref, bu_ref, ysel_ref, y_ref, sr_ref, si_ref, *, nb, chunk):
    half = N_SLABS // 2
    seqs = 2 * nb
    lane_blocks = 2 * SLAB_STATE // LANES
    rows_all = seqs * chunk

    @pl.when(pl.program_id(0) == 0)
    def _():
        sr_ref[...] = jnp.zeros_like(sr_ref)
        si_ref[...] = jnp.zeros_like(si_ref)
        lhs_ref[...] = jnp.zeros_like(lhs_ref)

    for b in range(nb):
        for m in range(N_SLABS):
            gh, mp = divmod(m, half)
            lhs_ref[2 * mp + gh, pl.ds(gh * nb + b, chunk, stride=seqs), :] = u_ref[b, :, LANES * m:LANES * (m + 1)]

    for mp in range(half):
        lhs = jnp.concatenate([lhs_ref[2 * mp], lhs_ref[2 * mp + 1]], axis=1).astype(BF16)
        bu = jnp.dot(lhs, wb_ref[mp], preferred_element_type=F32)
        for j in range(lane_blocks):
            bu_ref[lane_blocks * mp + j] = bu[:, LANES * j:LANES * (j + 1)]

    def load_state(rows, mp, part):
        j0 = lane_blocks * mp + part * (lane_blocks // 2)
        return jnp.concatenate([bu_ref[j0 + j, rows, :] for j in range(lane_blocks // 2)], axis=1)

    def store_state(rows, mp, part, val):
        j0 = lane_blocks * mp + part * (lane_blocks // 2)
        for j in range(lane_blocks // 2):
            bu_ref[j0 + j, rows, :] = val[:, LANES * j:LANES * (j + 1)]

    def step(t, carry):
        base = pl.multiple_of(t * seqs, seqs)
        rows = pl.ds(base, seqs)
        new = []
        for mp in range(half):
            xr, xi = carry[2 * mp], carry[2 * mp + 1]
            ar = are_ref[:, SLAB_STATE * mp:SLAB_STATE * (mp + 1)]
            ai = aim_ref[:, SLAB_STATE * mp:SLAB_STATE * (mp + 1)]
            nr = ar * xr - ai * xi + load_state(rows, mp, 0)
            ni = ar * xi + ai * xr + load_state(rows, mp, 1)
            store_state(rows, mp, 0, nr)
            store_state(rows, mp, 1, ni)
            new += [nr, ni]
        return tuple(new)

    init = []
    for mp in range(half):
        init += [sr_ref[:, SLAB_STATE * mp:SLAB_STATE * (mp + 1)], si_ref[:, SLAB_STATE * mp:SLAB_STATE * (mp + 1)]]
    final = lax.fori_loop(0, chunk, step, tuple(init), unroll=2)
    for mp in range(half):
        sr_ref[:, SLAB_STATE * mp:SLAB_STATE * (mp + 1)] = final[2 * mp]
        si_ref[:, SLAB_STATE * mp:SLAB_STATE * (mp + 1)] = final[2 * mp + 1]

    first_half = (lax.broadcasted_iota(jnp.int32, (rows_all, LANES), 0) & nb) == 0
    for mp in range(half):
        xs = jnp.concatenate([bu_ref[lane_blocks * mp + j] for j in range(lane_blocks)], axis=1).astype(BF16)
        yy = jnp.dot(xs, wc_ref[mp], preferred_element_type=F32)
        ysel_ref[mp] = jnp.where(first_half, yy[:, :LANES], yy[:, LANES:])
    for b in range(nb):
        for m in range(N_SLABS):
            gh, mp = divmod(m, half)
            cols = slice(LANES * m, LANES * (m + 1))
            y = ysel_ref[mp, pl.ds(gh * nb + b, chunk, stride=seqs), :]
            y_ref[b * chunk:(b + 1) * chunk, cols] = y + d_ref[:, cols] * u_ref[b, :, cols]

    y = _erf_gelu(y_ref[...])
    z = jnp.dot(y.astype(BF16), gw_ref[...], preferred_element_type=F32) + gb_ref[...]
    y = y * jax.nn.sigmoid(z)
    ms = jnp.mean(y * y, axis=-1, keepdims=True)
    yn = y * lax.rsqrt(ms + EPS) * gain_ref[...]
    for b in range(nb):
        o_ref[b] = yn[b * chunk:(b + 1) * chunk].astype(o_ref.dtype)


def _s5_mixer(u, lam_re, lam_im, log_dt, b_re, b_im, c_re, c_im, d_skip, glu_w, glu_b, out_gain):
    nb, s, _ = u.shape
    a_re, a_im, bb_re, bb_im = _ssm_zoh(lam_re, lam_im, log_dt, b_re, b_im)
    eye = jnp.eye(SSM_SLAB_GROUPS, dtype=F32)

    def in_slab(bb):
        w = jnp.einsum('mgnc,gh->mgchn', bb.reshape(N_SLABS, SSM_SLAB_GROUPS, SSM_STATE, SSM_GROUP), eye)
        return w.reshape(N_SLABS, LANES, SLAB_STATE)

    def out_slab(cc):
        w = jnp.einsum('mgcn,gh->mgnhc', cc.reshape(N_SLABS, SSM_SLAB_GROUPS, SSM_GROUP, SSM_STATE), eye)
        return w.reshape(N_SLABS, SLAB_STATE, LANES)

    wb = jnp.concatenate([in_slab(bb_re), in_slab(bb_im)], axis=2)
    wc = jnp.concatenate([out_slab(c_re.astype(F32)), -out_slab(c_im.astype(F32))], axis=1)
    hs = N_SLABS // 2
    wb = jnp.concatenate([wb[:hs], wb[hs:]], axis=1).astype(BF16)
    wc = jnp.concatenate([wc[:hs], wc[hs:]], axis=2).astype(BF16)
    half_states = (N_SSM_GROUPS // 2) * SSM_STATE

    def seq_rows(a):
        return jnp.repeat(a.reshape(2, half_states), nb, axis=0)

    fixed2 = lambda c: (0, 0)
    fixed3 = lambda c: (0, 0, 0)
    chunk = SSM_CHUNK
    return pl.pallas_call(
        functools.partial(_ssm_kernel, nb=nb, chunk=chunk),
        grid=(s // chunk,),
        in_specs=[pl.BlockSpec((nb, chunk, D_SSM), lambda c: (0, c, 0)),
                  pl.BlockSpec((N_SLABS // 2, 2 * LANES, 2 * SLAB_STATE), fixed3),
                  pl.BlockSpec((N_SLABS // 2, 2 * SLAB_STATE, 2 * LANES), fixed3),
                  pl.BlockSpec((2 * nb, half_states), fixed2),
                  pl.BlockSpec((2 * nb, half_states), fixed2),
                  pl.BlockSpec((1, D_SSM), fixed2),
                  pl.BlockSpec((D_SSM, D_SSM), fixed2),
                  pl.BlockSpec((1, D_SSM), fixed2),
                  pl.BlockSpec((1, D_SSM), fixed2)],
        out_specs=pl.BlockSpec((nb, chunk, D_SSM), lambda c: (0, c, 0)),
        out_shape=jax.ShapeDtypeStruct((nb, s, D_SSM), BF16),
        scratch_shapes=[pltpu.VMEM((N_SLABS, 2 * nb * chunk, LANES), F32),
                        pltpu.VMEM((2 * half_states // LANES, 2 * nb * chunk, LANES), F32),
                        pltpu.VMEM((N_SLABS // 2, 2 * nb * chunk, LANES), F32),
                        pltpu.VMEM((nb * chunk, D_SSM), F32),
                        pltpu.VMEM((2 * nb, half_states), F32),
                        pltpu.VMEM((2 * nb, half_states), F32)],
        compiler_params=_params(("arbitrary",)),
        name="ssm",
    )(u, wb, wc, seq_rows(a_re), seq_rows(a_im), d_skip.astype(F32).reshape(1, D_SSM),
      glu_w.astype(BF16), glu_b.astype(F32)[None, :], out_gain.astype(F32)[None, :])


def _outproj_kernel(a_ref, s_ref, x_ref, ag_ref, wa_ref, ws_ref, fg_ref, x1_ref, hn_ref):
    a = a_ref[...]
    ms = jnp.mean(a * a, axis=-1, keepdims=True)
    an = (a * lax.rsqrt(ms + EPS) * ag_ref[...]).astype(BF16)
    mixed = (jnp.dot(an, wa_ref[...], preferred_element_type=F32)
             + jnp.dot(s_ref[...], ws_ref[...], preferred_element_type=F32))
    x1 = x_ref[...] + mixed
    x1_ref[...] = x1
    ms1 = jnp.mean(x1 * x1, axis=-1, keepdims=True)
    hn_ref[...] = (x1 * lax.rsqrt(ms1 + EPS) * fg_ref[...]).astype(BF16)


def _out_proj(attn, ssm_n, x2, attn_g, w_out, ffn_g):
    t = x2.shape[0]
    row = lambda i: (i, 0)
    fixed = lambda i: (0, 0)
    w = w_out.astype(BF16)
    return pl.pallas_call(
        _outproj_kernel,
        grid=(t // ROW_TILE,),
        in_specs=[pl.BlockSpec((ROW_TILE, D_ATTN), row),
                  pl.BlockSpec((ROW_TILE, D_SSM), row),
                  pl.BlockSpec((ROW_TILE, D_MODEL), row),
                  pl.BlockSpec((1, D_ATTN), fixed),
                  pl.BlockSpec((D_ATTN, D_MODEL), fixed),
                  pl.BlockSpec((D_SSM, D_MODEL), fixed),
                  pl.BlockSpec((1, D_MODEL), fixed)],
        out_specs=[pl.BlockSpec((ROW_TILE, D_MODEL), row)] * 2,
        out_shape=[jax.ShapeDtypeStruct((t, D_MODEL), F32), jax.ShapeDtypeStruct((t, D_MODEL), BF16)],
        compiler_params=_params(("arbitrary",)),
        name="out_proj",
    )(attn, ssm_n, x2, attn_g.astype(F32)[None, :], w[:D_ATTN], w[D_ATTN:], ffn_g.astype(F32)[None, :])


def _take16(s, exact, index=None):
    if index is None:
        index = lax.broadcasted_iota(jnp.int32, s.shape, 0)
    vals = []
    for it in range(PEER_TOPK):
        m = jnp.max(s, axis=0, keepdims=True)
        if exact:
            first = jnp.min(jnp.where(s == m, index, jnp.iinfo(jnp.int32).max), axis=0, keepdims=True)
            hit = index == first
        else:
            hit = s == m
        s = jnp.where(hit, RANK_BASE * (1.0 + it / 32.0), s)
        vals.append(m)
    return vals, s


def _taken(marked):
    return jnp.logical_and(marked < 0.5 * RANK_BASE, marked > 2.0 * RANK_BASE)


def _rank_of(marked):
    rank = jnp.floor((marked * (1.0 / RANK_BASE) - 1.0) * 32.0 + 0.5)
    return jnp.where(_taken(marked), rank, NO_RANK)


def _stack_rows(rows, n):
    iota = lax.broadcasted_iota(jnp.int32, (n, rows[0].shape[1]), 0)
    out = jnp.zeros((n, rows[0].shape[1]), F32)
    for i, r in enumerate(rows):
        out = jnp.where(iota == i, r, out)
    return out


def _count(mask):
    return jnp.sum(jnp.where(mask, 1.0, 0.0), axis=0, keepdims=True)


def _route_head(s1, s2, exact):
    tb = s1.shape[1]
    v1, marked1 = _take16(s1, exact)
    v2, marked2 = _take16(s2, exact)
    rank1, rank2 = _rank_of(marked1), _rank_of(marked2)
    v2_all = _stack_rows(v2, PEER_TOPK)
    v1_hi = _stack_rows(v1[SUBLANES:], SUBLANES)
    sub = lax.broadcasted_iota(jnp.int32, (SUBLANES, tb), 0)
    pieces = [v1[0] + v2_all]
    flats = [lax.broadcasted_iota(jnp.int32, (PEER_TOPK, tb), 0)]
    for a in range(1, SUBLANES):
        limit = PEER_TOPK // (a + 1)
        pieces.append(jnp.where(sub < limit, v1[a] + v2_all[:SUBLANES], -jnp.inf))
        flats.append(a * PEER_TOPK + sub)
    pieces.append(v1_hi + v2[0])
    flats.append((sub + SUBLANES) * PEER_TOPK)
    cand = jnp.concatenate(pieces, axis=0)
    flat = jnp.concatenate(flats, axis=0)
    _, marked = _take16(cand, exact, flat)
    sel = jnp.where(_taken(marked), 1.0, 0.0)
    top = v1[0] + v2[0]
    z = jnp.sum(sel * jnp.exp(cand - top), axis=0, keepdims=True)
    cnt = [jnp.sum(sel[:PEER_TOPK], axis=0, keepdims=True)]
    for a in range(1, SUBLANES):
        lo = PEER_TOPK + SUBLANES * (a - 1)
        cnt.append(jnp.sum(sel[lo:lo + SUBLANES], axis=0, keepdims=True))
    lo = PEER_TOPK + SUBLANES * (SUBLANES - 1)
    for i in range(SUBLANES):
        cnt.append(sel[lo + i:lo + i + 1])
    c1 = jnp.zeros(rank1.shape, F32)
    for a in range(PEER_TOPK):
        c1 = jnp.where(rank1 == float(a), cnt[a], c1)
    e1 = jnp.exp(s1 - v1[0])
    e2n = jnp.exp(s2 - v2[0]) / z
    taken = jnp.maximum(jnp.maximum(_count(_taken(marked1)), _count(_taken(marked2))), jnp.sum(sel, axis=0, keepdims=True))
    return c1, e1, rank2, e2n, taken


def _route_one(h, qt_ref, k1_ref, k2_ref, c1_ref, e1_ref, r2_ref, e2_ref, exact):
    half = PEER_QDIM // 2
    q1 = qt_ref[PEER_QDIM * h:PEER_QDIM * h + half, :].astype(BF16)
    q2 = qt_ref[PEER_QDIM * h + half:PEER_QDIM * (h + 1), :].astype(BF16)
    s1 = jnp.dot(k1_ref[h], q1, preferred_element_type=F32)
    s2 = jnp.dot(k2_ref[h], q2, preferred_element_type=F32)
    c1, e1, r2, e2n, taken = _route_head(s1, s2, exact)
    c1_ref[h] = c1
    e1_ref[h] = e1
    r2_ref[h] = r2.astype(r2_ref.dtype)
    e2_ref[h] = e2n.astype(e2_ref.dtype)
    return taken


def _route_kernel(hn_ref, wq_ref, k1_ref, k2_ref, c1_ref, e1_ref, r2_ref, e2_ref, qt_ref):
    qt_ref[...] = lax.dot_general(wq_ref[...], hn_ref[...], (((1,), (1,)), ((), ())), preferred_element_type=F32)
    refs = (qt_ref, k1_ref, k2_ref, c1_ref, e1_ref, r2_ref, e2_ref)
    taken = [_route_one(h, *refs, exact=False) for h in range(PEER_HEADS)]
    for h in range(PEER_HEADS):
        @pl.when(jnp.max(taken[h]) > float(PEER_TOPK))
        def _():
            _route_one(h, *refs, exact=True)


def _peer_route(hn, w_q, keys1, keys2):
    t = hn.shape[0]
    blk = pl.BlockSpec((PEER_HEADS, PEER_KEYS, ROUTE_TILE), lambda i: (0, 0, i))
    fixed3 = lambda i: (0, 0, 0)
    shape = (PEER_HEADS, PEER_KEYS, t)
    return pl.pallas_call(
        _route_kernel,
        grid=(t // ROUTE_TILE,),
        in_specs=[pl.BlockSpec((ROUTE_TILE, D_MODEL), lambda i: (i, 0)),
                  pl.BlockSpec((PEER_HEADS * PEER_QDIM, D_MODEL), lambda i: (0, 0)),
                  pl.BlockSpec((PEER_HEADS, PEER_KEYS, PEER_QDIM // 2), fixed3),
                  pl.BlockSpec((PEER_HEADS, PEER_KEYS, PEER_QDIM // 2), fixed3)],
        out_specs=[blk] * 4,
        out_shape=[jax.ShapeDtypeStruct(shape, F32), jax.ShapeDtypeStruct(shape, F32),
                   jax.ShapeDtypeStruct(shape, BF16), jax.ShapeDtypeStruct(shape, BF16)],
        scratch_shapes=[pltpu.VMEM((PEER_HEADS * PEER_QDIM, ROUTE_TILE), F32)],
        compiler_params=_params(("arbitrary",)),
        name="peer_route",
    )(hn, w_q.T.astype(BF16), keys1.astype(BF16), keys2.astype(BF16))


def _peer_gated(c1_ref, e1_ref, r2_ref, e2_ref, at_ref, chunk, valid):
    packed = 2 * SUBLANES
    tiles = PEER_SUB // PEER_KEYS
    gs = []
    for tl in range(tiles):
        tile = chunk * tiles + tl
        gate = jnp.zeros((PEER_KEYS // packed, packed, PEER_TOKENS), BF16)
        for h in range(PEER_HEADS):
            c1 = jnp.broadcast_to(c1_ref[h, pl.ds(tile, 1), :], (packed, PEER_TOKENS)).astype(BF16)
            e1 = jnp.broadcast_to(e1_ref[h, pl.ds(tile, 1), :], (packed, PEER_TOKENS)).astype(BF16)
            r2 = r2_ref[h].reshape(gate.shape)
            e2 = e2_ref[h].reshape(gate.shape)
            gate = gate + jnp.where(r2 < c1[None], e1[None] * e2, jnp.zeros_like(e2))
        a = at_ref[PEER_KEYS * tl:PEER_KEYS * (tl + 1), :].astype(BF16)
        gs.append(gate.reshape(PEER_KEYS, PEER_TOKENS) * _erf_gelu(a))
    g = jnp.concatenate(gs, axis=0)
    return jnp.where(valid, g, jnp.zeros_like(g))


def _peer_kernel(hn_ref, u_ref, *refs, n_chunks):
    n = PEER_EXPERTS_STEP // PEER_SUB
    vt_refs, route = refs[:n], refs[n:n + 4]
    x1_ref, o_ref, acc_ref, at_ref = refs[n + 4:]
    e = pl.program_id(1)
    last = pl.num_programs(1) - 1
    nt = (((1,), (1,)), ((), ()))

    @pl.when(e == 0)
    def _():
        acc_ref[...] = jnp.zeros_like(acc_ref)
        at_ref[n - 1] = jnp.zeros((PEER_SUB, PEER_TOKENS), F32)

    hn = hn_ref[...]
    total = None
    for c in range(n):
        at_ref[c] = lax.dot_general(u_ref[PEER_SUB * c:PEER_SUB * (c + 1), :], hn, nt, preferred_element_type=F32)
        prev = n * e + c - 1
        valid = (e > 0) if c == 0 else (e < last)
        g = _peer_gated(*route, at_ref.at[(c - 1) % n], jnp.clip(prev, 0, n_chunks - 1), valid)
        part = jnp.dot(vt_refs[c][...], g, preferred_element_type=F32)
        total = part if total is None else total + part
    acc_ref[...] += total

    @pl.when(e == last)
    def _():
        o_ref[...] = x1_ref[...] + acc_ref[...].T


def _peer_mix(hn, x1, u_tab, v_tab, c1, e1, r2, e2n):
    t = hn.shape[0]
    n = PEER_EXPERTS_STEP // PEER_SUB
    n_steps = u_tab.shape[0] // PEER_EXPERTS_STEP
    n_chunks = n * n_steps
    route = pl.BlockSpec((PEER_HEADS, PEER_KEYS, PEER_TOKENS), lambda i, e: (0, 0, i))
    tok = lambda i, e: (i, 0)
    vt = v_tab.T.astype(BF16)
    vt_specs = [pl.BlockSpec((D_MODEL, PEER_SUB),
                             functools.partial(lambda i, e, c: (0, jnp.clip(n * e + c - 1, 0, n_chunks - 1)), c=c))
                for c in range(n)]
    return pl.pallas_call(
        functools.partial(_peer_kernel, n_chunks=n_chunks),
        grid=(t // PEER_TOKENS, n_steps + 1),
        in_specs=[pl.BlockSpec((PEER_TOKENS, D_MODEL), tok),
                  pl.BlockSpec((PEER_EXPERTS_STEP, D_MODEL), lambda i, e: (jnp.minimum(e, n_steps - 1), 0)),
                  *vt_specs,
                  route, route, route, route,
                  pl.BlockSpec((PEER_TOKENS, D_MODEL), tok)],
        out_specs=pl.BlockSpec((PEER_TOKENS, D_MODEL), tok),
        out_shape=jax.ShapeDtypeStruct((t, D_MODEL), F32),
        scratch_shapes=[pltpu.VMEM((D_MODEL, PEER_TOKENS), F32),
                        pltpu.VMEM((n, PEER_SUB, PEER_TOKENS), F32)],
        compiler_params=_params(("arbitrary", "arbitrary")),
        name="peer_mix",
    )(hn, u_tab.astype(BF16), *([vt] * n), c1, e1, r2, e2n, x1)


def kernel(x, norm_mix_g, w_in, q_norm_g, k_norm_g, rel_bias, ssm_lambda_re, ssm_lambda_im, ssm_log_dt,
           ssm_b_re, ssm_b_im, ssm_c_re, ssm_c_im, ssm_d, ssm_glu_w, ssm_glu_b, attn_out_g, ssm_out_g,
           w_out, norm_ffn_g, peer_w_q, peer_keys1, peer_keys2, peer_u, peer_v):
    b, s, d = x.shape
    x2 = x.reshape(b * s, d)
    q, k, v, u = _in_proj(x2, norm_mix_g, w_in, q_norm_g, k_norm_g)
    bias = _bias_tables(rel_bias)
    attn = _attention(q.reshape(b, s, D_ATTN), k.reshape(b, s, D_ATTN), v.reshape(b, s, D_ATTN), bias)
    ssm_n = _s5_mixer(u.reshape(b, s, D_SSM), ssm_lambda_re, ssm_lambda_im, ssm_log_dt, ssm_b_re, ssm_b_im,
                      ssm_c_re, ssm_c_im, ssm_d, ssm_glu_w, ssm_glu_b, ssm_out_g)
    x1, hn = _out_proj(attn.reshape(b * s, D_ATTN), ssm_n.reshape(b * s, D_SSM), x2, attn_out_g, w_out, norm_ffn_g)
    c1, e1, r2, e2n = _peer_route(hn, peer_w_q, peer_keys1, peer_keys2)
    out = _peer_mix(hn, x1, peer_u, peer_v, c1, e1, r2, e2n)
    return out.reshape(b, s, d).astype(x.dtype)
```

```python
import functools
import math

import jax
import jax.numpy as jnp
import numpy as np
from jax import lax
from jax.experimental import pallas as pl
from jax.experimental.pallas import tpu as pltpu

F32 = jnp.float32
BF16 = jnp.bfloat16

D_MODEL = 2048
HEAD_DIM = 64
N_ATTN_HEADS = 16
D_ATTN = N_ATTN_HEADS * HEAD_DIM
SSM_GROUP = 16
N_SSM_GROUPS = 64
D_SSM = N_SSM_GROUPS * SSM_GROUP
SSM_STATE = 64
D_IN_PROJ = 3 * D_ATTN + D_SSM
DILATED_BRANCHES = ((128, 1), (512, 4), (2048, 16))
BLK = 128
N_BUCKETS = 32
MAX_DISTANCE = 2048
PEER_HEADS = 8
PEER_KEYS = 128
PEER_QDIM = 256
PEER_TOPK = 16
EPS = 1e-6
NEG = -1e30
LOG2E = math.log2(math.e)

LANES = 128
SUBLANES = 8
VMEM_LIMIT = 56 * 1024 * 1024

ROW_TILE = 256
SSM_CHUNK = 128
SSM_SLAB_GROUPS = LANES // SSM_GROUP
N_SLABS = N_SSM_GROUPS // SSM_SLAB_GROUPS
SLAB_STATE = SSM_SLAB_GROUPS * SSM_STATE
ATTN_LOOKAHEAD = 2
ROUTE_TILE = 256
PEER_TOKENS = 512
PEER_EXPERTS_STEP = 512
PEER_SUB = 256
NO_RANK = 99.0
RANK_BASE = -2.0 ** 100


def _params(sem, vmem=VMEM_LIMIT):
    return pltpu.CompilerParams(dimension_semantics=sem, vmem_limit_bytes=vmem)


def _erf_gelu(x):
    return 0.5 * x * (1.0 + lax.erf(x * math.sqrt(0.5)))


def _t5_bucket(dist):
    max_exact = N_BUCKETS // 2
    n = np.maximum(dist, 0)
    nf = np.maximum(n, 1).astype(np.float32)
    large = max_exact + (np.log(nf / np.float32(max_exact)) / np.float32(math.log(MAX_DISTANCE / max_exact))
                         * np.float32(N_BUCKETS - max_exact)).astype(np.int32)
    large = np.minimum(large, N_BUCKETS - 1)
    return np.where(n < max_exact, n, large)


def _bias_kernel(bkt_ref, rb_ref, out_ref):
    bkt = bkt_ref[0]
    qi = lax.broadcasted_iota(jnp.int32, (BLK, 2 * BLK), 0)
    kj = lax.broadcasted_iota(jnp.int32, (BLK, 2 * BLK), 1)
    valid = jnp.where(kj < BLK, kj - qi, qi - (kj - BLK)) >= 0
    for h in range(N_ATTN_HEADS):
        acc = jnp.zeros((BLK, 2 * BLK), F32)
        for b in range(N_BUCKETS):
            acc = jnp.where(bkt == b, rb_ref[b * N_ATTN_HEADS + h], acc)
        out_ref[0, h] = jnp.where(valid, acc * LOG2E, NEG)


def _bias_tables(rel_bias):
    qi = np.arange(BLK)[:, None]
    kj = np.arange(2 * BLK)[None, :]
    rel = qi - kj + BLK
    buckets = jnp.asarray(np.stack([_t5_bucket(rel * dil) for _, dil in DILATED_BRANCHES]).astype(np.int32))
    nbr = len(DILATED_BRANCHES)
    return pl.pallas_call(
        _bias_kernel,
        grid=(nbr,),
        in_specs=[pl.BlockSpec((1, BLK, 2 * BLK), lambda i: (i, 0, 0)),
                  pl.BlockSpec(memory_space=pltpu.SMEM)],
        out_specs=pl.BlockSpec((1, N_ATTN_HEADS, BLK, 2 * BLK), lambda i: (i, 0, 0, 0)),
        out_shape=jax.ShapeDtypeStruct((nbr, N_ATTN_HEADS, BLK, 2 * BLK), F32),
        compiler_params=_params(("arbitrary",)),
        name="bias_table",
    )(buckets, rel_bias.astype(F32).reshape(N_BUCKETS * N_ATTN_HEADS))


def _head_rmsnorm(z, gain, ones, scale):
    outs = []
    for c in range(z.shape[1] // LANES):
        zc = z[:, LANES * c:LANES * (c + 1)]
        sq = zc * zc
        hi = sq.astype(BF16)
        lo = (sq - hi.astype(F32)).astype(BF16)
        msq = (jnp.dot(hi, ones, preferred_element_type=F32)
               + jnp.dot(lo, ones, preferred_element_type=F32))
        y = zc * lax.rsqrt(msq + EPS)
        outs.append(y * gain[:, LANES * c:LANES * (c + 1)] * scale)
    return jnp.concatenate(outs, axis=1)


def _inproj_kernel(x_ref, g_ref, w_ref, qg_ref, kg_ref, ones_ref, q_ref, k_ref, v_ref, u_ref):
    x = x_ref[...]
    ms = jnp.mean(x * x, axis=-1, keepdims=True)
    h = (x * lax.rsqrt(ms + EPS) * g_ref[...]).astype(BF16)
    proj = jnp.dot(h, w_ref[...], preferred_element_type=F32)
    ones = ones_ref[...]
    q_ref[...] = _head_rmsnorm(proj[:, :D_ATTN], qg_ref[...], ones, LOG2E / math.sqrt(HEAD_DIM))
    k_ref[...] = _head_rmsnorm(proj[:, D_ATTN:2 * D_ATTN], kg_ref[...], ones, 1.0)
    v_ref[...] = proj[:, 2 * D_ATTN:3 * D_ATTN]
    u_ref[...] = proj[:, 3 * D_ATTN:]


def _in_proj(x2, norm_g, w_in, q_g, k_g):
    t = x2.shape[0]
    head_of_lane = jnp.arange(LANES) // HEAD_DIM
    ones = jnp.where(head_of_lane[:, None] == head_of_lane[None, :], 1.0 / HEAD_DIM, 0.0).astype(BF16)
    qg = jnp.tile(q_g.astype(F32), N_ATTN_HEADS)[None, :]
    kg = jnp.tile(k_g.astype(F32), N_ATTN_HEADS)[None, :]
    row = lambda i: (i, 0)
    fixed = lambda i: (0, 0)
    outs = pl.pallas_call(
        _inproj_kernel,
        grid=(t // ROW_TILE,),
        in_specs=[pl.BlockSpec((ROW_TILE, D_MODEL), row),
                  pl.BlockSpec((1, D_MODEL), fixed),
                  pl.BlockSpec((D_MODEL, D_IN_PROJ), fixed),
                  pl.BlockSpec((1, D_ATTN), fixed),
                  pl.BlockSpec((1, D_ATTN), fixed),
                  pl.BlockSpec((LANES, LANES), fixed)],
        out_specs=[pl.BlockSpec((ROW_TILE, D_ATTN), row)] * 3 + [pl.BlockSpec((ROW_TILE, D_SSM), row)],
        out_shape=[jax.ShapeDtypeStruct((t, D_ATTN), F32)] * 3 + [jax.ShapeDtypeStruct((t, D_SSM), F32)],
        compiler_params=_params(("arbitrary",)),
        name="in_proj",
    )(x2, norm_g.astype(F32)[None, :], w_in.astype(BF16), qg, kg, ones)
    return outs


def _rows(start, size, stride):
    return pl.ds(start, size, stride=stride) if stride > 1 else pl.ds(start, size)


def _attn_scores(q_ref, k_ref, v_ref, bias_ref, blk, head0):
    br, dil, n, r = blk
    qrows = _rows(r + dil * BLK * n, BLK, dil)
    qb = q_ref[qrows, :].astype(BF16)
    krows = qrows if n == 0 else _rows(r + dil * BLK * (n - 1), 2 * BLK, dil)
    kb = k_ref[krows, :].astype(BF16)
    vb = v_ref[krows, :].astype(BF16)
    scores = []
    for h in range(2):
        mine = head0 if h == 0 else jnp.logical_not(head0)
        qh = jnp.where(mine, qb, jnp.zeros_like(qb))
        s = lax.dot_general(qh, kb, (((1,), (1,)), ((), ())), preferred_element_type=F32)
        scores.append(s + (bias_ref[br, h, :, BLK:] if n == 0 else bias_ref[br, h]))
    return scores, vb, qrows


def _attn_values(scores, vb, qrows, br, pv_ref, den_ref, m_ref, head0):
    ones = jnp.ones_like(vb)
    pv, mx = [], []
    for h in range(2):
        mine = head0 if h == 0 else jnp.logical_not(head0)
        m = jnp.max(scores[h], axis=-1, keepdims=True)
        p = jnp.exp2(scores[h] - m).astype(BF16)
        pv.append(jnp.dot(p, jnp.where(mine, vb, ones), preferred_element_type=F32))
        mx.append(m)
    pv_ref[br, qrows, :] = jnp.where(head0, pv[0], pv[1])
    den_ref[br, qrows, :] = jnp.where(head0, pv[1], pv[0])
    m_ref[br, qrows, :] = jnp.where(head0, mx[0], mx[1])


def _attn_kernel(q_ref, k_ref, v_ref, bias_ref, o_ref, pv_ref, den_ref, m_ref, *, seq):
    head0 = lax.broadcasted_iota(jnp.int32, (1, LANES), 1) < HEAD_DIM
    nbr = len(DILATED_BRANCHES)
    blocks = []
    for br, (window, dil) in enumerate(DILATED_BRANCHES):
        assert window // dil == BLK
        blocks += [(br, dil, n, r) for r in range(dil) for n in range(seq // dil // BLK)]
    ahead = [_attn_scores(q_ref, k_ref, v_ref, bias_ref, b, head0) for b in blocks[:ATTN_LOOKAHEAD]]
    for i, blk in enumerate(blocks):
        if i + ATTN_LOOKAHEAD < len(blocks):
            ahead.append(_attn_scores(q_ref, k_ref, v_ref, bias_ref, blocks[i + ATTN_LOOKAHEAD], head0))
        _attn_values(*ahead.pop(0), blk[0], pv_ref, den_ref, m_ref, head0)
    m_all = [m_ref[br] for br in range(nbr)]
    m_top = functools.reduce(jnp.maximum, m_all)
    num = jnp.zeros((seq, LANES), F32)
    den = jnp.zeros((seq, LANES), F32)
    for br in range(nbr):
        w = jnp.exp2(m_all[br] - m_top)
        num = num + w * pv_ref[br]
        den = den + w * pltpu.roll(den_ref[br], HEAD_DIM, axis=1)
    o_ref[...] = num / den


def _attention(q, k, v, bias):
    b, s, _ = q.shape
    blk = pl.BlockSpec((None, s, LANES), lambda i, p: (i, 0, p))
    nbr = len(DILATED_BRANCHES)
    return pl.pallas_call(
        functools.partial(_attn_kernel, seq=s),
        grid=(b, D_ATTN // LANES),
        in_specs=[blk, blk, blk,
                  pl.BlockSpec((nbr, 2, BLK, 2 * BLK), lambda i, p: (0, p, 0, 0))],
        out_specs=blk,
        out_shape=jax.ShapeDtypeStruct((b, s, D_ATTN), F32),
        scratch_shapes=[pltpu.VMEM((nbr, s, LANES), F32)] * 3,
        compiler_params=_params(("arbitrary", "arbitrary")),
        name="attention",
    )(q, k, v, bias)


def _zoh_kernel(lr_ref, li_ref, dt_ref, lrr_ref, lir_ref, br_ref, bi_ref,
                are_ref, aim_ref, bbr_ref, bbi_ref):
    dt = jnp.exp(dt_ref[...])

    def zoh(lr, li):
        mag = jnp.exp(lr * dt)
        a_re, a_im = mag * jnp.cos(li * dt), mag * jnp.sin(li * dt)
        den = lr * lr + li * li
        f_re = ((a_re - 1.0) * lr + a_im * li) / den
        f_im = (a_im * lr - (a_re - 1.0) * li) / den
        return a_re, a_im, f_re, f_im

    a_re, a_im, _, _ = zoh(lr_ref[...], li_ref[...])
    are_ref[...] = a_re
    aim_ref[...] = a_im
    _, _, f_re, f_im = zoh(lrr_ref[...], lir_ref[...])
    br, bi = br_ref[...], bi_ref[...]
    bbr_ref[...] = f_re * br - f_im * bi
    bbi_ref[...] = f_re * bi + f_im * br


def _ssm_zoh(lam_re, lam_im, log_dt, b_re, b_im):
    g, n, c = b_re.shape
    rep = lambda a: jnp.repeat(a.astype(F32), c, axis=1)
    a_re, a_im, bb_re, bb_im = pl.pallas_call(
        _zoh_kernel,
        out_shape=[jax.ShapeDtypeStruct((g, n), F32)] * 2 + [jax.ShapeDtypeStruct((g, n * c), F32)] * 2,
        name="ssm_zoh",
    )(lam_re.astype(F32), lam_im.astype(F32), log_dt.astype(F32)[:, None], rep(lam_re), rep(lam_im),
      b_re.astype(F32).reshape(g, n * c), b_im.astype(F32).reshape(g, n * c))
    return a_re, a_im, bb_re.reshape(g, n, c), bb_im.reshape(g, n, c)


def _ssm_kernel(u_ref, wb_ref, wc_ref, are_ref, aim_ref, d_ref, gw_ref, gb_ref, gain_ref,
                o_ref, lhs_ref, bu_ref, ysel_ref, y_ref, sr_ref, si_ref, *, nb, chunk):
    half = N_SLABS // 2
    seqs = 2 * nb
    lane_blocks = 2 * SLAB_STATE // LANES
    rows_all = seqs * chunk

    @pl.when(pl.program_id(0) == 0)
    def _():
        sr_ref[...] = jnp.zeros_like(sr_ref)
        si_ref[...] = jnp.zeros_like(si_ref)
        lhs_ref[...] = jnp.zeros_like(lhs_ref)

    for b in range(nb):
        for m in range(N_SLABS):
            gh, mp = divmod(m, half)
            lhs_ref[2 * mp + gh, pl.ds(gh * nb + b, chunk, stride=seqs), :] = u_ref[b, :, LANES * m:LANES * (m + 1)]

    for mp in range(half):
        lhs = jnp.concatenate([lhs_ref[2 * mp], lhs_ref[2 * mp + 1]], axis=1).astype(BF16)
        bu = jnp.dot(lhs, wb_ref[mp], preferred_element_type=F32)
        for j in range(lane_blocks):
            bu_ref[lane_blocks * mp + j] = bu[:, LANES * j:LANES * (j + 1)]

    def load_state(rows, mp, part):
        j0 = lane_blocks * mp + part * (lane_blocks // 2)
        return jnp.concatenate([bu_ref[j0 + j, rows, :] for j in range(lane_blocks // 2)], axis=1)

    def store_state(rows, mp, part, val):
        j0 = lane_blocks * mp + part * (lane_blocks // 2)
        for j in range(lane_blocks // 2):
            bu_ref[j0 + j, rows, :] = val[:, LANES * j:LANES * (j + 1)]

    def step(t, carry):
        base = pl.multiple_of(t * seqs, seqs)
        rows = pl.ds(base, seqs)
        new = []
        for mp in range(half):
            xr, xi = carry[2 * mp], carry[2 * mp + 1]
            ar = are_ref[:, SLAB_STATE * mp:SLAB_STATE * (mp + 1)]
            ai = aim_ref[:, SLAB_STATE * mp:SLAB_STATE * (mp + 1)]
            nr = ar * xr - ai * xi + load_state(rows, mp, 0)
            ni = ar * xi + ai * xr + load_state(rows, mp, 1)
            store_state(rows, mp, 0, nr)
            store_state(rows, mp, 1, ni)
            new += [nr, ni]
        return tuple(new)

    init = []
    for mp in range(half):
        init += [sr_ref[:, SLAB_STATE * mp:SLAB_STATE * (mp + 1)], si_ref[:, SLAB_STATE * mp:SLAB_STATE * (mp + 1)]]
    final = lax.fori_loop(0, chunk, step, tuple(init), unroll=2)
    for mp in range(half):
        sr_ref[:, SLAB_STATE * mp:SLAB_STATE * (mp + 1)] = final[2 * mp]
        si_ref[:, SLAB_STATE * mp:SLAB_STATE * (mp + 1)] = final[2 * mp + 1]

    first_half = (lax.broadcasted_iota(jnp.int32, (rows_all, LANES), 0) & nb) == 0
    for mp in range(half):
        xs = jnp.concatenate([bu_ref[lane_blocks * mp + j] for j in range(lane_blocks)], axis=1).astype(BF16)
        yy = jnp.dot(xs, wc_ref[mp], preferred_element_type=F32)
        ysel_ref[mp] = jnp.where(first_half, yy[:, :LANES], yy[:, LANES:])
    for b in range(nb):
        for m in range(N_SLABS):
            gh, mp = divmod(m, half)
            cols = slice(LANES * m, LANES * (m + 1))
            y = ysel_ref[mp, pl.ds(gh * nb + b, chunk, stride=seqs), :]
            y_ref[b * chunk:(b + 1) * chunk, cols] = y + d_ref[:, cols] * u_ref[b, :, cols]

    y = _erf_gelu(y_ref[...])
    z = jnp.dot(y.astype(BF16), gw_ref[...], preferred_element_type=F32) + gb_ref[...]
    y = y * jax.nn.sigmoid(z)
    ms = jnp.mean(y * y, axis=-1, keepdims=True)
    yn = y * lax.rsqrt(ms + EPS) * gain_ref[...]
    for b in range(nb):
        o_ref[b] = yn[b * chunk:(b + 1) * chunk].astype(o_ref.dtype)


def _s5_mixer(u, lam_re, lam_im, log_dt, b_re, b_im, c_re, c_im, d_skip, glu_w, glu_b, out_gain):
    nb, s, _ = u.shape
    a_re, a_im, bb_re, bb_im = _ssm_zoh(lam_re, lam_im, log_dt, b_re, b_im)
    eye = jnp.eye(SSM_SLAB_GROUPS, dtype=F32)

    def in_slab(bb):
        w = jnp.einsum('mgnc,gh->mgchn', bb.reshape(N_SLABS, SSM_SLAB_GROUPS, SSM_STATE, SSM_GROUP), eye)
        return w.reshape(N_SLABS, LANES, SLAB_STATE)

    def out_slab(cc):
        w = jnp.einsum('mgcn,gh->mgnhc', cc.reshape(N_SLABS, SSM_SLAB_GROUPS, SSM_GROUP, SSM_STATE), eye)
        return w.reshape(N_SLABS, SLAB_STATE, LANES)

    wb = jnp.concatenate([in_slab(bb_re), in_slab(bb_im)], axis=2)
    wc = jnp.concatenate([out_slab(c_re.astype(F32)), -out_slab(c_im.astype(F32))], axis=1)
    hs = N_SLABS // 2
    wb = jnp.concatenate([wb[:hs], wb[hs:]], axis=1).astype(BF16)
    wc = jnp.concatenate([wc[:hs], wc[hs:]], axis=2).astype(BF16)
    half_states = (N_SSM_GROUPS // 2) * SSM_STATE

    def seq_rows(a):
        return jnp.repeat(a.reshape(2, half_states), nb, axis=0)

    fixed2 = lambda c: (0, 0)
    fixed3 = lambda c: (0, 0, 0)
    chunk = SSM_CHUNK
    return pl.pallas_call(
        functools.partial(_ssm_kernel, nb=nb, chunk=chunk),
        grid=(s // chunk,),
        in_specs=[pl.BlockSpec((nb, chunk, D_SSM), lambda c: (0, c, 0)),
                  pl.BlockSpec((N_SLABS // 2, 2 * LANES, 2 * SLAB_STATE), fixed3),
                  pl.BlockSpec((N_SLABS // 2, 2 * SLAB_STATE, 2 * LANES), fixed3),
                  pl.BlockSpec((2 * nb, half_states), fixed2),
                  pl.BlockSpec((2 * nb, half_states), fixed2),
                  pl.BlockSpec((1, D_SSM), fixed2),
                  pl.BlockSpec((D_SSM, D_SSM), fixed2),
                  pl.BlockSpec((1, D_SSM), fixed2),
                  pl.BlockSpec((1, D_SSM), fixed2)],
        out_specs=pl.BlockSpec((nb, chunk, D_SSM), lambda c: (0, c, 0)),
        out_shape=jax.ShapeDtypeStruct((nb, s, D_SSM), BF16),
        scratch_shapes=[pltpu.VMEM((N_SLABS, 2 * nb * chunk, LANES), F32),
                        pltpu.VMEM((2 * half_states // LANES, 2 * nb * chunk, LANES), F32),
                        pltpu.VMEM((N_SLABS // 2, 2 * nb * chunk, LANES), F32),
                        pltpu.VMEM((nb * chunk, D_SSM), F32),
                        pltpu.VMEM((2 * nb, half_states), F32),
                        pltpu.VMEM((2 * nb, half_states), F32)],
        compiler_params=_params(("arbitrary",)),
        name="ssm",
    )(u, wb, wc, seq_rows(a_re), seq_rows(a_im), d_skip.astype(F32).reshape(1, D_SSM),
      glu_w.astype(BF16), glu_b.astype(F32)[None, :], out_gain.astype(F32)[None, :])


def _outproj_kernel(a_ref, s_ref, x_ref, ag_ref, wa_ref, ws_ref, fg_ref, x1_ref, hn_ref):
    a = a_ref[...]
    ms = jnp.mean(a * a, axis=-1, keepdims=True)
    an = (a * lax.rsqrt(ms + EPS) * ag_ref[...]).astype(BF16)
    mixed = (jnp.dot(an, wa_ref[...], preferred_element_type=F32)
             + jnp.dot(s_ref[...], ws_ref[...], preferred_element_type=F32))
    x1 = x_ref[...] + mixed
    x1_ref[...] = x1
    ms1 = jnp.mean(x1 * x1, axis=-1, keepdims=True)
    hn_ref[...] = (x1 * lax.rsqrt(ms1 + EPS) * fg_ref[...]).astype(BF16)


def _out_proj(attn, ssm_n, x2, attn_g, w_out, ffn_g):
    t = x2.shape[0]
    row = lambda i: (i, 0)
    fixed = lambda i: (0, 0)
    w = w_out.astype(BF16)
    return pl.pallas_call(
        _outproj_kernel,
        grid=(t // ROW_TILE,),
        in_specs=[pl.BlockSpec((ROW_TILE, D_ATTN), row),
                  pl.BlockSpec((ROW_TILE, D_SSM), row),
                  pl.BlockSpec((ROW_TILE, D_MODEL), row),
                  pl.BlockSpec((1, D_ATTN), fixed),
                  pl.BlockSpec((D_ATTN, D_MODEL), fixed),
                  pl.BlockSpec((D_SSM, D_MODEL), fixed),
                  pl.BlockSpec((1, D_MODEL), fixed)],
        out_specs=[pl.BlockSpec((ROW_TILE, D_MODEL), row)] * 2,
        out_shape=[jax.ShapeDtypeStruct((t, D_MODEL), F32), jax.ShapeDtypeStruct((t, D_MODEL), BF16)],
        compiler_params=_params(("arbitrary",)),
        name="out_proj",
    )(attn, ssm_n, x2, attn_g.astype(F32)[None, :], w[:D_ATTN], w[D_ATTN:], ffn_g.astype(F32)[None, :])


def _take16(s, exact, index=None):
    if index is None:
        index = lax.broadcasted_iota(jnp.int32, s.shape, 0)
    vals = []
    for it in range(PEER_TOPK):
        m = jnp.max(s, axis=0, keepdims=True)
        if exact:
            first = jnp.min(jnp.where(s == m, index, jnp.iinfo(jnp.int32).max), axis=0, keepdims=True)
            hit = index == first
        else:
            hit = s == m
        s = jnp.where(hit, RANK_BASE * (1.0 + it / 32.0), s)
        vals.append(m)
    return vals, s


def _taken(marked):
    return jnp.logical_and(marked < 0.5 * RANK_BASE, marked > 2.0 * RANK_BASE)


def _rank_of(marked):
    rank = jnp.floor((marked * (1.0 / RANK_BASE) - 1.0) * 32.0 + 0.5)
    return jnp.where(_taken(marked), rank, NO_RANK)


def _stack_rows(rows, n):
    iota = lax.broadcasted_iota(jnp.int32, (n, rows[0].shape[1]), 0)
    out = jnp.zeros((n, rows[0].shape[1]), F32)
    for i, r in enumerate(rows):
        out = jnp.where(iota == i, r, out)
    return out


def _count(mask):
    return jnp.sum(jnp.where(mask, 1.0, 0.0), axis=0, keepdims=True)


def _route_head(s1, s2, exact):
    tb = s1.shape[1]
    v1, marked1 = _take16(s1, exact)
    v2, marked2 = _take16(s2, exact)
    rank1, rank2 = _rank_of(marked1), _rank_of(marked2)
    v2_all = _stack_rows(v2, PEER_TOPK)
    v1_hi = _stack_rows(v1[SUBLANES:], SUBLANES)
    sub = lax.broadcasted_iota(jnp.int32, (SUBLANES, tb), 0)
    pieces = [v1[0] + v2_all]
    flats = [lax.broadcasted_iota(jnp.int32, (PEER_TOPK, tb), 0)]
    for a in range(1, SUBLANES):
        limit = PEER_TOPK // (a + 1)
        pieces.append(jnp.where(sub < limit, v1[a] + v2_all[:SUBLANES], -jnp.inf))
        flats.append(a * PEER_TOPK + sub)
    pieces.append(v1_hi + v2[0])
    flats.append((sub + SUBLANES) * PEER_TOPK)
    cand = jnp.concatenate(pieces, axis=0)
    flat = jnp.concatenate(flats, axis=0)
    _, marked = _take16(cand, exact, flat)
    sel = jnp.where(_taken(marked), 1.0, 0.0)
    top = v1[0] + v2[0]
    z = jnp.sum(sel * jnp.exp(cand - top), axis=0, keepdims=True)
    cnt = [jnp.sum(sel[:PEER_TOPK], axis=0, keepdims=True)]
    for a in range(1, SUBLANES):
        lo = PEER_TOPK + SUBLANES * (a - 1)
        cnt.append(jnp.sum(sel[lo:lo + SUBLANES], axis=0, keepdims=True))
    lo = PEER_TOPK + SUBLANES * (SUBLANES - 1)
    for i in range(SUBLANES):
        cnt.append(sel[lo + i:lo + i + 1])
    c1 = jnp.zeros(rank1.shape, F32)
    for a in range(PEER_TOPK):
        c1 = jnp.where(rank1 == float(a), cnt[a], c1)
    e1 = jnp.exp(s1 - v1[0])
    e2n = jnp.exp(s2 - v2[0]) / z
    taken = jnp.maximum(jnp.maximum(_count(_taken(marked1)), _count(_taken(marked2))), jnp.sum(sel, axis=0, keepdims=True))
    return c1, e1, rank2, e2n, taken


def _route_one(h, qt_ref, k1_ref, k2_ref, c1_ref, e1_ref, r2_ref, e2_ref, exact):
    half = PEER_QDIM // 2
    q1 = qt_ref[PEER_QDIM * h:PEER_QDIM * h + half, :].astype(BF16)
    q2 = qt_ref[PEER_QDIM * h + half:PEER_QDIM * (h + 1), :].astype(BF16)
    s1 = jnp.dot(k1_ref[h], q1, preferred_element_type=F32)
    s2 = jnp.dot(k2_ref[h], q2, preferred_element_type=F32)
    c1, e1, r2, e2n, taken = _route_head(s1, s2, exact)
    c1_ref[h] = c1
    e1_ref[h] = e1
    r2_ref[h] = r2.astype(r2_ref.dtype)
    e2_ref[h] = e2n.astype(e2_ref.dtype)
    return taken


def _route_kernel(hn_ref, wq_ref, k1_ref, k2_ref, c1_ref, e1_ref, r2_ref, e2_ref, qt_ref):
    qt_ref[...] = lax.dot_general(wq_ref[...], hn_ref[...], (((1,), (1,)), ((), ())), preferred_element_type=F32)
    refs = (qt_ref, k1_ref, k2_ref, c1_ref, e1_ref, r2_ref, e2_ref)
    taken = [_route_one(h, *refs, exact=False) for h in range(PEER_HEADS)]
    for h in range(PEER_HEADS):
        @pl.when(jnp.max(taken[h]) > float(PEER_TOPK))
        def _():
            _route_one(h, *refs, exact=True)


def _peer_route(hn, w_q, keys1, keys2):
    t = hn.shape[0]
    blk = pl.BlockSpec((PEER_HEADS, PEER_KEYS, ROUTE_TILE), lambda i: (0, 0, i))
    fixed3 = lambda i: (0, 0, 0)
    shape = (PEER_HEADS, PEER_KEYS, t)
    return pl.pallas_call(
        _route_kernel,
        grid=(t // ROUTE_TILE,),
        in_specs=[pl.BlockSpec((ROUTE_TILE, D_MODEL), lambda i: (i, 0)),
                  pl.BlockSpec((PEER_HEADS * PEER_QDIM, D_MODEL), lambda i: (0, 0)),
                  pl.BlockSpec((PEER_HEADS, PEER_KEYS, PEER_QDIM // 2), fixed3),
                  pl.BlockSpec((PEER_HEADS, PEER_KEYS, PEER_QDIM // 2), fixed3)],
        out_specs=[blk] * 4,
        out_shape=[jax.ShapeDtypeStruct(shape, F32), jax.ShapeDtypeStruct(shape, F32),
                   jax.ShapeDtypeStruct(shape, BF16), jax.ShapeDtypeStruct(shape, BF16)],
        scratch_shapes=[pltpu.VMEM((PEER_HEADS * PEER_QDIM, ROUTE_TILE), F32)],
        compiler_params=_params(("arbitrary",)),
        name="peer_route",
    )(hn, w_q.T.astype(BF16), keys1.astype(BF16), keys2.astype(BF16))


def _peer_gated(c1_ref, e1_ref, r2_ref, e2_ref, at_ref, chunk, valid):
    packed = 2 * SUBLANES
    tiles = PEER_SUB // PEER_KEYS
    gs = []
    for tl in range(tiles):
        tile = chunk * tiles + tl
        gate = jnp.zeros((PEER_KEYS // packed, packed, PEER_TOKENS), BF16)
        for h in range(PEER_HEADS):
            c1 = jnp.broadcast_to(c1_ref[h, pl.ds(tile, 1), :], (packed, PEER_TOKENS)).astype(BF16)
            e1 = jnp.broadcast_to(e1_ref[h, pl.ds(tile, 1), :], (packed, PEER_TOKENS)).astype(BF16)
            r2 = r2_ref[h].reshape(gate.shape)
            e2 = e2_ref[h].reshape(gate.shape)
            gate = gate + jnp.where(r2 < c1[None], e1[None] * e2, jnp.zeros_like(e2))
        a = at_ref[PEER_KEYS * tl:PEER_KEYS * (tl + 1), :].astype(BF16)
        gs.append(gate.reshape(PEER_KEYS, PEER_TOKENS) * _erf_gelu(a))
    g = jnp.concatenate(gs, axis=0)
    return g if valid is True else jnp.where(valid, g, jnp.zeros_like(g))


def _peer_kernel(hn_ref, u_ref, *refs, n_chunks):
    n = PEER_EXPERTS_STEP // PEER_SUB
    vt_refs, vt_tail_ref, route = refs[:n], refs[n], refs[n + 1:n + 5]
    x1_ref, o_ref, acc_ref, at_ref = refs[n + 5:]
    e = pl.program_id(1)
    last = pl.num_programs(1) - 1
    nt = (((1,), (1,)), ((), ()))

    @pl.when(e == 0)
    def _():
        acc_ref[...] = jnp.zeros_like(acc_ref)
        at_ref[n - 1] = jnp.zeros((PEER_SUB, PEER_TOKENS), F32)

    hn = hn_ref[...]
    total = None
    for c in range(n):
        at_ref[c] = lax.dot_general(u_ref[PEER_SUB * c:PEER_SUB * (c + 1), :], hn, nt, preferred_element_type=F32)
        prev = n * e + c - 1
        g = _peer_gated(*route, at_ref.at[(c - 1) % n], jnp.maximum(prev, 0), (e > 0) if c == 0 else True)
        part = jnp.dot(vt_refs[c][...], g, preferred_element_type=F32)
        total = part if total is None else total + part
    acc_ref[...] += total

    @pl.when(e == last)
    def _():
        g = _peer_gated(*route, at_ref.at[n - 1], n_chunks - 1, True)
        tail = jnp.dot(vt_tail_ref[...], g, preferred_element_type=F32)
        o_ref[...] = x1_ref[...] + (acc_ref[...] + tail).T


def _peer_mix(hn, x1, u_tab, v_tab, c1, e1, r2, e2n):
    t = hn.shape[0]
    n = PEER_EXPERTS_STEP // PEER_SUB
    n_steps = u_tab.shape[0] // PEER_EXPERTS_STEP
    n_chunks = n * n_steps
    route = pl.BlockSpec((PEER_HEADS, PEER_KEYS, PEER_TOKENS), lambda i, e: (0, 0, i))
    tok = lambda i, e: (i, 0)
    vt = v_tab.T.astype(BF16)
    vt_specs = [pl.BlockSpec((D_MODEL, PEER_SUB),
                             functools.partial(lambda i, e, c: (0, jnp.maximum(n * e + c - 1, 0)), c=c))
                for c in range(n)]
    vt_specs.append(pl.BlockSpec((D_MODEL, PEER_SUB), lambda i, e: (0, n_chunks - 1)))
    return pl.pallas_call(
        functools.partial(_peer_kernel, n_chunks=n_chunks),
        grid=(t // PEER_TOKENS, n_steps),
        in_specs=[pl.BlockSpec((PEER_TOKENS, D_MODEL), tok),
                  pl.BlockSpec((PEER_EXPERTS_STEP, D_MODEL), lambda i, e: (e, 0)),
                  *vt_specs,
                  route, route, route, route,
                  pl.BlockSpec((PEER_TOKENS, D_MODEL), tok)],
        out_specs=pl.BlockSpec((PEER_TOKENS, D_MODEL), tok),
        out_shape=jax.ShapeDtypeStruct((t, D_MODEL), F32),
        scratch_shapes=[pltpu.VMEM((D_MODEL, PEER_TOKENS), F32),
                        pltpu.VMEM((n, PEER_SUB, PEER_TOKENS), F32)],
        compiler_params=_params(("arbitrary", "arbitrary")),
        name="peer_mix",
    )(hn, u_tab.astype(BF16), *([vt] * (n + 1)), c1, e1, r2, e2n, x1)


def kernel(x, norm_mix_g, w_in, q_norm_g, k_norm_g, rel_bias, ssm_lambda_re, ssm_lambda_im, ssm_log_dt,
           ssm_b_re, ssm_b_im, ssm_c_re, ssm_c_im, ssm_d, ssm_glu_w, ssm_glu_b, attn_out_g, ssm_out_g,
           w_out, norm_ffn_g, peer_w_q, peer_keys1, peer_keys2, peer_u, peer_v):
    b, s, d = x.shape
    x2 = x.reshape(b * s, d)
    q, k, v, u = _in_proj(x2, norm_mix_g, w_in, q_norm_g, k_norm_g)
    bias = _bias_tables(rel_bias)
    attn = _attention(q.reshape(b, s, D_ATTN), k.reshape(b, s, D_ATTN), v.reshape(b, s, D_ATTN), bias)
    ssm_n = _s5_mixer(u.reshape(b, s, D_SSM), ssm_lambda_re, ssm_lambda_im, ssm_log_dt, ssm_b_re, ssm_b_im,
                      ssm_c_re, ssm_c_im, ssm_d, ssm_glu_w, ssm_glu_b, ssm_out_g)
    x1, hn = _out_proj(attn.reshape(b * s, D_ATTN), ssm_n.reshape(b * s, D_SSM), x2, attn_out_g, w_out, norm_ffn_g)
    c1, e1, r2, e2n = _peer_route(hn, peer_w_q, peer_keys1, peer_keys2)
    out = _peer_mix(hn, x1, peer_u, peer_v, c1, e1, r2, e2n)
    return out.reshape(b, s, d).astype(x.dtype)
```

```python
import functools
import math

import jax
import jax.numpy as jnp
import numpy as np
from jax import lax
from jax.experimental import pallas as pl
from jax.experimental.pallas import tpu as pltpu

F32 = jnp.float32
BF16 = jnp.bfloat16

D_MODEL = 2048
HEAD_DIM = 64
N_ATTN_HEADS = 16
D_ATTN = N_ATTN_HEADS * HEAD_DIM
SSM_GROUP = 16
N_SSM_GROUPS = 64
D_SSM = N_SSM_GROUPS * SSM_GROUP
SSM_STATE = 64
D_IN_PROJ = 3 * D_ATTN + D_SSM
DILATED_BRANCHES = ((128, 1), (512, 4), (2048, 16))
BLK = 128
N_BUCKETS = 32
MAX_DISTANCE = 2048
PEER_HEADS = 8
PEER_KEYS = 128
PEER_QDIM = 256
PEER_TOPK = 16
EPS = 1e-6
NEG = -1e30
LOG2E = math.log2(math.e)

LANES = 128
SUBLANES = 8
VMEM_LIMIT = 56 * 1024 * 1024

ROW_TILE = 256
SSM_CHUNK = 128
SSM_SLAB_GROUPS = LANES // SSM_GROUP
N_SLABS = N_SSM_GROUPS // SSM_SLAB_GROUPS
SLAB_STATE = SSM_SLAB_GROUPS * SSM_STATE
ATTN_LOOKAHEAD = 2
ROUTE_TILE = 256
PEER_TOKENS = 512
PEER_EXPERTS_STEP = 512
PEER_SUB = 256
NO_RANK = 99.0
RANK_BASE = -2.0 ** 100


def _params(sem, vmem=VMEM_LIMIT):
    return pltpu.CompilerParams(dimension_semantics=sem, vmem_limit_bytes=vmem)


def _erf_gelu(x):
    return 0.5 * x * (1.0 + lax.erf(x * math.sqrt(0.5)))


def _t5_bucket(dist):
    max_exact = N_BUCKETS // 2
    n = np.maximum(dist, 0)
    nf = np.maximum(n, 1).astype(np.float32)
    large = max_exact + (np.log(nf / np.float32(max_exact)) / np.float32(math.log(MAX_DISTANCE / max_exact))
                         * np.float32(N_BUCKETS - max_exact)).astype(np.int32)
    large = np.minimum(large, N_BUCKETS - 1)
    return np.where(n < max_exact, n, large)


def _bias_kernel(bkt_ref, rb_ref, out_ref):
    bkt = bkt_ref[0]
    qi = lax.broadcasted_iota(jnp.int32, (BLK, 2 * BLK), 0)
    kj = lax.broadcasted_iota(jnp.int32, (BLK, 2 * BLK), 1)
    valid = jnp.where(kj < BLK, kj - qi, qi - (kj - BLK)) >= 0
    for h in range(N_ATTN_HEADS):
        acc = jnp.zeros((BLK, 2 * BLK), F32)
        for b in range(N_BUCKETS):
            acc = jnp.where(bkt == b, rb_ref[b * N_ATTN_HEADS + h], acc)
        out_ref[0, h] = jnp.where(valid, acc * LOG2E, NEG)


def _bias_tables(rel_bias):
    qi = np.arange(BLK)[:, None]
    kj = np.arange(2 * BLK)[None, :]
    rel = qi - kj + BLK
    buckets = jnp.asarray(np.stack([_t5_bucket(rel * dil) for _, dil in DILATED_BRANCHES]).astype(np.int32))
    nbr = len(DILATED_BRANCHES)
    return pl.pallas_call(
        _bias_kernel,
        grid=(nbr,),
        in_specs=[pl.BlockSpec((1, BLK, 2 * BLK), lambda i: (i, 0, 0)),
                  pl.BlockSpec(memory_space=pltpu.SMEM)],
        out_specs=pl.BlockSpec((1, N_ATTN_HEADS, BLK, 2 * BLK), lambda i: (i, 0, 0, 0)),
        out_shape=jax.ShapeDtypeStruct((nbr, N_ATTN_HEADS, BLK, 2 * BLK), F32),
        compiler_params=_params(("arbitrary",)),
        name="bias_table",
    )(buckets, rel_bias.astype(F32).reshape(N_BUCKETS * N_ATTN_HEADS))


def _head_rmsnorm(z, gain, ones, scale):
    outs = []
    for c in range(z.shape[1] // LANES):
        zc = z[:, LANES * c:LANES * (c + 1)]
        sq = zc * zc
        hi = sq.astype(BF16)
        lo = (sq - hi.astype(F32)).astype(BF16)
        msq = (jnp.dot(hi, ones, preferred_element_type=F32)
               + jnp.dot(lo, ones, preferred_element_type=F32))
        y = zc * lax.rsqrt(msq + EPS)
        outs.append(y * gain[:, LANES * c:LANES * (c + 1)] * scale)
    return jnp.concatenate(outs, axis=1)


def _inproj_kernel(x_ref, g_ref, w_ref, qg_ref, kg_ref, ones_ref, tab_ref, q_ref, k_ref, v_ref, u_ref, tabo_ref):
    x = x_ref[...]
    ms = jnp.mean(x * x, axis=-1, keepdims=True)
    h = (x * lax.rsqrt(ms + EPS) * g_ref[...]).astype(BF16)
    proj = jnp.dot(h, w_ref[...], preferred_element_type=F32)
    ones = ones_ref[...]
    q_ref[...] = _head_rmsnorm(proj[:, :D_ATTN], qg_ref[...], ones, LOG2E / math.sqrt(HEAD_DIM))
    k_ref[...] = _head_rmsnorm(proj[:, D_ATTN:2 * D_ATTN], kg_ref[...], ones, 1.0)
    v_ref[...] = proj[:, 2 * D_ATTN:3 * D_ATTN]
    u_ref[...] = proj[:, 3 * D_ATTN:]
    tabo_ref[...] = tab_ref[...].astype(BF16)


def _in_proj(x2, norm_g, w_in, q_g, k_g, table):
    t = x2.shape[0]
    steps = t // ROW_TILE
    tab_rows = table.shape[0] // steps
    assert tab_rows * steps == table.shape[0]
    head_of_lane = jnp.arange(LANES) // HEAD_DIM
    ones = jnp.where(head_of_lane[:, None] == head_of_lane[None, :], 1.0 / HEAD_DIM, 0.0).astype(BF16)
    qg = jnp.tile(q_g.astype(F32), N_ATTN_HEADS)[None, :]
    kg = jnp.tile(k_g.astype(F32), N_ATTN_HEADS)[None, :]
    row = lambda i: (i, 0)
    fixed = lambda i: (0, 0)
    outs = pl.pallas_call(
        _inproj_kernel,
        grid=(t // ROW_TILE,),
        in_specs=[pl.BlockSpec((ROW_TILE, D_MODEL), row),
                  pl.BlockSpec((1, D_MODEL), fixed),
                  pl.BlockSpec((D_MODEL, D_IN_PROJ), fixed, pipeline_mode=pl.Buffered(1)),
                  pl.BlockSpec((1, D_ATTN), fixed),
                  pl.BlockSpec((1, D_ATTN), fixed),
                  pl.BlockSpec((LANES, LANES), fixed),
                  pl.BlockSpec((tab_rows, D_MODEL), row)],
        out_specs=[pl.BlockSpec((ROW_TILE, D_ATTN), row)] * 3 + [pl.BlockSpec((ROW_TILE, D_SSM), row),
                                                                   pl.BlockSpec((tab_rows, D_MODEL), row)],
        out_shape=[jax.ShapeDtypeStruct((t, D_ATTN), F32)] * 3 + [jax.ShapeDtypeStruct((t, D_SSM), F32),
                                                                   jax.ShapeDtypeStruct(table.shape, BF16)],
        compiler_params=_params(("arbitrary",)),
        name="in_proj",
    )(x2, norm_g.astype(F32)[None, :], w_in.astype(BF16), qg, kg, ones, table)
    return outs


def _rows(start, size, stride):
    return pl.ds(start, size, stride=stride) if stride > 1 else pl.ds(start, size)


def _attn_scores(q_ref, k_ref, v_ref, bias_ref, blk, head0):
    br, dil, n, r = blk
    qrows = _rows(r + dil * BLK * n, BLK, dil)
    qb = q_ref[qrows, :].astype(BF16)
    krows = qrows if n == 0 else _rows(r + dil * BLK * (n - 1), 2 * BLK, dil)
    kb = k_ref[krows, :].astype(BF16)
    vb = v_ref[krows, :].astype(BF16)
    scores = []
    for h in range(2):
        mine = head0 if h == 0 else jnp.logical_not(head0)
        qh = jnp.where(mine, qb, jnp.zeros_like(qb))
        s = lax.dot_general(qh, kb, (((1,), (1,)), ((), ())), preferred_element_type=F32)
        scores.append(s + (bias_ref[br, h, :, BLK:] if n == 0 else bias_ref[br, h]))
    return scores, vb, qrows


def _attn_values(scores, vb, qrows, br, pv_ref, den_ref, m_ref, head0):
    ones = jnp.ones_like(vb)
    pv, mx = [], []
    for h in range(2):
        mine = head0 if h == 0 else jnp.logical_not(head0)
        m = jnp.max(scores[h], axis=-1, keepdims=True)
        p = jnp.exp2(scores[h] - m).astype(BF16)
        pv.append(jnp.dot(p, jnp.where(mine, vb, ones), preferred_element_type=F32))
        mx.append(m)
    pv_ref[br, qrows, :] = jnp.where(head0, pv[0], pv[1])
    den_ref[br, qrows, :] = jnp.where(head0, pv[1], pv[0])
    m_ref[br, qrows, :] = jnp.where(head0, mx[0], mx[1])


def _attn_kernel(q_ref, k_ref, v_ref, bias_ref, o_ref, pv_ref, den_ref, m_ref, *, seq):
    head0 = lax.broadcasted_iota(jnp.int32, (1, LANES), 1) < HEAD_DIM
    nbr = len(DILATED_BRANCHES)
    blocks = []
    for br, (window, dil) in enumerate(DILATED_BRANCHES):
        assert window // dil == BLK
        blocks += [(br, dil, n, r) for r in range(dil) for n in range(seq // dil // BLK)]
    ahead = [_attn_scores(q_ref, k_ref, v_ref, bias_ref, b, head0) for b in blocks[:ATTN_LOOKAHEAD]]
    for i, blk in enumerate(blocks):
        if i + ATTN_LOOKAHEAD < len(blocks):
            ahead.append(_attn_scores(q_ref, k_ref, v_ref, bias_ref, blocks[i + ATTN_LOOKAHEAD], head0))
        _attn_values(*ahead.pop(0), blk[0], pv_ref, den_ref, m_ref, head0)
    m_all = [m_ref[br] for br in range(nbr)]
    m_top = functools.reduce(jnp.maximum, m_all)
    num = jnp.zeros((seq, LANES), F32)
    den = jnp.zeros((seq, LANES), F32)
    for br in range(nbr):
        w = jnp.exp2(m_all[br] - m_top)
        num = num + w * pv_ref[br]
        den = den + w * pltpu.roll(den_ref[br], HEAD_DIM, axis=1)
    o_ref[...] = num / den


def _attention(q, k, v, bias):
    b, s, _ = q.shape
    blk = pl.BlockSpec((None, s, LANES), lambda i, p: (i, 0, p))
    nbr = len(DILATED_BRANCHES)
    return pl.pallas_call(
        functools.partial(_attn_kernel, seq=s),
        grid=(b, D_ATTN // LANES),
        in_specs=[blk, blk, blk,
                  pl.BlockSpec((nbr, 2, BLK, 2 * BLK), lambda i, p: (0, p, 0, 0))],
        out_specs=blk,
        out_shape=jax.ShapeDtypeStruct((b, s, D_ATTN), F32),
        scratch_shapes=[pltpu.VMEM((nbr, s, LANES), F32)] * 3,
        compiler_params=_params(("arbitrary", "arbitrary")),
        name="attention",
    )(q, k, v, bias)


def _zoh_kernel(lr_ref, li_ref, dt_ref, lrr_ref, lir_ref, br_ref, bi_ref,
                are_ref, aim_ref, bbr_ref, bbi_ref):
    dt = jnp.exp(dt_ref[...])

    def zoh(lr, li):
        mag = jnp.exp(lr * dt)
        a_re, a_im = mag * jnp.cos(li * dt), mag * jnp.sin(li * dt)
        den = lr * lr + li * li
        f_re = ((a_re - 1.0) * lr + a_im * li) / den
        f_im = (a_im * lr - (a_re - 1.0) * li) / den
        return a_re, a_im, f_re, f_im

    a_re, a_im, _, _ = zoh(lr_ref[...], li_ref[...])
    are_ref[...] = a_re
    aim_ref[...] = a_im
    _, _, f_re, f_im = zoh(lrr_ref[...], lir_ref[...])
    br, bi = br_ref[...], bi_ref[...]
    bbr_ref[...] = f_re * br - f_im * bi
    bbi_ref[...] = f_re * bi + f_im * br


def _ssm_zoh(lam_re, lam_im, log_dt, b_re, b_im):
    g, n, c = b_re.shape
    rep = lambda a: jnp.repeat(a.astype(F32), c, axis=1)
    a_re, a_im, bb_re, bb_im = pl.pallas_call(
        _zoh_kernel,
        out_shape=[jax.ShapeDtypeStruct((g, n), F32)] * 2 + [jax.ShapeDtypeStruct((g, n * c), F32)] * 2,
        name="ssm_zoh",
    )(lam_re.astype(F32), lam_im.astype(F32), log_dt.astype(F32)[:, None], rep(lam_re), rep(lam_im),
      b_re.astype(F32).reshape(g, n * c), b_im.astype(F32).reshape(g, n * c))
    return a_re, a_im, bb_re.reshape(g, n, c), bb_im.reshape(g, n, c)


def _ssm_kernel(u_ref, wb_ref, wc_ref, are_ref, aim_ref, d_ref, gw_ref, gb_ref, gain_ref,
                o_ref, lhs_ref, bu_ref, ysel_ref, y_ref, sr_ref, si_ref, *, nb, chunk):
    half = N_SLABS // 2
    seqs = 2 * nb
    lane_blocks = 2 * SLAB_STATE // LANES
    rows_all = seqs * chunk

    @pl.when(pl.program_id(0) == 0)
    def _():
        sr_ref[...] = jnp.zeros_like(sr_ref)
        si_ref[...] = jnp.zeros_like(si_ref)
        lhs_ref[...] = jnp.zeros_like(lhs_ref)

    for b in range(nb):
        for m in range(N_SLABS):
            gh, mp = divmod(m, half)
            lhs_ref[2 * mp + gh, pl.ds(gh * nb + b, chunk, stride=seqs), :] = u_ref[b, :, LANES * m:LANES * (m + 1)]

    for mp in range(half):
        lhs = jnp.concatenate([lhs_ref[2 * mp], lhs_ref[2 * mp + 1]], axis=1).astype(BF16)
        bu = jnp.dot(lhs, wb_ref[mp], preferred_element_type=F32)
        for j in range(lane_blocks):
            bu_ref[lane_blocks * mp + j] = bu[:, LANES * j:LANES * (j + 1)]

    def load_state(rows, mp, part):
        j0 = lane_blocks * mp + part * (lane_blocks // 2)
        return jnp.concatenate([bu_ref[j0 + j, rows, :] for j in range(lane_blocks // 2)], axis=1)

    def store_state(rows, mp, part, val):
        j0 = lane_blocks * mp + part * (lane_blocks // 2)
        for j in range(lane_blocks // 2):
            bu_ref[j0 + j, rows, :] = val[:, LANES * j:LANES * (j + 1)]

    def step(t, carry):
        base = pl.multiple_of(t * seqs, seqs)
        rows = pl.ds(base, seqs)
        new = []
        for mp in range(half):
            xr, xi = carry[2 * mp], carry[2 * mp + 1]
            ar = are_ref[:, SLAB_STATE * mp:SLAB_STATE * (mp + 1)]
            ai = aim_ref[:, SLAB_STATE * mp:SLAB_STATE * (mp + 1)]
            nr = ar * xr - ai * xi + load_state(rows, mp, 0)
            ni = ar * xi + ai * xr + load_state(rows, mp, 1)
            store_state(rows, mp, 0, nr)
            store_state(rows, mp, 1, ni)
            new += [nr, ni]
        return tuple(new)

    init = []
    for mp in range(half):
        init += [sr_ref[:, SLAB_STATE * mp:SLAB_STATE * (mp + 1)], si_ref[:, SLAB_STATE * mp:SLAB_STATE * (mp + 1)]]
    final = lax.fori_loop(0, chunk, step, tuple(init), unroll=2)
    for mp in range(half):
        sr_ref[:, SLAB_STATE * mp:SLAB_STATE * (mp + 1)] = final[2 * mp]
        si_ref[:, SLAB_STATE * mp:SLAB_STATE * (mp + 1)] = final[2 * mp + 1]

    first_half = (lax.broadcasted_iota(jnp.int32, (rows_all, LANES), 0) & nb) == 0
    for mp in range(half):
        xs = jnp.concatenate([bu_ref[lane_blocks * mp + j] for j in range(lane_blocks)], axis=1).astype(BF16)
        yy = jnp.dot(xs, wc_ref[mp], preferred_element_type=F32)
        ysel_ref[mp] = jnp.where(first_half, yy[:, :LANES], yy[:, LANES:])
    for b in range(nb):
        for m in range(N_SLABS):
            gh, mp = divmod(m, half)
            cols = slice(LANES * m, LANES * (m + 1))
            y = ysel_ref[mp, pl.ds(gh * nb + b, chunk, stride=seqs), :]
            y_ref[b * chunk:(b + 1) * chunk, cols] = y + d_ref[:, cols] * u_ref[b, :, cols]

    y = _erf_gelu(y_ref[...])
    z = jnp.dot(y.astype(BF16), gw_ref[...], preferred_element_type=F32) + gb_ref[...]
    y = y * jax.nn.sigmoid(z)
    ms = jnp.mean(y * y, axis=-1, keepdims=True)
    yn = y * lax.rsqrt(ms + EPS) * gain_ref[...]
    for b in range(nb):
        o_ref[b] = yn[b * chunk:(b + 1) * chunk].astype(o_ref.dtype)


def _s5_mixer(u, lam_re, lam_im, log_dt, b_re, b_im, c_re, c_im, d_skip, glu_w, glu_b, out_gain):
    nb, s, _ = u.shape
    a_re, a_im, bb_re, bb_im = _ssm_zoh(lam_re, lam_im, log_dt, b_re, b_im)
    eye = jnp.eye(SSM_SLAB_GROUPS, dtype=F32)

    def in_slab(bb):
        w = jnp.einsum('mgnc,gh->mgchn', bb.reshape(N_SLABS, SSM_SLAB_GROUPS, SSM_STATE, SSM_GROUP), eye)
        return w.reshape(N_SLABS, LANES, SLAB_STATE)

    def out_slab(cc):
        w = jnp.einsum('mgcn,gh->mgnhc', cc.reshape(N_SLABS, SSM_SLAB_GROUPS, SSM_GROUP, SSM_STATE), eye)
        return w.reshape(N_SLABS, SLAB_STATE, LANES)

    wb = jnp.concatenate([in_slab(bb_re), in_slab(bb_im)], axis=2)
    wc = jnp.concatenate([out_slab(c_re.astype(F32)), -out_slab(c_im.astype(F32))], axis=1)
    hs = N_SLABS // 2
    wb = jnp.concatenate([wb[:hs], wb[hs:]], axis=1).astype(BF16)
    wc = jnp.concatenate([wc[:hs], wc[hs:]], axis=2).astype(BF16)
    half_states = (N_SSM_GROUPS // 2) * SSM_STATE

    def seq_rows(a):
        return jnp.repeat(a.reshape(2, half_states), nb, axis=0)

    fixed2 = lambda c: (0, 0)
    fixed3 = lambda c: (0, 0, 0)
    chunk = SSM_CHUNK
    return pl.pallas_call(
        functools.partial(_ssm_kernel, nb=nb, chunk=chunk),
        grid=(s // chunk,),
        in_specs=[pl.BlockSpec((nb, chunk, D_SSM), lambda c: (0, c, 0)),
                  pl.BlockSpec((N_SLABS // 2, 2 * LANES, 2 * SLAB_STATE), fixed3),
                  pl.BlockSpec((N_SLABS // 2, 2 * SLAB_STATE, 2 * LANES), fixed3),
                  pl.BlockSpec((2 * nb, half_states), fixed2),
                  pl.BlockSpec((2 * nb, half_states), fixed2),
                  pl.BlockSpec((1, D_SSM), fixed2),
                  pl.BlockSpec((D_SSM, D_SSM), fixed2),
                  pl.BlockSpec((1, D_SSM), fixed2),
                  pl.BlockSpec((1, D_SSM), fixed2)],
        out_specs=pl.BlockSpec((nb, chunk, D_SSM), lambda c: (0, c, 0)),
        out_shape=jax.ShapeDtypeStruct((nb, s, D_SSM), BF16),
        scratch_shapes=[pltpu.VMEM((N_SLABS, 2 * nb * chunk, LANES), F32),
                        pltpu.VMEM((2 * half_states // LANES, 2 * nb * chunk, LANES), F32),
                        pltpu.VMEM((N_SLABS // 2, 2 * nb * chunk, LANES), F32),
                        pltpu.VMEM((nb * chunk, D_SSM), F32),
                        pltpu.VMEM((2 * nb, half_states), F32),
                        pltpu.VMEM((2 * nb, half_states), F32)],
        compiler_params=_params(("arbitrary",)),
        name="ssm",
    )(u, wb, wc, seq_rows(a_re), seq_rows(a_im), d_skip.astype(F32).reshape(1, D_SSM),
      glu_w.astype(BF16), glu_b.astype(F32)[None, :], out_gain.astype(F32)[None, :])


def _outproj_kernel(a_ref, s_ref, x_ref, ag_ref, wa_ref, ws_ref, fg_ref, tab_ref, x1_ref, hn_ref, tabt_ref):
    a = a_ref[...]
    ms = jnp.mean(a * a, axis=-1, keepdims=True)
    an = (a * lax.rsqrt(ms + EPS) * ag_ref[...]).astype(BF16)
    mixed = (jnp.dot(an, wa_ref[...], preferred_element_type=F32)
             + jnp.dot(s_ref[...], ws_ref[...], preferred_element_type=F32))
    x1 = x_ref[...] + mixed
    x1_ref[...] = x1
    ms1 = jnp.mean(x1 * x1, axis=-1, keepdims=True)
    hn_ref[...] = (x1 * lax.rsqrt(ms1 + EPS) * fg_ref[...]).astype(BF16)
    tabt_ref[...] = tab_ref[...].T.astype(BF16)


def _out_proj(attn, ssm_n, x2, attn_g, w_out, ffn_g, table):
    t = x2.shape[0]
    steps = t // ROW_TILE
    tab_rows = table.shape[0] // steps
    assert tab_rows * steps == table.shape[0]
    row = lambda i: (i, 0)
    fixed = lambda i: (0, 0)
    w = w_out.astype(BF16)
    return pl.pallas_call(
        _outproj_kernel,
        grid=(t // ROW_TILE,),
        in_specs=[pl.BlockSpec((ROW_TILE, D_ATTN), row),
                  pl.BlockSpec((ROW_TILE, D_SSM), row),
                  pl.BlockSpec((ROW_TILE, D_MODEL), row),
                  pl.BlockSpec((1, D_ATTN), fixed),
                  pl.BlockSpec((D_ATTN, D_MODEL), fixed),
                  pl.BlockSpec((D_SSM, D_MODEL), fixed),
                  pl.BlockSpec((1, D_MODEL), fixed),
                  pl.BlockSpec((tab_rows, D_MODEL), row)],
        out_specs=[pl.BlockSpec((ROW_TILE, D_MODEL), row)] * 2 + [pl.BlockSpec((D_MODEL, tab_rows), lambda i: (0, i))],
        out_shape=[jax.ShapeDtypeStruct((t, D_MODEL), F32), jax.ShapeDtypeStruct((t, D_MODEL), BF16),
                   jax.ShapeDtypeStruct(table.shape[::-1], BF16)],
        compiler_params=_params(("arbitrary",)),
        name="out_proj",
    )(attn, ssm_n, x2, attn_g.astype(F32)[None, :], w[:D_ATTN], w[D_ATTN:], ffn_g.astype(F32)[None, :], table)


def _take16(s, exact, index=None):
    if index is None:
        index = lax.broadcasted_iota(jnp.int32, s.shape, 0)
    vals = []
    for it in range(PEER_TOPK):
        m = jnp.max(s, axis=0, keepdims=True)
        if exact:
            first = jnp.min(jnp.where(s == m, index, jnp.iinfo(jnp.int32).max), axis=0, keepdims=True)
            hit = index == first
        else:
            hit = s == m
        s = jnp.where(hit, RANK_BASE * (1.0 + it / 32.0), s)
        vals.append(m)
    return vals, s


def _taken(marked):
    return jnp.logical_and(marked < 0.5 * RANK_BASE, marked > 2.0 * RANK_BASE)


def _rank_of(marked):
    rank = jnp.floor((marked * (1.0 / RANK_BASE) - 1.0) * 32.0 + 0.5)
    return jnp.where(_taken(marked), rank, NO_RANK)


def _stack_rows(rows, n):
    iota = lax.broadcasted_iota(jnp.int32, (n, rows[0].shape[1]), 0)
    out = jnp.zeros((n, rows[0].shape[1]), F32)
    for i, r in enumerate(rows):
        out = jnp.where(iota == i, r, out)
    return out


def _count(mask):
    return jnp.sum(jnp.where(mask, 1.0, 0.0), axis=0, keepdims=True)


def _route_head(s1, s2, exact):
    tb = s1.shape[1]
    v1, marked1 = _take16(s1, exact)
    v2, marked2 = _take16(s2, exact)
    rank1, rank2 = _rank_of(marked1), _rank_of(marked2)
    v2_all = _stack_rows(v2, PEER_TOPK)
    v1_hi = _stack_rows(v1[SUBLANES:], SUBLANES)
    sub = lax.broadcasted_iota(jnp.int32, (SUBLANES, tb), 0)
    pieces = [v1[0] + v2_all]
    flats = [lax.broadcasted_iota(jnp.int32, (PEER_TOPK, tb), 0)]
    for a in range(1, SUBLANES):
        limit = PEER_TOPK // (a + 1)
        pieces.append(jnp.where(sub < limit, v1[a] + v2_all[:SUBLANES], -jnp.inf))
        flats.append(a * PEER_TOPK + sub)
    pieces.append(v1_hi + v2[0])
    flats.append((sub + SUBLANES) * PEER_TOPK)
    cand = jnp.concatenate(pieces, axis=0)
    flat = jnp.concatenate(flats, axis=0)
    _, marked = _take16(cand, exact, flat)
    sel = jnp.where(_taken(marked), 1.0, 0.0)
    top = v1[0] + v2[0]
    z = jnp.sum(sel * jnp.exp(cand - top), axis=0, keepdims=True)
    cnt = [jnp.sum(sel[:PEER_TOPK], axis=0, keepdims=True)]
    for a in range(1, SUBLANES):
        lo = PEER_TOPK + SUBLANES * (a - 1)
        cnt.append(jnp.sum(sel[lo:lo + SUBLANES], axis=0, keepdims=True))
    lo = PEER_TOPK + SUBLANES * (SUBLANES - 1)
    for i in range(SUBLANES):
        cnt.append(sel[lo + i:lo + i + 1])
    c1 = jnp.zeros(rank1.shape, F32)
    for a in range(PEER_TOPK):
        c1 = jnp.where(rank1 == float(a), cnt[a], c1)
    e1 = jnp.exp(s1 - v1[0])
    e2n = jnp.exp(s2 - v2[0]) / z
    taken = jnp.maximum(jnp.maximum(_count(_taken(marked1)), _count(_taken(marked2))), jnp.sum(sel, axis=0, keepdims=True))
    return c1, e1, rank2, e2n, taken


def _route_one(h, qt_ref, k1_ref, k2_ref, c1_ref, e1_ref, r2_ref, e2_ref, exact):
    half = PEER_QDIM // 2
    q1 = qt_ref[PEER_QDIM * h:PEER_QDIM * h + half, :].astype(BF16)
    q2 = qt_ref[PEER_QDIM * h + half:PEER_QDIM * (h + 1), :].astype(BF16)
    s1 = jnp.dot(k1_ref[h], q1, preferred_element_type=F32)
    s2 = jnp.dot(k2_ref[h], q2, preferred_element_type=F32)
    c1, e1, r2, e2n, taken = _route_head(s1, s2, exact)
    c1_ref[h] = c1
    e1_ref[h] = e1
    r2_ref[h] = r2.astype(r2_ref.dtype)
    e2_ref[h] = e2n.astype(e2_ref.dtype)
    return taken


def _route_kernel(hn_ref, wq_ref, k1_ref, k2_ref, c1_ref, e1_ref, r2_ref, e2_ref, qt_ref):
    qt_ref[...] = lax.dot_general(wq_ref[...], hn_ref[...], (((1,), (1,)), ((), ())), preferred_element_type=F32)
    refs = (qt_ref, k1_ref, k2_ref, c1_ref, e1_ref, r2_ref, e2_ref)
    taken = [_route_one(h, *refs, exact=False) for h in range(PEER_HEADS)]
    for h in range(PEER_HEADS):
        @pl.when(jnp.max(taken[h]) > float(PEER_TOPK))
        def _():
            _route_one(h, *refs, exact=True)


def _peer_route(hn, w_q, keys1, keys2):
    t = hn.shape[0]
    blk = pl.BlockSpec((PEER_HEADS, PEER_KEYS, ROUTE_TILE), lambda i: (0, 0, i))
    fixed3 = lambda i: (0, 0, 0)
    shape = (PEER_HEADS, PEER_KEYS, t)
    return pl.pallas_call(
        _route_kernel,
        grid=(t // ROUTE_TILE,),
        in_specs=[pl.BlockSpec((ROUTE_TILE, D_MODEL), lambda i: (i, 0)),
                  pl.BlockSpec((PEER_HEADS * PEER_QDIM, D_MODEL), lambda i: (0, 0)),
                  pl.BlockSpec((PEER_HEADS, PEER_KEYS, PEER_QDIM // 2), fixed3),
                  pl.BlockSpec((PEER_HEADS, PEER_KEYS, PEER_QDIM // 2), fixed3)],
        out_specs=[blk] * 4,
        out_shape=[jax.ShapeDtypeStruct(shape, F32), jax.ShapeDtypeStruct(shape, F32),
                   jax.ShapeDtypeStruct(shape, BF16), jax.ShapeDtypeStruct(shape, BF16)],
        scratch_shapes=[pltpu.VMEM((PEER_HEADS * PEER_QDIM, ROUTE_TILE), F32)],
        compiler_params=_params(("arbitrary",)),
        name="peer_route",
    )(hn, w_q.T.astype(BF16), keys1.astype(BF16), keys2.astype(BF16))


def _peer_gated(c1_ref, e1_ref, r2_ref, e2_ref, at_ref, chunk, valid):
    packed = 2 * SUBLANES
    tiles = PEER_SUB // PEER_KEYS
    gs = []
    for tl in range(tiles):
        tile = chunk * tiles + tl
        gate = jnp.zeros((PEER_KEYS // packed, packed, PEER_TOKENS), BF16)
        for h in range(PEER_HEADS):
            c1 = jnp.broadcast_to(c1_ref[h, pl.ds(tile, 1), :], (packed, PEER_TOKENS)).astype(BF16)
            e1 = jnp.broadcast_to(e1_ref[h, pl.ds(tile, 1), :], (packed, PEER_TOKENS)).astype(BF16)
            r2 = r2_ref[h].reshape(gate.shape)
            e2 = e2_ref[h].reshape(gate.shape)
            gate = gate + jnp.where(r2 < c1[None], e1[None] * e2, jnp.zeros_like(e2))
        a = at_ref[PEER_KEYS * tl:PEER_KEYS * (tl + 1), :].astype(BF16)
        gs.append(gate.reshape(PEER_KEYS, PEER_TOKENS) * _erf_gelu(a))
    g = jnp.concatenate(gs, axis=0)
    return g if valid is True else jnp.where(valid, g, jnp.zeros_like(g))


def _peer_kernel(hn_ref, u_ref, *refs, n_chunks):
    n = PEER_EXPERTS_STEP // PEER_SUB
    vt_refs, vt_tail_ref, route = refs[:n], refs[n], refs[n + 1:n + 5]
    x1_ref, o_ref, acc_ref, at_ref = refs[n + 5:]
    e = pl.program_id(1)
    last = pl.num_programs(1) - 1
    nt = (((1,), (1,)), ((), ()))

    @pl.when(e == 0)
    def _():
        acc_ref[...] = jnp.zeros_like(acc_ref)
        at_ref[n - 1] = jnp.zeros((PEER_SUB, PEER_TOKENS), F32)

    hn = hn_ref[...]
    total = None
    for c in range(n):
        at_ref[c] = lax.dot_general(u_ref[PEER_SUB * c:PEER_SUB * (c + 1), :], hn, nt, preferred_element_type=F32)
        prev = n * e + c - 1
        g = _peer_gated(*route, at_ref.at[(c - 1) % n], jnp.maximum(prev, 0), (e > 0) if c == 0 else True)
        part = jnp.dot(vt_refs[c][...], g, preferred_element_type=F32)
        total = part if total is None else total + part
    acc_ref[...] += total

    @pl.when(e == last)
    def _():
        g = _peer_gated(*route, at_ref.at[n - 1], n_chunks - 1, True)
        tail = jnp.dot(vt_tail_ref[...], g, preferred_element_type=F32)
        o_ref[...] = x1_ref[...] + (acc_ref[...] + tail).T


def _peer_mix(hn, x1, u_bf, vt, c1, e1, r2, e2n):
    t = hn.shape[0]
    n = PEER_EXPERTS_STEP // PEER_SUB
    n_steps = u_bf.shape[0] // PEER_EXPERTS_STEP
    n_chunks = n * n_steps
    route = pl.BlockSpec((PEER_HEADS, PEER_KEYS, PEER_TOKENS), lambda i, e: (0, 0, i))
    tok = lambda i, e: (i, 0)
    vt_specs = [pl.BlockSpec((D_MODEL, PEER_SUB),
                             functools.partial(lambda i, e, c: (0, jnp.maximum(n * e + c - 1, 0)), c=c))
                for c in range(n)]
    vt_specs.append(pl.BlockSpec((D_MODEL, PEER_SUB), lambda i, e: (0, n_chunks - 1)))
    return pl.pallas_call(
        functools.partial(_peer_kernel, n_chunks=n_chunks),
        grid=(t // PEER_TOKENS, n_steps),
        in_specs=[pl.BlockSpec((PEER_TOKENS, D_MODEL), tok),
                  pl.BlockSpec((PEER_EXPERTS_STEP, D_MODEL), lambda i, e: (e, 0)),
                  *vt_specs,
                  route, route, route, route,
                  pl.BlockSpec((PEER_TOKENS, D_MODEL), tok)],
        out_specs=pl.BlockSpec((PEER_TOKENS, D_MODEL), tok),
        out_shape=jax.ShapeDtypeStruct((t, D_MODEL), F32),
        scratch_shapes=[pltpu.VMEM((D_MODEL, PEER_TOKENS), F32),
                        pltpu.VMEM((n, PEER_SUB, PEER_TOKENS), F32)],
        compiler_params=_params(("arbitrary", "arbitrary")),
        name="peer_mix",
    )(hn, u_bf, *([vt] * (n + 1)), c1, e1, r2, e2n, x1)


def kernel(x, norm_mix_g, w_in, q_norm_g, k_norm_g, rel_bias, ssm_lambda_re, ssm_lambda_im, ssm_log_dt,
           ssm_b_re, ssm_b_im, ssm_c_re, ssm_c_im, ssm_d, ssm_glu_w, ssm_glu_b, attn_out_g, ssm_out_g,
           w_out, norm_ffn_g, peer_w_q, peer_keys1, peer_keys2, peer_u, peer_v):
    b, s, d = x.shape
    x2 = x.reshape(b * s, d)
    q, k, v, u, peer_u_bf = _in_proj(x2, norm_mix_g, w_in, q_norm_g, k_norm_g, peer_u)
    bias = _bias_tables(rel_bias)
    attn = _attention(q.reshape(b, s, D_ATTN), k.reshape(b, s, D_ATTN), v.reshape(b, s, D_ATTN), bias)
    ssm_n = _s5_mixer(u.reshape(b, s, D_SSM), ssm_lambda_re, ssm_lambda_im, ssm_log_dt, ssm_b_re, ssm_b_im,
                      ssm_c_re, ssm_c_im, ssm_d, ssm_glu_w, ssm_glu_b, ssm_out_g)
    x1, hn, peer_vt = _out_proj(attn.reshape(b * s, D_ATTN), ssm_n.reshape(b * s, D_SSM), x2, attn_out_g, w_out,
                                norm_ffn_g, peer_v)
    c1, e1, r2, e2n = _peer_route(hn, peer_w_q, peer_keys1, peer_keys2)
    out = _peer_mix(hn, x1, peer_u_bf, peer_vt, c1, e1, r2, e2n)
    return out.reshape(b, s, d).astype(x.dtype)
```

```python
import functools
import math

import jax
import jax.numpy as jnp
import numpy as np
from jax import lax
from jax.experimental import pallas as pl
from jax.experimental.pallas import tpu as pltpu

F32 = jnp.float32
BF16 = jnp.bfloat16

D_MODEL = 2048
HEAD_DIM = 64
N_ATTN_HEADS = 16
D_ATTN = N_ATTN_HEADS * HEAD_DIM
SSM_GROUP = 16
N_SSM_GROUPS = 64
D_SSM = N_SSM_GROUPS * SSM_GROUP
SSM_STATE = 64
D_IN_PROJ = 3 * D_ATTN + D_SSM
DILATED_BRANCHES = ((128, 1), (512, 4), (2048, 16))
BLK = 128
N_BUCKETS = 32
MAX_DISTANCE = 2048
PEER_HEADS = 8
PEER_KEYS = 128
PEER_QDIM = 256
PEER_TOPK = 16
EPS = 1e-6
NEG = -1e30
LOG2E = math.log2(math.e)

LANES = 128
SUBLANES = 8
VMEM_LIMIT = 56 * 1024 * 1024

ROW_TILE = 256
SSM_CHUNK = 128
SSM_SLAB_GROUPS = LANES // SSM_GROUP
N_SLABS = N_SSM_GROUPS // SSM_SLAB_GROUPS
SLAB_STATE = SSM_SLAB_GROUPS * SSM_STATE
ATTN_LOOKAHEAD = 2
ROUTE_TILE = 256
PEER_TOKENS = 512
PEER_EXPERTS_STEP = 512
PEER_SUB = 256
NO_RANK = 99.0
RANK_BASE = -2.0 ** 100


def _params(sem, vmem=VMEM_LIMIT):
    return pltpu.CompilerParams(dimension_semantics=sem, vmem_limit_bytes=vmem)


def _erf_gelu(x):
    return 0.5 * x * (1.0 + lax.erf(x * math.sqrt(0.5)))


def _t5_bucket(dist):
    max_exact = N_BUCKETS // 2
    n = np.maximum(dist, 0)
    nf = np.maximum(n, 1).astype(np.float32)
    large = max_exact + (np.log(nf / np.float32(max_exact)) / np.float32(math.log(MAX_DISTANCE / max_exact))
                         * np.float32(N_BUCKETS - max_exact)).astype(np.int32)
    large = np.minimum(large, N_BUCKETS - 1)
    return np.where(n < max_exact, n, large)


def _bias_kernel(bkt_ref, rb_ref, out_ref):
    bkt = bkt_ref[0]
    qi = lax.broadcasted_iota(jnp.int32, (BLK, 2 * BLK), 0)
    kj = lax.broadcasted_iota(jnp.int32, (BLK, 2 * BLK), 1)
    valid = jnp.where(kj < BLK, kj - qi, qi - (kj - BLK)) >= 0
    for h in range(N_ATTN_HEADS):
        acc = jnp.zeros((BLK, 2 * BLK), F32)
        for b in range(N_BUCKETS):
            acc = jnp.where(bkt == b, rb_ref[b * N_ATTN_HEADS + h], acc)
        out_ref[0, h] = jnp.where(valid, acc * LOG2E, NEG)


def _bias_tables(rel_bias):
    qi = np.arange(BLK)[:, None]
    kj = np.arange(2 * BLK)[None, :]
    rel = qi - kj + BLK
    buckets = jnp.asarray(np.stack([_t5_bucket(rel * dil) for _, dil in DILATED_BRANCHES]).astype(np.int32))
    nbr = len(DILATED_BRANCHES)
    return pl.pallas_call(
        _bias_kernel,
        grid=(nbr,),
        in_specs=[pl.BlockSpec((1, BLK, 2 * BLK), lambda i: (i, 0, 0)),
                  pl.BlockSpec(memory_space=pltpu.SMEM)],
        out_specs=pl.BlockSpec((1, N_ATTN_HEADS, BLK, 2 * BLK), lambda i: (i, 0, 0, 0)),
        out_shape=jax.ShapeDtypeStruct((nbr, N_ATTN_HEADS, BLK, 2 * BLK), F32),
        compiler_params=_params(("arbitrary",)),
        name="bias_table",
    )(buckets, rel_bias.astype(F32).reshape(N_BUCKETS * N_ATTN_HEADS))


def _head_rmsnorm(z, gain, ones, scale):
    outs = []
    for c in range(z.shape[1] // LANES):
        zc = z[:, LANES * c:LANES * (c + 1)]
        sq = zc * zc
        hi = sq.astype(BF16)
        lo = (sq - hi.astype(F32)).astype(BF16)
        msq = (jnp.dot(hi, ones, preferred_element_type=F32)
               + jnp.dot(lo, ones, preferred_element_type=F32))
        y = zc * lax.rsqrt(msq + EPS)
        outs.append(y * gain[:, LANES * c:LANES * (c + 1)] * scale)
    return jnp.concatenate(outs, axis=1)


def _inproj_kernel(x_ref, g_ref, w_ref, qg_ref, kg_ref, ones_ref, tab_ref, q_ref, k_ref, v_ref, u_ref, tabo_ref):
    x = x_ref[...]
    ms = jnp.mean(x * x, axis=-1, keepdims=True)
    h = (x * lax.rsqrt(ms + EPS) * g_ref[...]).astype(BF16)
    proj = jnp.dot(h, w_ref[...], preferred_element_type=F32)
    ones = ones_ref[...]
    q_ref[...] = _head_rmsnorm(proj[:, :D_ATTN], qg_ref[...], ones, LOG2E / math.sqrt(HEAD_DIM))
    k_ref[...] = _head_rmsnorm(proj[:, D_ATTN:2 * D_ATTN], kg_ref[...], ones, 1.0)
    v_ref[...] = proj[:, 2 * D_ATTN:3 * D_ATTN]
    u_ref[...] = proj[:, 3 * D_ATTN:]
    tabo_ref[...] = tab_ref[...].astype(BF16)


def _in_proj(x2, norm_g, w_in, q_g, k_g, table):
    t = x2.shape[0]
    steps = t // ROW_TILE
    tab_rows = table.shape[0] // steps
    assert tab_rows * steps == table.shape[0]
    head_of_lane = jnp.arange(LANES) // HEAD_DIM
    ones = jnp.where(head_of_lane[:, None] == head_of_lane[None, :], 1.0 / HEAD_DIM, 0.0).astype(BF16)
    qg = jnp.tile(q_g.astype(F32), N_ATTN_HEADS)[None, :]
    kg = jnp.tile(k_g.astype(F32), N_ATTN_HEADS)[None, :]
    row = lambda i: (i, 0)
    fixed = lambda i: (0, 0)
    outs = pl.pallas_call(
        _inproj_kernel,
        grid=(t // ROW_TILE,),
        in_specs=[pl.BlockSpec((ROW_TILE, D_MODEL), row),
                  pl.BlockSpec((1, D_MODEL), fixed),
                  pl.BlockSpec((D_MODEL, D_IN_PROJ), fixed, pipeline_mode=pl.Buffered(1)),
                  pl.BlockSpec((1, D_ATTN), fixed),
                  pl.BlockSpec((1, D_ATTN), fixed),
                  pl.BlockSpec((LANES, LANES), fixed),
                  pl.BlockSpec((tab_rows, D_MODEL), row)],
        out_specs=[pl.BlockSpec((ROW_TILE, D_ATTN), row)] * 3 + [pl.BlockSpec((ROW_TILE, D_SSM), row),
                                                                   pl.BlockSpec((tab_rows, D_MODEL), row)],
        out_shape=[jax.ShapeDtypeStruct((t, D_ATTN), F32)] * 3 + [jax.ShapeDtypeStruct((t, D_SSM), F32),
                                                                   jax.ShapeDtypeStruct(table.shape, BF16)],
        compiler_params=_params(("arbitrary",)),
        name="in_proj",
    )(x2, norm_g.astype(F32)[None, :], w_in.astype(BF16), qg, kg, ones, table)
    return outs


def _rows(start, size, stride):
    return pl.ds(start, size, stride=stride) if stride > 1 else pl.ds(start, size)


def _attn_scores(q_ref, k_ref, v_ref, bias_ref, blk, head0):
    br, dil, n, r = blk
    qrows = _rows(r + dil * BLK * n, BLK, dil)
    qb = q_ref[qrows, :].astype(BF16)
    krows = qrows if n == 0 else _rows(r + dil * BLK * (n - 1), 2 * BLK, dil)
    kb = k_ref[krows, :].astype(BF16)
    vb = v_ref[krows, :].astype(BF16)
    scores = []
    for h in range(2):
        mine = head0 if h == 0 else jnp.logical_not(head0)
        qh = jnp.where(mine, qb, jnp.zeros_like(qb))
        s = lax.dot_general(qh, kb, (((1,), (1,)), ((), ())), preferred_element_type=F32)
        scores.append(s + (bias_ref[br, h, :, BLK:] if n == 0 else bias_ref[br, h]))
    return scores, vb, qrows


def _attn_values(scores, vb, qrows, br, pv_ref, den_ref, m_ref, head0):
    ones = jnp.ones_like(vb)
    pv, mx = [], []
    for h in range(2):
        mine = head0 if h == 0 else jnp.logical_not(head0)
        m = jnp.max(scores[h], axis=-1, keepdims=True)
        p = jnp.exp2(scores[h] - m).astype(BF16)
        pv.append(jnp.dot(p, jnp.where(mine, vb, ones), preferred_element_type=F32))
        mx.append(m)
    pv_ref[br, qrows, :] = jnp.where(head0, pv[0], pv[1])
    den_ref[br, qrows, :] = jnp.where(head0, pv[1], pv[0])
    m_ref[br, qrows, :] = jnp.where(head0, mx[0], mx[1])


def _attn_kernel(q_ref, k_ref, v_ref, bias_ref, o_ref, pv_ref, den_ref, m_ref, *, seq):
    head0 = lax.broadcasted_iota(jnp.int32, (1, LANES), 1) < HEAD_DIM
    nbr = len(DILATED_BRANCHES)
    blocks = []
    for br, (window, dil) in enumerate(DILATED_BRANCHES):
        assert window // dil == BLK
        blocks += [(br, dil, n, r) for r in range(dil) for n in range(seq // dil // BLK)]
    ahead = [_attn_scores(q_ref, k_ref, v_ref, bias_ref, b, head0) for b in blocks[:ATTN_LOOKAHEAD]]
    for i, blk in enumerate(blocks):
        if i + ATTN_LOOKAHEAD < len(blocks):
            ahead.append(_attn_scores(q_ref, k_ref, v_ref, bias_ref, blocks[i + ATTN_LOOKAHEAD], head0))
        _attn_values(*ahead.pop(0), blk[0], pv_ref, den_ref, m_ref, head0)
    m_all = [m_ref[br] for br in range(nbr)]
    m_top = functools.reduce(jnp.maximum, m_all)
    num = jnp.zeros((seq, LANES), F32)
    den = jnp.zeros((seq, LANES), F32)
    for br in range(nbr):
        w = jnp.exp2(m_all[br] - m_top)
        num = num + w * pv_ref[br]
        den = den + w * pltpu.roll(den_ref[br], HEAD_DIM, axis=1)
    o_ref[...] = num / den


def _attention(q, k, v, bias):
    b, s, _ = q.shape
    blk = pl.BlockSpec((None, s, LANES), lambda i, p: (i, 0, p))
    nbr = len(DILATED_BRANCHES)
    return pl.pallas_call(
        functools.partial(_attn_kernel, seq=s),
        grid=(b, D_ATTN // LANES),
        in_specs=[blk, blk, blk,
                  pl.BlockSpec((nbr, 2, BLK, 2 * BLK), lambda i, p: (0, p, 0, 0))],
        out_specs=blk,
        out_shape=jax.ShapeDtypeStruct((b, s, D_ATTN), F32),
        scratch_shapes=[pltpu.VMEM((nbr, s, LANES), F32)] * 3,
        compiler_params=_params(("arbitrary", "arbitrary")),
        name="attention",
    )(q, k, v, bias)


def _zoh_kernel(lr_ref, li_ref, dt_ref, lrr_ref, lir_ref, br_ref, bi_ref,
                are_ref, aim_ref, bbr_ref, bbi_ref):
    dt = jnp.exp(dt_ref[...])

    def zoh(lr, li):
        mag = jnp.exp(lr * dt)
        a_re, a_im = mag * jnp.cos(li * dt), mag * jnp.sin(li * dt)
        den = lr * lr + li * li
        f_re = ((a_re - 1.0) * lr + a_im * li) / den
        f_im = (a_im * lr - (a_re - 1.0) * li) / den
        return a_re, a_im, f_re, f_im

    a_re, a_im, _, _ = zoh(lr_ref[...], li_ref[...])
    are_ref[...] = a_re
    aim_ref[...] = a_im
    _, _, f_re, f_im = zoh(lrr_ref[...], lir_ref[...])
    br, bi = br_ref[...], bi_ref[...]
    bbr_ref[...] = f_re * br - f_im * bi
    bbi_ref[...] = f_re * bi + f_im * br


def _ssm_zoh(lam_re, lam_im, log_dt, b_re, b_im):
    g, n, c = b_re.shape
    rep = lambda a: jnp.repeat(a.astype(F32), c, axis=1)
    a_re, a_im, bb_re, bb_im = pl.pallas_call(
        _zoh_kernel,
        out_shape=[jax.ShapeDtypeStruct((g, n), F32)] * 2 + [jax.ShapeDtypeStruct((g, n * c), F32)] * 2,
        name="ssm_zoh",
    )(lam_re.astype(F32), lam_im.astype(F32), log_dt.astype(F32)[:, None], rep(lam_re), rep(lam_im),
      b_re.astype(F32).reshape(g, n * c), b_im.astype(F32).reshape(g, n * c))
    return a_re, a_im, bb_re.reshape(g, n, c), bb_im.reshape(g, n, c)


def _ssm_kernel(u_ref, wb_ref, wc_ref, are_ref, aim_ref, d_ref, gw_ref, gb_ref, gain_ref,
                o_ref, lhs_ref, bu_ref, ysel_ref, y_ref, sr_ref, si_ref, *, nb, chunk):
    half = N_SLABS // 2
    seqs = 2 * nb
    lane_blocks = 2 * SLAB_STATE // LANES
    rows_all = seqs * chunk

    @pl.when(pl.program_id(0) == 0)
    def _():
        sr_ref[...] = jnp.zeros_like(sr_ref)
        si_ref[...] = jnp.zeros_like(si_ref)
        lhs_ref[...] = jnp.zeros_like(lhs_ref)

    for b in range(nb):
        for m in range(N_SLABS):
            gh, mp = divmod(m, half)
            lhs_ref[2 * mp + gh, pl.ds(gh * nb + b, chunk, stride=seqs), :] = u_ref[b, :, LANES * m:LANES * (m + 1)]

    for mp in range(half):
        lhs = jnp.concatenate([lhs_ref[2 * mp], lhs_ref[2 * mp + 1]], axis=1).astype(BF16)
        bu = jnp.dot(lhs, wb_ref[mp], preferred_element_type=F32)
        for j in range(lane_blocks):
            bu_ref[lane_blocks * mp + j] = bu[:, LANES * j:LANES * (j + 1)]

    def load_state(rows, mp, part):
        j0 = lane_blocks * mp + part * (lane_blocks // 2)
        return jnp.concatenate([bu_ref[j0 + j, rows, :] for j in range(lane_blocks // 2)], axis=1)

    def store_state(rows, mp, part, val):
        j0 = lane_blocks * mp + part * (lane_blocks // 2)
        for j in range(lane_blocks // 2):
            bu_ref[j0 + j, rows, :] = val[:, LANES * j:LANES * (j + 1)]

    def step(t, carry):
        base = pl.multiple_of(t * seqs, seqs)
        rows = pl.ds(base, seqs)
        new = []
        for mp in range(half):
            xr, xi = carry[2 * mp], carry[2 * mp + 1]
            ar = are_ref[:, SLAB_STATE * mp:SLAB_STATE * (mp + 1)]
            ai = aim_ref[:, SLAB_STATE * mp:SLAB_STATE * (mp + 1)]
            nr = ar * xr - ai * xi + load_state(rows, mp, 0)
            ni = ar * xi + ai * xr + load_state(rows, mp, 1)
            store_state(rows, mp, 0, nr)
            store_state(rows, mp, 1, ni)
            new += [nr, ni]
        return tuple(new)

    init = []
    for mp in range(half):
        init += [sr_ref[:, SLAB_STATE * mp:SLAB_STATE * (mp + 1)], si_ref[:, SLAB_STATE * mp:SLAB_STATE * (mp + 1)]]
    final = lax.fori_loop(0, chunk, step, tuple(init), unroll=2)
    for mp in range(half):
        sr_ref[:, SLAB_STATE * mp:SLAB_STATE * (mp + 1)] = final[2 * mp]
        si_ref[:, SLAB_STATE * mp:SLAB_STATE * (mp + 1)] = final[2 * mp + 1]

    first_half = (lax.broadcasted_iota(jnp.int32, (rows_all, LANES), 0) & nb) == 0
    for mp in range(half):
        xs = jnp.concatenate([bu_ref[lane_blocks * mp + j] for j in range(lane_blocks)], axis=1).astype(BF16)
        yy = jnp.dot(xs, wc_ref[mp], preferred_element_type=F32)
        ysel_ref[mp] = jnp.where(first_half, yy[:, :LANES], yy[:, LANES:])
    for b in range(nb):
        for m in range(N_SLABS):
            gh, mp = divmod(m, half)
            cols = slice(LANES * m, LANES * (m + 1))
            y = ysel_ref[mp, pl.ds(gh * nb + b, chunk, stride=seqs), :]
            y_ref[b * chunk:(b + 1) * chunk, cols] = y + d_ref[:, cols] * u_ref[b, :, cols]

    y = _erf_gelu(y_ref[...])
    z = jnp.dot(y.astype(BF16), gw_ref[...], preferred_element_type=F32) + gb_ref[...]
    y = y * jax.nn.sigmoid(z)
    ms = jnp.mean(y * y, axis=-1, keepdims=True)
    yn = y * lax.rsqrt(ms + EPS) * gain_ref[...]
    for b in range(nb):
        o_ref[b] = yn[b * chunk:(b + 1) * chunk].astype(o_ref.dtype)


def _s5_mixer(u, lam_re, lam_im, log_dt, b_re, b_im, c_re, c_im, d_skip, glu_w, glu_b, out_gain):
    nb, s, _ = u.shape
    a_re, a_im, bb_re, bb_im = _ssm_zoh(lam_re, lam_im, log_dt, b_re, b_im)
    eye = jnp.eye(SSM_SLAB_GROUPS, dtype=F32)

    def in_slab(bb):
        w = jnp.einsum('mgnc,gh->mgchn', bb.reshape(N_SLABS, SSM_SLAB_GROUPS, SSM_STATE, SSM_GROUP), eye)
        return w.reshape(N_SLABS, LANES, SLAB_STATE)

    def out_slab(cc):
        w = jnp.einsum('mgcn,gh->mgnhc', cc.reshape(N_SLABS, SSM_SLAB_GROUPS, SSM_GROUP, SSM_STATE), eye)
        return w.reshape(N_SLABS, SLAB_STATE, LANES)

    wb = jnp.concatenate([in_slab(bb_re), in_slab(bb_im)], axis=2)
    wc = jnp.concatenate([out_slab(c_re.astype(F32)), -out_slab(c_im.astype(F32))], axis=1)
    hs = N_SLABS // 2
    wb = jnp.concatenate([wb[:hs], wb[hs:]], axis=1).astype(BF16)
    wc = jnp.concatenate([wc[:hs], wc[hs:]], axis=2).astype(BF16)
    half_states = (N_SSM_GROUPS // 2) * SSM_STATE

    def seq_rows(a):
        return jnp.repeat(a.reshape(2, half_states), nb, axis=0)

    fixed2 = lambda c: (0, 0)
    fixed3 = lambda c: (0, 0, 0)
    chunk = SSM_CHUNK
    return pl.pallas_call(
        functools.partial(_ssm_kernel, nb=nb, chunk=chunk),
        grid=(s // chunk,),
        in_specs=[pl.BlockSpec((nb, chunk, D_SSM), lambda c: (0, c, 0)),
                  pl.BlockSpec((N_SLABS // 2, 2 * LANES, 2 * SLAB_STATE), fixed3),
                  pl.BlockSpec((N_SLABS // 2, 2 * SLAB_STATE, 2 * LANES), fixed3),
                  pl.BlockSpec((2 * nb, half_states), fixed2),
                  pl.BlockSpec((2 * nb, half_states), fixed2),
                  pl.BlockSpec((1, D_SSM), fixed2),
                  pl.BlockSpec((D_SSM, D_SSM), fixed2),
                  pl.BlockSpec((1, D_SSM), fixed2),
                  pl.BlockSpec((1, D_SSM), fixed2)],
        out_specs=pl.BlockSpec((nb, chunk, D_SSM), lambda c: (0, c, 0)),
        out_shape=jax.ShapeDtypeStruct((nb, s, D_SSM), BF16),
        scratch_shapes=[pltpu.VMEM((N_SLABS, 2 * nb * chunk, LANES), F32),
                        pltpu.VMEM((2 * half_states // LANES, 2 * nb * chunk, LANES), F32),
                        pltpu.VMEM((N_SLABS // 2, 2 * nb * chunk, LANES), F32),
                        pltpu.VMEM((nb * chunk, D_SSM), F32),
                        pltpu.VMEM((2 * nb, half_states), F32),
                        pltpu.VMEM((2 * nb, half_states), F32)],
        compiler_params=_params(("arbitrary",)),
        name="ssm",
    )(u, wb, wc, seq_rows(a_re), seq_rows(a_im), d_skip.astype(F32).reshape(1, D_SSM),
      glu_w.astype(BF16), glu_b.astype(F32)[None, :], out_gain.astype(F32)[None, :])


def _outproj_kernel(a_ref, s_ref, x_ref, ag_ref, wa_ref, ws_ref, fg_ref, tab_ref, x1_ref, hn_ref, tabt_ref):
    a = a_ref[...]
    ms = jnp.mean(a * a, axis=-1, keepdims=True)
    an = (a * lax.rsqrt(ms + EPS) * ag_ref[...]).astype(BF16)
    mixed = (jnp.dot(an, wa_ref[...], preferred_element_type=F32)
             + jnp.dot(s_ref[...], ws_ref[...], preferred_element_type=F32))
    x1 = x_ref[...] + mixed
    x1_ref[...] = x1
    ms1 = jnp.mean(x1 * x1, axis=-1, keepdims=True)
    hn_ref[...] = (x1 * lax.rsqrt(ms1 + EPS) * fg_ref[...]).astype(BF16)
    tabt_ref[...] = tab_ref[...].T.astype(BF16)


def _out_proj(attn, ssm_n, x2, attn_g, w_out, ffn_g, table):
    t = x2.shape[0]
    steps = t // ROW_TILE
    tab_rows = table.shape[0] // steps
    assert tab_rows * steps == table.shape[0]
    row = lambda i: (i, 0)
    fixed = lambda i: (0, 0)
    w = w_out.astype(BF16)
    return pl.pallas_call(
        _outproj_kernel,
        grid=(t // ROW_TILE,),
        in_specs=[pl.BlockSpec((ROW_TILE, D_ATTN), row),
                  pl.BlockSpec((ROW_TILE, D_SSM), row),
                  pl.BlockSpec((ROW_TILE, D_MODEL), row),
                  pl.BlockSpec((1, D_ATTN), fixed),
                  pl.BlockSpec((D_ATTN, D_MODEL), fixed),
                  pl.BlockSpec((D_SSM, D_MODEL), fixed),
                  pl.BlockSpec((1, D_MODEL), fixed),
                  pl.BlockSpec((tab_rows, D_MODEL), row)],
        out_specs=[pl.BlockSpec((ROW_TILE, D_MODEL), row)] * 2 + [pl.BlockSpec((None, D_MODEL, tab_rows), lambda i: (i, 0, 0))],
        out_shape=[jax.ShapeDtypeStruct((t, D_MODEL), F32), jax.ShapeDtypeStruct((t, D_MODEL), BF16),
                   jax.ShapeDtypeStruct((steps, D_MODEL, tab_rows), BF16)],
        compiler_params=_params(("arbitrary",)),
        name="out_proj",
    )(attn, ssm_n, x2, attn_g.astype(F32)[None, :], w[:D_ATTN], w[D_ATTN:], ffn_g.astype(F32)[None, :], table)


def _take16(s, exact, index=None):
    if index is None:
        index = lax.broadcasted_iota(jnp.int32, s.shape, 0)
    vals = []
    for it in range(PEER_TOPK):
        m = jnp.max(s, axis=0, keepdims=True)
        if exact:
            first = jnp.min(jnp.where(s == m, index, jnp.iinfo(jnp.int32).max), axis=0, keepdims=True)
            hit = index == first
        else:
            hit = s == m
        s = jnp.where(hit, RANK_BASE * (1.0 + it / 32.0), s)
        vals.append(m)
    return vals, s


def _taken(marked):
    return jnp.logical_and(marked < 0.5 * RANK_BASE, marked > 2.0 * RANK_BASE)


def _rank_of(marked):
    rank = jnp.floor((marked * (1.0 / RANK_BASE) - 1.0) * 32.0 + 0.5)
    return jnp.where(_taken(marked), rank, NO_RANK)


def _stack_rows(rows, n):
    iota = lax.broadcasted_iota(jnp.int32, (n, rows[0].shape[1]), 0)
    out = jnp.zeros((n, rows[0].shape[1]), F32)
    for i, r in enumerate(rows):
        out = jnp.where(iota == i, r, out)
    return out


def _count(mask):
    return jnp.sum(jnp.where(mask, 1.0, 0.0), axis=0, keepdims=True)


def _route_head(s1, s2, exact):
    tb = s1.shape[1]
    v1, marked1 = _take16(s1, exact)
    v2, marked2 = _take16(s2, exact)
    rank1, rank2 = _rank_of(marked1), _rank_of(marked2)
    v2_all = _stack_rows(v2, PEER_TOPK)
    v1_hi = _stack_rows(v1[SUBLANES:], SUBLANES)
    sub = lax.broadcasted_iota(jnp.int32, (SUBLANES, tb), 0)
    pieces = [v1[0] + v2_all]
    flats = [lax.broadcasted_iota(jnp.int32, (PEER_TOPK, tb), 0)]
    for a in range(1, SUBLANES):
        limit = PEER_TOPK // (a + 1)
        pieces.append(jnp.where(sub < limit, v1[a] + v2_all[:SUBLANES], -jnp.inf))
        flats.append(a * PEER_TOPK + sub)
    pieces.append(v1_hi + v2[0])
    flats.append((sub + SUBLANES) * PEER_TOPK)
    cand = jnp.concatenate(pieces, axis=0)
    flat = jnp.concatenate(flats, axis=0)
    _, marked = _take16(cand, exact, flat)
    sel = jnp.where(_taken(marked), 1.0, 0.0)
    top = v1[0] + v2[0]
    z = jnp.sum(sel * jnp.exp(cand - top), axis=0, keepdims=True)
    cnt = [jnp.sum(sel[:PEER_TOPK], axis=0, keepdims=True)]
    for a in range(1, SUBLANES):
        lo = PEER_TOPK + SUBLANES * (a - 1)
        cnt.append(jnp.sum(sel[lo:lo + SUBLANES], axis=0, keepdims=True))
    lo = PEER_TOPK + SUBLANES * (SUBLANES - 1)
    for i in range(SUBLANES):
        cnt.append(sel[lo + i:lo + i + 1])
    c1 = jnp.zeros(rank1.shape, F32)
    for a in range(PEER_TOPK):
        c1 = jnp.where(rank1 == float(a), cnt[a], c1)
    e1 = jnp.exp(s1 - v1[0])
    e2n = jnp.exp(s2 - v2[0]) / z
    taken = jnp.maximum(jnp.maximum(_count(_taken(marked1)), _count(_taken(marked2))), jnp.sum(sel, axis=0, keepdims=True))
    return c1, e1, rank2, e2n, taken


def _route_one(h, qt_ref, k1_ref, k2_ref, c1_ref, e1_ref, r2_ref, e2_ref, exact):
    half = PEER_QDIM // 2
    q1 = qt_ref[PEER_QDIM * h:PEER_QDIM * h + half, :].astype(BF16)
    q2 = qt_ref[PEER_QDIM * h + half:PEER_QDIM * (h + 1), :].astype(BF16)
    s1 = jnp.dot(k1_ref[h], q1, preferred_element_type=F32)
    s2 = jnp.dot(k2_ref[h], q2, preferred_element_type=F32)
    c1, e1, r2, e2n, taken = _route_head(s1, s2, exact)
    c1_ref[h] = c1
    e1_ref[h] = e1
    r2_ref[h] = r2.astype(r2_ref.dtype)
    e2_ref[h] = e2n.astype(e2_ref.dtype)
    return taken


def _route_kernel(hn_ref, wq_ref, k1_ref, k2_ref, c1_ref, e1_ref, r2_ref, e2_ref, qt_ref):
    qt_ref[...] = lax.dot_general(wq_ref[...], hn_ref[...], (((1,), (1,)), ((), ())), preferred_element_type=F32)
    refs = (qt_ref, k1_ref, k2_ref, c1_ref, e1_ref, r2_ref, e2_ref)
    taken = [_route_one(h, *refs, exact=False) for h in range(PEER_HEADS)]
    for h in range(PEER_HEADS):
        @pl.when(jnp.max(taken[h]) > float(PEER_TOPK))
        def _():
            _route_one(h, *refs, exact=True)


def _peer_route(hn, w_q, keys1, keys2):
    t = hn.shape[0]
    blk = pl.BlockSpec((PEER_HEADS, PEER_KEYS, ROUTE_TILE), lambda i: (0, 0, i))
    fixed3 = lambda i: (0, 0, 0)
    shape = (PEER_HEADS, PEER_KEYS, t)
    return pl.pallas_call(
        _route_kernel,
        grid=(t // ROUTE_TILE,),
        in_specs=[pl.BlockSpec((ROUTE_TILE, D_MODEL), lambda i: (i, 0)),
                  pl.BlockSpec((PEER_HEADS * PEER_QDIM, D_MODEL), lambda i: (0, 0)),
                  pl.BlockSpec((PEER_HEADS, PEER_KEYS, PEER_QDIM // 2), fixed3),
                  pl.BlockSpec((PEER_HEADS, PEER_KEYS, PEER_QDIM // 2), fixed3)],
        out_specs=[blk] * 4,
        out_shape=[jax.ShapeDtypeStruct(shape, F32), jax.ShapeDtypeStruct(shape, F32),
                   jax.ShapeDtypeStruct(shape, BF16), jax.ShapeDtypeStruct(shape, BF16)],
        scratch_shapes=[pltpu.VMEM((PEER_HEADS * PEER_QDIM, ROUTE_TILE), F32)],
        compiler_params=_params(("arbitrary",)),
        name="peer_route",
    )(hn, w_q.T.astype(BF16), keys1.astype(BF16), keys2.astype(BF16))


def _peer_gated(c1_ref, e1_ref, r2_ref, e2_ref, at_ref, chunk, valid):
    packed = 2 * SUBLANES
    tiles = PEER_SUB // PEER_KEYS
    gs = []
    for tl in range(tiles):
        tile = chunk * tiles + tl
        gate = jnp.zeros((PEER_KEYS // packed, packed, PEER_TOKENS), BF16)
        for h in range(PEER_HEADS):
            c1 = jnp.broadcast_to(c1_ref[h, pl.ds(tile, 1), :], (packed, PEER_TOKENS)).astype(BF16)
            e1 = jnp.broadcast_to(e1_ref[h, pl.ds(tile, 1), :], (packed, PEER_TOKENS)).astype(BF16)
            r2 = r2_ref[h].reshape(gate.shape)
            e2 = e2_ref[h].reshape(gate.shape)
            gate = gate + jnp.where(r2 < c1[None], e1[None] * e2, jnp.zeros_like(e2))
        a = at_ref[PEER_KEYS * tl:PEER_KEYS * (tl + 1), :].astype(BF16)
        gs.append(gate.reshape(PEER_KEYS, PEER_TOKENS) * _erf_gelu(a))
    g = jnp.concatenate(gs, axis=0)
    return g if valid is True else jnp.where(valid, g, jnp.zeros_like(g))


def _peer_kernel(hn_ref, u_ref, *refs, n_chunks):
    n = PEER_EXPERTS_STEP // PEER_SUB
    vt_refs, vt_tail_ref, route = refs[:n], refs[n], refs[n + 1:n + 5]
    x1_ref, o_ref, acc_ref, at_ref = refs[n + 5:]
    e = pl.program_id(1)
    last = pl.num_programs(1) - 1
    nt = (((1,), (1,)), ((), ()))

    @pl.when(e == 0)
    def _():
        acc_ref[...] = jnp.zeros_like(acc_ref)
        at_ref[n - 1] = jnp.zeros((PEER_SUB, PEER_TOKENS), F32)

    hn = hn_ref[...]
    total = None
    for c in range(n):
        at_ref[c] = lax.dot_general(u_ref[PEER_SUB * c:PEER_SUB * (c + 1), :], hn, nt, preferred_element_type=F32)
        prev = n * e + c - 1
        g = _peer_gated(*route, at_ref.at[(c - 1) % n], jnp.maximum(prev, 0), (e > 0) if c == 0 else True)
        part = jnp.dot(vt_refs[c][...], g, preferred_element_type=F32)
        total = part if total is None else total + part
    acc_ref[...] += total

    @pl.when(e == last)
    def _():
        g = _peer_gated(*route, at_ref.at[n - 1], n_chunks - 1, True)
        tail = jnp.dot(vt_tail_ref[...], g, preferred_element_type=F32)
        o_ref[...] = x1_ref[...] + (acc_ref[...] + tail).T


def _peer_mix(hn, x1, u_bf, vt, c1, e1, r2, e2n):
    t = hn.shape[0]
    n = PEER_EXPERTS_STEP // PEER_SUB
    n_steps = u_bf.shape[0] // PEER_EXPERTS_STEP
    n_chunks = n * n_steps
    route = pl.BlockSpec((PEER_HEADS, PEER_KEYS, PEER_TOKENS), lambda i, e: (0, 0, i))
    tok = lambda i, e: (i, 0)
    per_block = vt.shape[2] // PEER_SUB

    def vt_spec(chunk_of):
        def index(i, e):
            chunk = chunk_of(e)
            return chunk // per_block, 0, chunk % per_block
        return pl.BlockSpec((None, D_MODEL, PEER_SUB), index)

    vt_specs = [vt_spec(functools.partial(lambda e, c: jnp.maximum(n * e + c - 1, 0), c=c)) for c in range(n)]
    vt_specs.append(vt_spec(lambda e: n_chunks - 1))
    return pl.pallas_call(
        functools.partial(_peer_kernel, n_chunks=n_chunks),
        grid=(t // PEER_TOKENS, n_steps),
        in_specs=[pl.BlockSpec((PEER_TOKENS, D_MODEL), tok),
                  pl.BlockSpec((PEER_EXPERTS_STEP, D_MODEL), lambda i, e: (e, 0)),
                  *vt_specs,
                  route, route, route, route,
                  pl.BlockSpec((PEER_TOKENS, D_MODEL), tok)],
        out_specs=pl.BlockSpec((PEER_TOKENS, D_MODEL), tok),
        out_shape=jax.ShapeDtypeStruct((t, D_MODEL), F32),
        scratch_shapes=[pltpu.VMEM((D_MODEL, PEER_TOKENS), F32),
                        pltpu.VMEM((n, PEER_SUB, PEER_TOKENS), F32)],
        compiler_params=_params(("arbitrary", "arbitrary")),
        name="peer_mix",
    )(hn, u_bf, *([vt] * (n + 1)), c1, e1, r2, e2n, x1)


def kernel(x, norm_mix_g, w_in, q_norm_g, k_norm_g, rel_bias, ssm_lambda_re, ssm_lambda_im, ssm_log_dt,
           ssm_b_re, ssm_b_im, ssm_c_re, ssm_c_im, ssm_d, ssm_glu_w, ssm_glu_b, attn_out_g, ssm_out_g,
           w_out, norm_ffn_g, peer_w_q, peer_keys1, peer_keys2, peer_u, peer_v):
    b, s, d = x.shape
    x2 = x.reshape(b * s, d)
    q, k, v, u, peer_u_bf = _in_proj(x2, norm_mix_g, w_in, q_norm_g, k_norm_g, peer_u)
    bias = _bias_tables(rel_bias)
    attn = _attention(q.reshape(b, s, D_ATTN), k.reshape(b, s, D_ATTN), v.reshape(b, s, D_ATTN), bias)
    ssm_n = _s5_mixer(u.reshape(b, s, D_SSM), ssm_lambda_re, ssm_lambda_im, ssm_log_dt, ssm_b_re, ssm_b_im,
                      ssm_c_re, ssm_c_im, ssm_d, ssm_glu_w, ssm_glu_b, ssm_out_g)
    x1, hn, peer_vt = _out_proj(attn.reshape(b * s, D_ATTN), ssm_n.reshape(b * s, D_SSM), x2, attn_out_g, w_out,
                                norm_ffn_g, peer_v)
    c1, e1, r2, e2n = _peer_route(hn, peer_w_q, peer_keys1, peer_keys2)
    out = _peer_mix(hn, x1, peer_u_bf, peer_vt, c1, e1, r2, e2n)
    return out.reshape(b, s, d).astype(x.dtype)
```

```python
import functools
import math

import jax
import jax.numpy as jnp
import numpy as np
from jax import lax
from jax.experimental import pallas as pl
from jax.experimental.pallas import tpu as pltpu

F32 = jnp.float32
BF16 = jnp.bfloat16

D_MODEL = 2048
HEAD_DIM = 64
N_ATTN_HEADS = 16
D_ATTN = N_ATTN_HEADS * HEAD_DIM
SSM_GROUP = 16
N_SSM_GROUPS = 64
D_SSM = N_SSM_GROUPS * SSM_GROUP
SSM_STATE = 64
D_IN_PROJ = 3 * D_ATTN + D_SSM
DILATED_BRANCHES = ((128, 1), (512, 4), (2048, 16))
BLK = 128
N_BUCKETS = 32
MAX_DISTANCE = 2048
PEER_HEADS = 8
PEER_KEYS = 128
PEER_QDIM = 256
PEER_TOPK = 16
EPS = 1e-6
NEG = -1e30
LOG2E = math.log2(math.e)

LANES = 128
SUBLANES = 8
VMEM_LIMIT = 56 * 1024 * 1024

ROW_TILE = 256
SSM_CHUNK = 128
SSM_SLAB_GROUPS = LANES // SSM_GROUP
N_SLABS = N_SSM_GROUPS // SSM_SLAB_GROUPS
SLAB_STATE = SSM_SLAB_GROUPS * SSM_STATE
ATTN_LOOKAHEAD = 2
ROUTE_TILE = 256
PEER_TOKENS = 512
PEER_EXPERTS_STEP = 512
PEER_SUB = 256
NO_RANK = 99.0
RANK_BASE = -2.0 ** 100


def _params(sem, vmem=VMEM_LIMIT):
    return pltpu.CompilerParams(dimension_semantics=sem, vmem_limit_bytes=vmem)


def _erf_gelu(x):
    return 0.5 * x * (1.0 + lax.erf(x * math.sqrt(0.5)))


def _t5_bucket(dist):
    max_exact = N_BUCKETS // 2
    n = np.maximum(dist, 0)
    nf = np.maximum(n, 1).astype(np.float32)
    large = max_exact + (np.log(nf / np.float32(max_exact)) / np.float32(math.log(MAX_DISTANCE / max_exact))
                         * np.float32(N_BUCKETS - max_exact)).astype(np.int32)
    large = np.minimum(large, N_BUCKETS - 1)
    return np.where(n < max_exact, n, large)


def _bias_kernel(bkt_ref, rb_ref, out_ref):
    bkt = bkt_ref[0]
    qi = lax.broadcasted_iota(jnp.int32, (BLK, 2 * BLK), 0)
    kj = lax.broadcasted_iota(jnp.int32, (BLK, 2 * BLK), 1)
    valid = jnp.where(kj < BLK, kj - qi, qi - (kj - BLK)) >= 0
    for h in range(N_ATTN_HEADS):
        acc = jnp.zeros((BLK, 2 * BLK), F32)
        for b in range(N_BUCKETS):
            acc = jnp.where(bkt == b, rb_ref[b * N_ATTN_HEADS + h], acc)
        out_ref[0, h] = jnp.where(valid, acc * LOG2E, NEG)


def _bias_tables(rel_bias):
    qi = np.arange(BLK)[:, None]
    kj = np.arange(2 * BLK)[None, :]
    rel = qi - kj + BLK
    buckets = jnp.asarray(np.stack([_t5_bucket(rel * dil) for _, dil in DILATED_BRANCHES]).astype(np.int32))
    nbr = len(DILATED_BRANCHES)
    return pl.pallas_call(
        _bias_kernel,
        grid=(nbr,),
        in_specs=[pl.BlockSpec((1, BLK, 2 * BLK), lambda i: (i, 0, 0)),
                  pl.BlockSpec(memory_space=pltpu.SMEM)],
        out_specs=pl.BlockSpec((1, N_ATTN_HEADS, BLK, 2 * BLK), lambda i: (i, 0, 0, 0)),
        out_shape=jax.ShapeDtypeStruct((nbr, N_ATTN_HEADS, BLK, 2 * BLK), F32),
        compiler_params=_params(("arbitrary",)),
        name="bias_table",
    )(buckets, rel_bias.astype(F32).reshape(N_BUCKETS * N_ATTN_HEADS))


def _head_rmsnorm(z, gain, ones, scale):
    outs = []
    for c in range(z.shape[1] // LANES):
        zc = z[:, LANES * c:LANES * (c + 1)]
        sq = zc * zc
        hi = sq.astype(BF16)
        lo = (sq - hi.astype(F32)).astype(BF16)
        msq = (jnp.dot(hi, ones, preferred_element_type=F32)
               + jnp.dot(lo, ones, preferred_element_type=F32))
        y = zc * lax.rsqrt(msq + EPS)
        outs.append(y * gain[:, LANES * c:LANES * (c + 1)] * scale)
    return jnp.concatenate(outs, axis=1)


def _inproj_kernel(x_ref, g_ref, w_ref, qg_ref, kg_ref, ones_ref, tab_ref, q_ref, k_ref, v_ref, u_ref, tabo_ref):
    x = x_ref[...]
    ms = jnp.mean(x * x, axis=-1, keepdims=True)
    h = (x * lax.rsqrt(ms + EPS) * g_ref[...]).astype(BF16)
    proj = jnp.dot(h, w_ref[...], preferred_element_type=F32)
    ones = ones_ref[...]
    q_ref[...] = _head_rmsnorm(proj[:, :D_ATTN], qg_ref[...], ones, LOG2E / math.sqrt(HEAD_DIM))
    k_ref[...] = _head_rmsnorm(proj[:, D_ATTN:2 * D_ATTN], kg_ref[...], ones, 1.0)
    v_ref[...] = proj[:, 2 * D_ATTN:3 * D_ATTN]
    u_ref[...] = proj[:, 3 * D_ATTN:]
    tabo_ref[...] = tab_ref[...].T.astype(BF16)


def _in_proj(x2, norm_g, w_in, q_g, k_g, table):
    t = x2.shape[0]
    steps = t // ROW_TILE
    tab_rows = table.shape[0] // steps
    assert tab_rows * steps == table.shape[0]
    head_of_lane = jnp.arange(LANES) // HEAD_DIM
    ones = jnp.where(head_of_lane[:, None] == head_of_lane[None, :], 1.0 / HEAD_DIM, 0.0).astype(BF16)
    qg = jnp.tile(q_g.astype(F32), N_ATTN_HEADS)[None, :]
    kg = jnp.tile(k_g.astype(F32), N_ATTN_HEADS)[None, :]
    row = lambda i: (i, 0)
    fixed = lambda i: (0, 0)
    outs = pl.pallas_call(
        _inproj_kernel,
        grid=(t // ROW_TILE,),
        in_specs=[pl.BlockSpec((ROW_TILE, D_MODEL), row),
                  pl.BlockSpec((1, D_MODEL), fixed),
                  pl.BlockSpec((D_MODEL, D_IN_PROJ), fixed, pipeline_mode=pl.Buffered(1)),
                  pl.BlockSpec((1, D_ATTN), fixed),
                  pl.BlockSpec((1, D_ATTN), fixed),
                  pl.BlockSpec((LANES, LANES), fixed),
                  pl.BlockSpec((tab_rows, D_MODEL), row)],
        out_specs=[pl.BlockSpec((ROW_TILE, D_ATTN), row)] * 3 + [pl.BlockSpec((ROW_TILE, D_SSM), row),
                                                                   pl.BlockSpec((D_MODEL, tab_rows), lambda i: (0, i))],
        out_shape=[jax.ShapeDtypeStruct((t, D_ATTN), F32)] * 3 + [jax.ShapeDtypeStruct((t, D_SSM), F32),
                                                                   jax.ShapeDtypeStruct(table.shape[::-1], BF16)],
        compiler_params=_params(("arbitrary",)),
        name="in_proj",
    )(x2, norm_g.astype(F32)[None, :], w_in.astype(BF16), qg, kg, ones, table)
    return outs


def _rows(start, size, stride):
    return pl.ds(start, size, stride=stride) if stride > 1 else pl.ds(start, size)


def _attn_scores(q_ref, k_ref, v_ref, bias_ref, blk, head0):
    br, dil, n, r = blk
    qrows = _rows(r + dil * BLK * n, BLK, dil)
    qb = q_ref[qrows, :].astype(BF16)
    krows = qrows if n == 0 else _rows(r + dil * BLK * (n - 1), 2 * BLK, dil)
    kb = k_ref[krows, :].astype(BF16)
    vb = v_ref[krows, :].astype(BF16)
    scores = []
    for h in range(2):
        mine = head0 if h == 0 else jnp.logical_not(head0)
        qh = jnp.where(mine, qb, jnp.zeros_like(qb))
        s = lax.dot_general(qh, kb, (((1,), (1,)), ((), ())), preferred_element_type=F32)
        scores.append(s + (bias_ref[br, h, :, BLK:] if n == 0 else bias_ref[br, h]))
    return scores, vb, qrows


def _attn_values(scores, vb, qrows, br, pv_ref, den_ref, m_ref, head0):
    ones = jnp.ones_like(vb)
    pv, mx = [], []
    for h in range(2):
        mine = head0 if h == 0 else jnp.logical_not(head0)
        m = jnp.max(scores[h], axis=-1, keepdims=True)
        p = jnp.exp2(scores[h] - m).astype(BF16)
        pv.append(jnp.dot(p, jnp.where(mine, vb, ones), preferred_element_type=F32))
        mx.append(m)
    pv_ref[br, qrows, :] = jnp.where(head0, pv[0], pv[1])
    den_ref[br, qrows, :] = jnp.where(head0, pv[1], pv[0])
    m_ref[br, qrows, :] = jnp.where(head0, mx[0], mx[1])


def _attn_kernel(q_ref, k_ref, v_ref, bias_ref, o_ref, pv_ref, den_ref, m_ref, *, seq):
    head0 = lax.broadcasted_iota(jnp.int32, (1, LANES), 1) < HEAD_DIM
    nbr = len(DILATED_BRANCHES)
    blocks = []
    for br, (window, dil) in enumerate(DILATED_BRANCHES):
        assert window // dil == BLK
        blocks += [(br, dil, n, r) for r in range(dil) for n in range(seq // dil // BLK)]
    ahead = [_attn_scores(q_ref, k_ref, v_ref, bias_ref, b, head0) for b in blocks[:ATTN_LOOKAHEAD]]
    for i, blk in enumerate(blocks):
        if i + ATTN_LOOKAHEAD < len(blocks):
            ahead.append(_attn_scores(q_ref, k_ref, v_ref, bias_ref, blocks[i + ATTN_LOOKAHEAD], head0))
        _attn_values(*ahead.pop(0), blk[0], pv_ref, den_ref, m_ref, head0)
    m_all = [m_ref[br] for br in range(nbr)]
    m_top = functools.reduce(jnp.maximum, m_all)
    num = jnp.zeros((seq, LANES), F32)
    den = jnp.zeros((seq, LANES), F32)
    for br in range(nbr):
        w = jnp.exp2(m_all[br] - m_top)
        num = num + w * pv_ref[br]
        den = den + w * pltpu.roll(den_ref[br], HEAD_DIM, axis=1)
    o_ref[...] = num / den


def _attention(q, k, v, bias):
    b, s, _ = q.shape
    blk = pl.BlockSpec((None, s, LANES), lambda i, p: (i, 0, p))
    nbr = len(DILATED_BRANCHES)
    return pl.pallas_call(
        functools.partial(_attn_kernel, seq=s),
        grid=(b, D_ATTN // LANES),
        in_specs=[blk, blk, blk,
                  pl.BlockSpec((nbr, 2, BLK, 2 * BLK), lambda i, p: (0, p, 0, 0))],
        out_specs=blk,
        out_shape=jax.ShapeDtypeStruct((b, s, D_ATTN), F32),
        scratch_shapes=[pltpu.VMEM((nbr, s, LANES), F32)] * 3,
        compiler_params=_params(("arbitrary", "arbitrary")),
        name="attention",
    )(q, k, v, bias)


def _zoh_kernel(lr_ref, li_ref, dt_ref, lrr_ref, lir_ref, br_ref, bi_ref,
                are_ref, aim_ref, bbr_ref, bbi_ref):
    dt = jnp.exp(dt_ref[...])

    def zoh(lr, li):
        mag = jnp.exp(lr * dt)
        a_re, a_im = mag * jnp.cos(li * dt), mag * jnp.sin(li * dt)
        den = lr * lr + li * li
        f_re = ((a_re - 1.0) * lr + a_im * li) / den
        f_im = (a_im * lr - (a_re - 1.0) * li) / den
        return a_re, a_im, f_re, f_im

    a_re, a_im, _, _ = zoh(lr_ref[...], li_ref[...])
    are_ref[...] = a_re
    aim_ref[...] = a_im
    _, _, f_re, f_im = zoh(lrr_ref[...], lir_ref[...])
    br, bi = br_ref[...], bi_ref[...]
    bbr_ref[...] = f_re * br - f_im * bi
    bbi_ref[...] = f_re * bi + f_im * br


def _ssm_zoh(lam_re, lam_im, log_dt, b_re, b_im):
    g, n, c = b_re.shape
    rep = lambda a: jnp.repeat(a.astype(F32), c, axis=1)
    a_re, a_im, bb_re, bb_im = pl.pallas_call(
        _zoh_kernel,
        out_shape=[jax.ShapeDtypeStruct((g, n), F32)] * 2 + [jax.ShapeDtypeStruct((g, n * c), F32)] * 2,
        name="ssm_zoh",
    )(lam_re.astype(F32), lam_im.astype(F32), log_dt.astype(F32)[:, None], rep(lam_re), rep(lam_im),
      b_re.astype(F32).reshape(g, n * c), b_im.astype(F32).reshape(g, n * c))
    return a_re, a_im, bb_re.reshape(g, n, c), bb_im.reshape(g, n, c)


def _ssm_kernel(u_ref, wb_ref, wc_ref, are_ref, aim_ref, d_ref, gw_ref, gb_ref, gain_ref,
                o_ref, lhs_ref, bu_ref, ysel_ref, y_ref, sr_ref, si_ref, *, nb, chunk):
    half = N_SLABS // 2
    seqs = 2 * nb
    lane_blocks = 2 * SLAB_STATE // LANES
    rows_all = seqs * chunk

    @pl.when(pl.program_id(0) == 0)
    def _():
        sr_ref[...] = jnp.zeros_like(sr_ref)
        si_ref[...] = jnp.zeros_like(si_ref)
        lhs_ref[...] = jnp.zeros_like(lhs_ref)

    for b in range(nb):
        for m in range(N_SLABS):
            gh, mp = divmod(m, half)
            lhs_ref[2 * mp + gh, pl.ds(gh * nb + b, chunk, stride=seqs), :] = u_ref[b, :, LANES * m:LANES * (m + 1)]

    for mp in range(half):
        lhs = jnp.concatenate([lhs_ref[2 * mp], lhs_ref[2 * mp + 1]], axis=1).astype(BF16)
        bu = jnp.dot(lhs, wb_ref[mp], preferred_element_type=F32)
        for j in range(lane_blocks):
            bu_ref[lane_blocks * mp + j] = bu[:, LANES * j:LANES * (j + 1)]

    def load_state(rows, mp, part):
        j0 = lane_blocks * mp + part * (lane_blocks // 2)
        return jnp.concatenate([bu_ref[j0 + j, rows, :] for j in range(lane_blocks // 2)], axis=1)

    def store_state(rows, mp, part, val):
        j0 = lane_blocks * mp + part * (lane_blocks // 2)
        for j in range(lane_blocks // 2):
            bu_ref[j0 + j, rows, :] = val[:, LANES * j:LANES * (j + 1)]

    def step(t, carry):
        base = pl.multiple_of(t * seqs, seqs)
        rows = pl.ds(base, seqs)
        new = []
        for mp in range(half):
            xr, xi = carry[2 * mp], carry[2 * mp + 1]
            ar = are_ref[:, SLAB_STATE * mp:SLAB_STATE * (mp + 1)]
            ai = aim_ref[:, SLAB_STATE * mp:SLAB_STATE * (mp + 1)]
            nr = ar * xr - ai * xi + load_state(rows, mp, 0)
            ni = ar * xi + ai * xr + load_state(rows, mp, 1)
            store_state(rows, mp, 0, nr)
            store_state(rows, mp, 1, ni)
            new += [nr, ni]
        return tuple(new)

    init = []
    for mp in range(half):
        init += [sr_ref[:, SLAB_STATE * mp:SLAB_STATE * (mp + 1)], si_ref[:, SLAB_STATE * mp:SLAB_STATE * (mp + 1)]]
    final = lax.fori_loop(0, chunk, step, tuple(init), unroll=2)
    for mp in range(half):
        sr_ref[:, SLAB_STATE * mp:SLAB_STATE * (mp + 1)] = final[2 * mp]
        si_ref[:, SLAB_STATE * mp:SLAB_STATE * (mp + 1)] = final[2 * mp + 1]

    first_half = (lax.broadcasted_iota(jnp.int32, (rows_all, LANES), 0) & nb) == 0
    for mp in range(half):
        xs = jnp.concatenate([bu_ref[lane_blocks * mp + j] for j in range(lane_blocks)], axis=1).astype(BF16)
        yy = jnp.dot(xs, wc_ref[mp], preferred_element_type=F32)
        ysel_ref[mp] = jnp.where(first_half, yy[:, :LANES], yy[:, LANES:])
    for b in range(nb):
        for m in range(N_SLABS):
            gh, mp = divmod(m, half)
            cols = slice(LANES * m, LANES * (m + 1))
            y = ysel_ref[mp, pl.ds(gh * nb + b, chunk, stride=seqs), :]
            y_ref[b * chunk:(b + 1) * chunk, cols] = y + d_ref[:, cols] * u_ref[b, :, cols]

    y = _erf_gelu(y_ref[...])
    z = jnp.dot(y.astype(BF16), gw_ref[...], preferred_element_type=F32) + gb_ref[...]
    y = y * jax.nn.sigmoid(z)
    ms = jnp.mean(y * y, axis=-1, keepdims=True)
    yn = y * lax.rsqrt(ms + EPS) * gain_ref[...]
    for b in range(nb):
        o_ref[b] = yn[b * chunk:(b + 1) * chunk].astype(o_ref.dtype)


def _s5_mixer(u, lam_re, lam_im, log_dt, b_re, b_im, c_re, c_im, d_skip, glu_w, glu_b, out_gain):
    nb, s, _ = u.shape
    a_re, a_im, bb_re, bb_im = _ssm_zoh(lam_re, lam_im, log_dt, b_re, b_im)
    eye = jnp.eye(SSM_SLAB_GROUPS, dtype=F32)

    def in_slab(bb):
        w = jnp.einsum('mgnc,gh->mgchn', bb.reshape(N_SLABS, SSM_SLAB_GROUPS, SSM_STATE, SSM_GROUP), eye)
        return w.reshape(N_SLABS, LANES, SLAB_STATE)

    def out_slab(cc):
        w = jnp.einsum('mgcn,gh->mgnhc', cc.reshape(N_SLABS, SSM_SLAB_GROUPS, SSM_GROUP, SSM_STATE), eye)
        return w.reshape(N_SLABS, SLAB_STATE, LANES)

    wb = jnp.concatenate([in_slab(bb_re), in_slab(bb_im)], axis=2)
    wc = jnp.concatenate([out_slab(c_re.astype(F32)), -out_slab(c_im.astype(F32))], axis=1)
    hs = N_SLABS // 2
    wb = jnp.concatenate([wb[:hs], wb[hs:]], axis=1).astype(BF16)
    wc = jnp.concatenate([wc[:hs], wc[hs:]], axis=2).astype(BF16)
    half_states = (N_SSM_GROUPS // 2) * SSM_STATE

    def seq_rows(a):
        return jnp.repeat(a.reshape(2, half_states), nb, axis=0)

    fixed2 = lambda c: (0, 0)
    fixed3 = lambda c: (0, 0, 0)
    chunk = SSM_CHUNK
    return pl.pallas_call(
        functools.partial(_ssm_kernel, nb=nb, chunk=chunk),
        grid=(s // chunk,),
        in_specs=[pl.BlockSpec((nb, chunk, D_SSM), lambda c: (0, c, 0)),
                  pl.BlockSpec((N_SLABS // 2, 2 * LANES, 2 * SLAB_STATE), fixed3),
                  pl.BlockSpec((N_SLABS // 2, 2 * SLAB_STATE, 2 * LANES), fixed3),
                  pl.BlockSpec((2 * nb, half_states), fixed2),
                  pl.BlockSpec((2 * nb, half_states), fixed2),
                  pl.BlockSpec((1, D_SSM), fixed2),
                  pl.BlockSpec((D_SSM, D_SSM), fixed2),
                  pl.BlockSpec((1, D_SSM), fixed2),
                  pl.BlockSpec((1, D_SSM), fixed2)],
        out_specs=pl.BlockSpec((nb, chunk, D_SSM), lambda c: (0, c, 0)),
        out_shape=jax.ShapeDtypeStruct((nb, s, D_SSM), BF16),
        scratch_shapes=[pltpu.VMEM((N_SLABS, 2 * nb * chunk, LANES), F32),
                        pltpu.VMEM((2 * half_states // LANES, 2 * nb * chunk, LANES), F32),
                        pltpu.VMEM((N_SLABS // 2, 2 * nb * chunk, LANES), F32),
                        pltpu.VMEM((nb * chunk, D_SSM), F32),
                        pltpu.VMEM((2 * nb, half_states), F32),
                        pltpu.VMEM((2 * nb, half_states), F32)],
        compiler_params=_params(("arbitrary",)),
        name="ssm",
    )(u, wb, wc, seq_rows(a_re), seq_rows(a_im), d_skip.astype(F32).reshape(1, D_SSM),
      glu_w.astype(BF16), glu_b.astype(F32)[None, :], out_gain.astype(F32)[None, :])


def _outproj_kernel(a_ref, s_ref, x_ref, ag_ref, wa_ref, ws_ref, fg_ref, tab_ref, x1_ref, hn_ref, tabt_ref):
    a = a_ref[...]
    ms = jnp.mean(a * a, axis=-1, keepdims=True)
    an = (a * lax.rsqrt(ms + EPS) * ag_ref[...]).astype(BF16)
    mixed = (jnp.dot(an, wa_ref[...], preferred_element_type=F32)
             + jnp.dot(s_ref[...], ws_ref[...], preferred_element_type=F32))
    x1 = x_ref[...] + mixed
    x1_ref[...] = x1
    ms1 = jnp.mean(x1 * x1, axis=-1, keepdims=True)
    hn_ref[...] = (x1 * lax.rsqrt(ms1 + EPS) * fg_ref[...]).astype(BF16)
    tabt_ref[...] = tab_ref[...].astype(BF16)


def _out_proj(attn, ssm_n, x2, attn_g, w_out, ffn_g, table):
    t = x2.shape[0]
    steps = t // ROW_TILE
    tab_rows = table.shape[0] // steps
    assert tab_rows * steps == table.shape[0]
    row = lambda i: (i, 0)
    fixed = lambda i: (0, 0)
    w = w_out.astype(BF16)
    return pl.pallas_call(
        _outproj_kernel,
        grid=(t // ROW_TILE,),
        in_specs=[pl.BlockSpec((ROW_TILE, D_ATTN), row),
                  pl.BlockSpec((ROW_TILE, D_SSM), row),
                  pl.BlockSpec((ROW_TILE, D_MODEL), row),
                  pl.BlockSpec((1, D_ATTN), fixed),
                  pl.BlockSpec((D_ATTN, D_MODEL), fixed),
                  pl.BlockSpec((D_SSM, D_MODEL), fixed),
                  pl.BlockSpec((1, D_MODEL), fixed),
                  pl.BlockSpec((tab_rows, D_MODEL), row)],
        out_specs=[pl.BlockSpec((ROW_TILE, D_MODEL), row)] * 2 + [pl.BlockSpec((tab_rows, D_MODEL), row)],
        out_shape=[jax.ShapeDtypeStruct((t, D_MODEL), F32), jax.ShapeDtypeStruct((t, D_MODEL), BF16),
                   jax.ShapeDtypeStruct(table.shape, BF16)],
        compiler_params=_params(("arbitrary",)),
        name="out_proj",
    )(attn, ssm_n, x2, attn_g.astype(F32)[None, :], w[:D_ATTN], w[D_ATTN:], ffn_g.astype(F32)[None, :], table)


def _take16(s, exact, index=None):
    if index is None:
        index = lax.broadcasted_iota(jnp.int32, s.shape, 0)
    vals = []
    for it in range(PEER_TOPK):
        m = jnp.max(s, axis=0, keepdims=True)
        if exact:
            first = jnp.min(jnp.where(s == m, index, jnp.iinfo(jnp.int32).max), axis=0, keepdims=True)
            hit = index == first
        else:
            hit = s == m
        s = jnp.where(hit, RANK_BASE * (1.0 + it / 32.0), s)
        vals.append(m)
    return vals, s


def _taken(marked):
    return jnp.logical_and(marked < 0.5 * RANK_BASE, marked > 2.0 * RANK_BASE)


def _rank_of(marked):
    rank = jnp.floor((marked * (1.0 / RANK_BASE) - 1.0) * 32.0 + 0.5)
    return jnp.where(_taken(marked), rank, NO_RANK)


def _stack_rows(rows, n):
    iota = lax.broadcasted_iota(jnp.int32, (n, rows[0].shape[1]), 0)
    out = jnp.zeros((n, rows[0].shape[1]), F32)
    for i, r in enumerate(rows):
        out = jnp.where(iota == i, r, out)
    return out


def _count(mask):
    return jnp.sum(jnp.where(mask, 1.0, 0.0), axis=0, keepdims=True)


def _route_head(s1, s2, exact):
    tb = s1.shape[1]
    v1, marked1 = _take16(s1, exact)
    v2, marked2 = _take16(s2, exact)
    rank1, rank2 = _rank_of(marked1), _rank_of(marked2)
    v2_all = _stack_rows(v2, PEER_TOPK)
    v1_hi = _stack_rows(v1[SUBLANES:], SUBLANES)
    sub = lax.broadcasted_iota(jnp.int32, (SUBLANES, tb), 0)
    pieces = [v1[0] + v2_all]
    flats = [lax.broadcasted_iota(jnp.int32, (PEER_TOPK, tb), 0)]
    for a in range(1, SUBLANES):
        limit = PEER_TOPK // (a + 1)
        pieces.append(jnp.where(sub < limit, v1[a] + v2_all[:SUBLANES], -jnp.inf))
        flats.append(a * PEER_TOPK + sub)
    pieces.append(v1_hi + v2[0])
    flats.append((sub + SUBLANES) * PEER_TOPK)
    cand = jnp.concatenate(pieces, axis=0)
    flat = jnp.concatenate(flats, axis=0)
    _, marked = _take16(cand, exact, flat)
    sel = jnp.where(_taken(marked), 1.0, 0.0)
    top = v1[0] + v2[0]
    z = jnp.sum(sel * jnp.exp(cand - top), axis=0, keepdims=True)
    cnt = [jnp.sum(sel[:PEER_TOPK], axis=0, keepdims=True)]
    for a in range(1, SUBLANES):
        lo = PEER_TOPK + SUBLANES * (a - 1)
        cnt.append(jnp.sum(sel[lo:lo + SUBLANES], axis=0, keepdims=True))
    lo = PEER_TOPK + SUBLANES * (SUBLANES - 1)
    for i in range(SUBLANES):
        cnt.append(sel[lo + i:lo + i + 1])
    c1 = jnp.zeros(rank1.shape, F32)
    for a in range(PEER_TOPK):
        c1 = jnp.where(rank1 == float(a), cnt[a], c1)
    e1 = jnp.exp(s1 - v1[0])
    e2n = jnp.exp(s2 - v2[0]) / z
    taken = jnp.maximum(jnp.maximum(_count(_taken(marked1)), _count(_taken(marked2))), jnp.sum(sel, axis=0, keepdims=True))
    return c1, e1, rank2, e2n, taken


def _route_one(h, qt_ref, k1_ref, k2_ref, c1_ref, e1_ref, r2_ref, e2_ref, exact):
    half = PEER_QDIM // 2
    q1 = qt_ref[PEER_QDIM * h:PEER_QDIM * h + half, :].astype(BF16)
    q2 = qt_ref[PEER_QDIM * h + half:PEER_QDIM * (h + 1), :].astype(BF16)
    s1 = jnp.dot(k1_ref[h], q1, preferred_element_type=F32)
    s2 = jnp.dot(k2_ref[h], q2, preferred_element_type=F32)
    c1, e1, r2, e2n, taken = _route_head(s1, s2, exact)
    c1_ref[h] = c1
    e1_ref[h] = e1
    r2_ref[h] = r2.astype(r2_ref.dtype)
    e2_ref[h] = e2n.astype(e2_ref.dtype)
    return taken


def _route_kernel(hn_ref, wq_ref, k1_ref, k2_ref, c1_ref, e1_ref, r2_ref, e2_ref, qt_ref):
    qt_ref[...] = lax.dot_general(wq_ref[...], hn_ref[...], (((1,), (1,)), ((), ())), preferred_element_type=F32)
    refs = (qt_ref, k1_ref, k2_ref, c1_ref, e1_ref, r2_ref, e2_ref)
    taken = [_route_one(h, *refs, exact=False) for h in range(PEER_HEADS)]
    for h in range(PEER_HEADS):
        @pl.when(jnp.max(taken[h]) > float(PEER_TOPK))
        def _():
            _route_one(h, *refs, exact=True)


def _peer_route(hn, w_q, keys1, keys2):
    t = hn.shape[0]
    blk = pl.BlockSpec((PEER_HEADS, PEER_KEYS, ROUTE_TILE), lambda i: (0, 0, i))
    fixed3 = lambda i: (0, 0, 0)
    shape = (PEER_HEADS, PEER_KEYS, t)
    return pl.pallas_call(
        _route_kernel,
        grid=(t // ROUTE_TILE,),
        in_specs=[pl.BlockSpec((ROUTE_TILE, D_MODEL), lambda i: (i, 0)),
                  pl.BlockSpec((PEER_HEADS * PEER_QDIM, D_MODEL), lambda i: (0, 0)),
                  pl.BlockSpec((PEER_HEADS, PEER_KEYS, PEER_QDIM // 2), fixed3),
                  pl.BlockSpec((PEER_HEADS, PEER_KEYS, PEER_QDIM // 2), fixed3)],
        out_specs=[blk] * 4,
        out_shape=[jax.ShapeDtypeStruct(shape, F32), jax.ShapeDtypeStruct(shape, F32),
                   jax.ShapeDtypeStruct(shape, BF16), jax.ShapeDtypeStruct(shape, BF16)],
        scratch_shapes=[pltpu.VMEM((PEER_HEADS * PEER_QDIM, ROUTE_TILE), F32)],
        compiler_params=_params(("arbitrary",)),
        name="peer_route",
    )(hn, w_q.T.astype(BF16), keys1.astype(BF16), keys2.astype(BF16))


def _peer_gated(c1_ref, e1_ref, r2_ref, e2_ref, at_ref, chunk, valid):
    packed = 2 * SUBLANES
    tiles = PEER_SUB // PEER_KEYS
    gs = []
    for tl in range(tiles):
        tile = chunk * tiles + tl
        gate = jnp.zeros((PEER_KEYS // packed, packed, PEER_TOKENS), BF16)
        for h in range(PEER_HEADS):
            c1 = jnp.broadcast_to(c1_ref[h, pl.ds(tile, 1), :], (packed, PEER_TOKENS)).astype(BF16)
            e1 = jnp.broadcast_to(e1_ref[h, pl.ds(tile, 1), :], (packed, PEER_TOKENS)).astype(BF16)
            r2 = r2_ref[h].reshape(gate.shape)
            e2 = e2_ref[h].reshape(gate.shape)
            gate = gate + jnp.where(r2 < c1[None], e1[None] * e2, jnp.zeros_like(e2))
        a = at_ref[PEER_KEYS * tl:PEER_KEYS * (tl + 1), :].astype(BF16)
        gs.append(gate.reshape(PEER_KEYS, PEER_TOKENS) * _erf_gelu(a))
    g = jnp.concatenate(gs, axis=0)
    return g if valid is True else jnp.where(valid, g, jnp.zeros_like(g))


def _peer_kernel(hn_ref, u_ref, *refs, n_chunks):
    n = PEER_EXPERTS_STEP // PEER_SUB
    vt_refs, vt_tail_ref, route = refs[:n], refs[n], refs[n + 1:n + 5]
    x1_ref, o_ref, acc_ref, at_ref = refs[n + 5:]
    e = pl.program_id(1)
    last = pl.num_programs(1) - 1
    nt = (((1,), (1,)), ((), ()))

    @pl.when(e == 0)
    def _():
        acc_ref[...] = jnp.zeros_like(acc_ref)
        at_ref[n - 1] = jnp.zeros((PEER_SUB, PEER_TOKENS), F32)

    hn = hn_ref[...]
    total = None
    for c in range(n):
        at_ref[c] = lax.dot_general(u_ref[PEER_SUB * c:PEER_SUB * (c + 1), :], hn, nt, preferred_element_type=F32)
        prev = n * e + c - 1
        g = _peer_gated(*route, at_ref.at[(c - 1) % n], jnp.maximum(prev, 0), (e > 0) if c == 0 else True)
        part = jnp.dot(vt_refs[c][...], g, preferred_element_type=F32)
        total = part if total is None else total + part
    acc_ref[...] += total

    @pl.when(e == last)
    def _():
        g = _peer_gated(*route, at_ref.at[n - 1], n_chunks - 1, True)
        tail = jnp.dot(vt_tail_ref[...], g, preferred_element_type=F32)
        o_ref[...] = x1_ref[...] + (acc_ref[...] + tail).T


def _peer_mix(hn, x1, u_bf, vt, c1, e1, r2, e2n):
    t = hn.shape[0]
    n = PEER_EXPERTS_STEP // PEER_SUB
    n_steps = u_bf.shape[0] // PEER_EXPERTS_STEP
    n_chunks = n * n_steps
    route = pl.BlockSpec((PEER_HEADS, PEER_KEYS, PEER_TOKENS), lambda i, e: (0, 0, i))
    tok = lambda i, e: (i, 0)
    vt_specs = [pl.BlockSpec((D_MODEL, PEER_SUB),
                             functools.partial(lambda i, e, c: (0, jnp.maximum(n * e + c - 1, 0)), c=c))
                for c in range(n)]
    vt_specs.append(pl.BlockSpec((D_MODEL, PEER_SUB), lambda i, e: (0, n_chunks - 1)))
    return pl.pallas_call(
        functools.partial(_peer_kernel, n_chunks=n_chunks),
        grid=(t // PEER_TOKENS, n_steps),
        in_specs=[pl.BlockSpec((PEER_TOKENS, D_MODEL), tok),
                  pl.BlockSpec((PEER_EXPERTS_STEP, D_MODEL), lambda i, e: (e, 0)),
                  *vt_specs,
                  route, route, route, route,
                  pl.BlockSpec((PEER_TOKENS, D_MODEL), tok)],
        out_specs=pl.BlockSpec((PEER_TOKENS, D_MODEL), tok),
        out_shape=jax.ShapeDtypeStruct((t, D_MODEL), F32),
        scratch_shapes=[pltpu.VMEM((D_MODEL, PEER_TOKENS), F32),
                        pltpu.VMEM((n, PEER_SUB, PEER_TOKENS), F32)],
        compiler_params=_params(("arbitrary", "arbitrary")),
        name="peer_mix",
    )(hn, u_bf, *([vt] * (n + 1)), c1, e1, r2, e2n, x1)


def kernel(x, norm_mix_g, w_in, q_norm_g, k_norm_g, rel_bias, ssm_lambda_re, ssm_lambda_im, ssm_log_dt,
           ssm_b_re, ssm_b_im, ssm_c_re, ssm_c_im, ssm_d, ssm_glu_w, ssm_glu_b, attn_out_g, ssm_out_g,
           w_out, norm_ffn_g, peer_w_q, peer_keys1, peer_keys2, peer_u, peer_v):
    b, s, d = x.shape
    x2 = x.reshape(b * s, d)
    q, k, v, u, peer_vt = _in_proj(x2, norm_mix_g, w_in, q_norm_g, k_norm_g, peer_v)
    bias = _bias_tables(rel_bias)
    attn = _attention(q.reshape(b, s, D_ATTN), k.reshape(b, s, D_ATTN), v.reshape(b, s, D_ATTN), bias)
    ssm_n = _s5_mixer(u.reshape(b, s, D_SSM), ssm_lambda_re, ssm_lambda_im, ssm_log_dt, ssm_b_re, ssm_b_im,
                      ssm_c_re, ssm_c_im, ssm_d, ssm_glu_w, ssm_glu_b, ssm_out_g)
    x1, hn, peer_u_bf = _out_proj(attn.reshape(b * s, D_ATTN), ssm_n.reshape(b * s, D_SSM), x2, attn_out_g, w_out,
                                  norm_ffn_g, peer_u)
    c1, e1, r2, e2n = _peer_route(hn, peer_w_q, peer_keys1, peer_keys2)
    out = _peer_mix(hn, x1, peer_u_bf, peer_vt, c1, e1, r2, e2n)
    return out.reshape(b, s, d).astype(x.dtype)
```

```python
import functools
import math

import jax
import jax.numpy as jnp
import numpy as np
from jax import lax
from jax.experimental import pallas as pl
from jax.experimental.pallas import tpu as pltpu

F32 = jnp.float32
BF16 = jnp.bfloat16

D_MODEL = 2048
HEAD_DIM = 64
N_ATTN_HEADS = 16
D_ATTN = N_ATTN_HEADS * HEAD_DIM
SSM_GROUP = 16
N_SSM_GROUPS = 64
D_SSM = N_SSM_GROUPS * SSM_GROUP
SSM_STATE = 64
D_IN_PROJ = 3 * D_ATTN + D_SSM
DILATED_BRANCHES = ((128, 1), (512, 4), (2048, 16))
BLK = 128
N_BUCKETS = 32
MAX_DISTANCE = 2048
PEER_HEADS = 8
PEER_KEYS = 128
PEER_QDIM = 256
PEER_TOPK = 16
EPS = 1e-6
NEG = -1e30
LOG2E = math.log2(math.e)

LANES = 128
SUBLANES = 8
VMEM_LIMIT = 56 * 1024 * 1024

ROW_TILE = 256
SSM_CHUNK = 128
SSM_SLAB_GROUPS = LANES // SSM_GROUP
N_SLABS = N_SSM_GROUPS // SSM_SLAB_GROUPS
SLAB_STATE = SSM_SLAB_GROUPS * SSM_STATE
ATTN_LOOKAHEAD = 2
ROUTE_TILE = 256
PEER_TOKENS = 512
PEER_EXPERTS_STEP = 512
PEER_SUB = 256
NO_RANK = 99.0
RANK_BASE = -2.0 ** 100


def _params(sem, vmem=VMEM_LIMIT):
    return pltpu.CompilerParams(dimension_semantics=sem, vmem_limit_bytes=vmem)


def _erf_gelu(x):
    return 0.5 * x * (1.0 + lax.erf(x * math.sqrt(0.5)))


def _t5_bucket(dist):
    max_exact = N_BUCKETS // 2
    n = np.maximum(dist, 0)
    nf = np.maximum(n, 1).astype(np.float32)
    large = max_exact + (np.log(nf / np.float32(max_exact)) / np.float32(math.log(MAX_DISTANCE / max_exact))
                         * np.float32(N_BUCKETS - max_exact)).astype(np.int32)
    large = np.minimum(large, N_BUCKETS - 1)
    return np.where(n < max_exact, n, large)


def _bias_kernel(bkt_ref, rb_ref, out_ref):
    bkt = bkt_ref[0]
    qi = lax.broadcasted_iota(jnp.int32, (BLK, 2 * BLK), 0)
    kj = lax.broadcasted_iota(jnp.int32, (BLK, 2 * BLK), 1)
    valid = jnp.where(kj < BLK, kj - qi, qi - (kj - BLK)) >= 0
    for h in range(N_ATTN_HEADS):
        acc = jnp.zeros((BLK, 2 * BLK), F32)
        for b in range(N_BUCKETS):
            acc = jnp.where(bkt == b, rb_ref[b * N_ATTN_HEADS + h], acc)
        out_ref[0, h] = jnp.where(valid, acc * LOG2E, NEG)


def _bias_tables(rel_bias):
    qi = np.arange(BLK)[:, None]
    kj = np.arange(2 * BLK)[None, :]
    rel = qi - kj + BLK
    buckets = jnp.asarray(np.stack([_t5_bucket(rel * dil) for _, dil in DILATED_BRANCHES]).astype(np.int32))
    nbr = len(DILATED_BRANCHES)
    return pl.pallas_call(
        _bias_kernel,
        grid=(nbr,),
        in_specs=[pl.BlockSpec((1, BLK, 2 * BLK), lambda i: (i, 0, 0)),
                  pl.BlockSpec(memory_space=pltpu.SMEM)],
        out_specs=pl.BlockSpec((1, N_ATTN_HEADS, BLK, 2 * BLK), lambda i: (i, 0, 0, 0)),
        out_shape=jax.ShapeDtypeStruct((nbr, N_ATTN_HEADS, BLK, 2 * BLK), F32),
        compiler_params=_params(("arbitrary",)),
        name="bias_table",
    )(buckets, rel_bias.astype(F32).reshape(N_BUCKETS * N_ATTN_HEADS))


def _head_rmsnorm(z, gain, ones, scale):
    outs = []
    for c in range(z.shape[1] // LANES):
        zc = z[:, LANES * c:LANES * (c + 1)]
        sq = zc * zc
        hi = sq.astype(BF16)
        lo = (sq - hi.astype(F32)).astype(BF16)
        msq = (jnp.dot(hi, ones, preferred_element_type=F32)
               + jnp.dot(lo, ones, preferred_element_type=F32))
        y = zc * lax.rsqrt(msq + EPS)
        outs.append(y * gain[:, LANES * c:LANES * (c + 1)] * scale)
    return jnp.concatenate(outs, axis=1)


def _inproj_kernel(x_ref, g_ref, w_ref, qg_ref, kg_ref, ones_ref, tab_ref, q_ref, k_ref, v_ref, u_ref, tabo_ref):
    x = x_ref[...]
    ms = jnp.mean(x * x, axis=-1, keepdims=True)
    h = (x * lax.rsqrt(ms + EPS) * g_ref[...]).astype(BF16)
    proj = jnp.dot(h, w_ref[...], preferred_element_type=F32)
    ones = ones_ref[...]
    q_ref[...] = _head_rmsnorm(proj[:, :D_ATTN], qg_ref[...], ones, LOG2E / math.sqrt(HEAD_DIM))
    k_ref[...] = _head_rmsnorm(proj[:, D_ATTN:2 * D_ATTN], kg_ref[...], ones, 1.0)
    v_ref[...] = proj[:, 2 * D_ATTN:3 * D_ATTN]
    u_ref[...] = proj[:, 3 * D_ATTN:]
    tabo_ref[...] = tab_ref[...].T.astype(BF16)


def _in_proj(x2, norm_g, w_in, q_g, k_g, table):
    t = x2.shape[0]
    steps = t // ROW_TILE
    tab_rows = table.shape[0] // steps
    assert tab_rows * steps == table.shape[0]
    head_of_lane = jnp.arange(LANES) // HEAD_DIM
    ones = jnp.where(head_of_lane[:, None] == head_of_lane[None, :], 1.0 / HEAD_DIM, 0.0).astype(BF16)
    qg = jnp.tile(q_g.astype(F32), N_ATTN_HEADS)[None, :]
    kg = jnp.tile(k_g.astype(F32), N_ATTN_HEADS)[None, :]
    row = lambda i: (i, 0)
    fixed = lambda i: (0, 0)
    outs = pl.pallas_call(
        _inproj_kernel,
        grid=(t // ROW_TILE,),
        in_specs=[pl.BlockSpec((ROW_TILE, D_MODEL), row),
                  pl.BlockSpec((1, D_MODEL), fixed),
                  pl.BlockSpec((D_MODEL, D_IN_PROJ), fixed, pipeline_mode=pl.Buffered(1)),
                  pl.BlockSpec((1, D_ATTN), fixed),
                  pl.BlockSpec((1, D_ATTN), fixed),
                  pl.BlockSpec((LANES, LANES), fixed),
                  pl.BlockSpec((tab_rows, D_MODEL), row)],
        out_specs=[pl.BlockSpec((ROW_TILE, D_ATTN), row)] * 3 + [pl.BlockSpec((ROW_TILE, D_SSM), row),
                                                                   pl.BlockSpec((D_MODEL, tab_rows), lambda i: (0, i))],
        out_shape=[jax.ShapeDtypeStruct((t, D_ATTN), F32)] * 3 + [jax.ShapeDtypeStruct((t, D_SSM), F32),
                                                                   jax.ShapeDtypeStruct(table.shape[::-1], BF16)],
        compiler_params=_params(("arbitrary",)),
        name="in_proj",
    )(x2, norm_g.astype(F32)[None, :], w_in.astype(BF16), qg, kg, ones, table)
    return outs


def _rows(start, size, stride):
    return pl.ds(start, size, stride=stride) if stride > 1 else pl.ds(start, size)


def _attn_scores(q_ref, k_ref, v_ref, bias_ref, blk, head0):
    br, dil, n, r = blk
    qrows = _rows(r + dil * BLK * n, BLK, dil)
    qb = q_ref[qrows, :].astype(BF16)
    krows = qrows if n == 0 else _rows(r + dil * BLK * (n - 1), 2 * BLK, dil)
    kb = k_ref[krows, :].astype(BF16)
    vb = v_ref[krows, :].astype(BF16)
    scores = []
    for h in range(2):
        mine = head0 if h == 0 else jnp.logical_not(head0)
        qh = jnp.where(mine, qb, jnp.zeros_like(qb))
        s = lax.dot_general(qh, kb, (((1,), (1,)), ((), ())), preferred_element_type=F32)
        scores.append(s + (bias_ref[br, h, :, BLK:] if n == 0 else bias_ref[br, h]))
    return scores, vb, qrows


def _attn_values(scores, vb, qrows, br, pv_ref, den_ref, m_ref, head0):
    ones = jnp.ones_like(vb)
    pv, mx = [], []
    for h in range(2):
        mine = head0 if h == 0 else jnp.logical_not(head0)
        m = jnp.max(scores[h], axis=-1, keepdims=True)
        p = jnp.exp2(scores[h] - m).astype(BF16)
        pv.append(jnp.dot(p, jnp.where(mine, vb, ones), preferred_element_type=F32))
        mx.append(m)
    pv_ref[br, qrows, :] = jnp.where(head0, pv[0], pv[1])
    den_ref[br, qrows, :] = jnp.where(head0, pv[1], pv[0])
    m_ref[br, qrows, :] = jnp.where(head0, mx[0], mx[1])


def _attn_kernel(q_ref, k_ref, v_ref, bias_ref, tab_ref, o_ref, tabo_ref, pv_ref, den_ref, m_ref, *, seq):
    tabo_ref[...] = tab_ref[...].astype(BF16)
    head0 = lax.broadcasted_iota(jnp.int32, (1, LANES), 1) < HEAD_DIM
    nbr = len(DILATED_BRANCHES)
    blocks = []
    for br, (window, dil) in enumerate(DILATED_BRANCHES):
        assert window // dil == BLK
        blocks += [(br, dil, n, r) for r in range(dil) for n in range(seq // dil // BLK)]
    ahead = [_attn_scores(q_ref, k_ref, v_ref, bias_ref, b, head0) for b in blocks[:ATTN_LOOKAHEAD]]
    for i, blk in enumerate(blocks):
        if i + ATTN_LOOKAHEAD < len(blocks):
            ahead.append(_attn_scores(q_ref, k_ref, v_ref, bias_ref, blocks[i + ATTN_LOOKAHEAD], head0))
        _attn_values(*ahead.pop(0), blk[0], pv_ref, den_ref, m_ref, head0)
    m_all = [m_ref[br] for br in range(nbr)]
    m_top = functools.reduce(jnp.maximum, m_all)
    num = jnp.zeros((seq, LANES), F32)
    den = jnp.zeros((seq, LANES), F32)
    for br in range(nbr):
        w = jnp.exp2(m_all[br] - m_top)
        num = num + w * pv_ref[br]
        den = den + w * pltpu.roll(den_ref[br], HEAD_DIM, axis=1)
    o_ref[...] = num / den


def _attention(q, k, v, bias, table):
    b, s, _ = q.shape
    pairs = D_ATTN // LANES
    tab_rows = table.shape[0] // (b * pairs)
    assert tab_rows * b * pairs == table.shape[0]
    blk = pl.BlockSpec((None, s, LANES), lambda i, p: (i, 0, p))
    tab = pl.BlockSpec((tab_rows, table.shape[1]), lambda i, p: (i * pairs + p, 0))
    nbr = len(DILATED_BRANCHES)
    return pl.pallas_call(
        functools.partial(_attn_kernel, seq=s),
        grid=(b, pairs),
        in_specs=[blk, blk, blk,
                  pl.BlockSpec((nbr, 2, BLK, 2 * BLK), lambda i, p: (0, p, 0, 0)),
                  tab],
        out_specs=[blk, tab],
        out_shape=[jax.ShapeDtypeStruct((b, s, D_ATTN), F32), jax.ShapeDtypeStruct(table.shape, BF16)],
        scratch_shapes=[pltpu.VMEM((nbr, s, LANES), F32)] * 3,
        compiler_params=_params(("arbitrary", "arbitrary")),
        name="attention",
    )(q, k, v, bias, table)


def _zoh_kernel(lr_ref, li_ref, dt_ref, lrr_ref, lir_ref, br_ref, bi_ref,
                are_ref, aim_ref, bbr_ref, bbi_ref):
    dt = jnp.exp(dt_ref[...])

    def zoh(lr, li):
        mag = jnp.exp(lr * dt)
        a_re, a_im = mag * jnp.cos(li * dt), mag * jnp.sin(li * dt)
        den = lr * lr + li * li
        f_re = ((a_re - 1.0) * lr + a_im * li) / den
        f_im = (a_im * lr - (a_re - 1.0) * li) / den
        return a_re, a_im, f_re, f_im

    a_re, a_im, _, _ = zoh(lr_ref[...], li_ref[...])
    are_ref[...] = a_re
    aim_ref[...] = a_im
    _, _, f_re, f_im = zoh(lrr_ref[...], lir_ref[...])
    br, bi = br_ref[...], bi_ref[...]
    bbr_ref[...] = f_re * br - f_im * bi
    bbi_ref[...] = f_re * bi + f_im * br


def _ssm_zoh(lam_re, lam_im, log_dt, b_re, b_im):
    g, n, c = b_re.shape
    rep = lambda a: jnp.repeat(a.astype(F32), c, axis=1)
    a_re, a_im, bb_re, bb_im = pl.pallas_call(
        _zoh_kernel,
        out_shape=[jax.ShapeDtypeStruct((g, n), F32)] * 2 + [jax.ShapeDtypeStruct((g, n * c), F32)] * 2,
        name="ssm_zoh",
    )(lam_re.astype(F32), lam_im.astype(F32), log_dt.astype(F32)[:, None], rep(lam_re), rep(lam_im),
      b_re.astype(F32).reshape(g, n * c), b_im.astype(F32).reshape(g, n * c))
    return a_re, a_im, bb_re.reshape(g, n, c), bb_im.reshape(g, n, c)


def _ssm_kernel(u_ref, wb_ref, wc_ref, are_ref, aim_ref, d_ref, gw_ref, gb_ref, gain_ref,
                o_ref, lhs_ref, bu_ref, ysel_ref, y_ref, sr_ref, si_ref, *, nb, chunk):
    half = N_SLABS // 2
    seqs = 2 * nb
    lane_blocks = 2 * SLAB_STATE // LANES
    rows_all = seqs * chunk

    @pl.when(pl.program_id(0) == 0)
    def _():
        sr_ref[...] = jnp.zeros_like(sr_ref)
        si_ref[...] = jnp.zeros_like(si_ref)
        lhs_ref[...] = jnp.zeros_like(lhs_ref)

    for b in range(nb):
        for m in range(N_SLABS):
            gh, mp = divmod(m, half)
            lhs_ref[2 * mp + gh, pl.ds(gh * nb + b, chunk, stride=seqs), :] = u_ref[b, :, LANES * m:LANES * (m + 1)]

    for mp in range(half):
        lhs = jnp.concatenate([lhs_ref[2 * mp], lhs_ref[2 * mp + 1]], axis=1).astype(BF16)
        bu = jnp.dot(lhs, wb_ref[mp], preferred_element_type=F32)
        for j in range(lane_blocks):
            bu_ref[lane_blocks * mp + j] = bu[:, LANES * j:LANES * (j + 1)]

    def load_state(rows, mp, part):
        j0 = lane_blocks * mp + part * (lane_blocks // 2)
        return jnp.concatenate([bu_ref[j0 + j, rows, :] for j in range(lane_blocks // 2)], axis=1)

    def store_state(rows, mp, part, val):
        j0 = lane_blocks * mp + part * (lane_blocks // 2)
        for j in range(lane_blocks // 2):
            bu_ref[j0 + j, rows, :] = val[:, LANES * j:LANES * (j + 1)]

    def step(t, carry):
        base = pl.multiple_of(t * seqs, seqs)
        rows = pl.ds(base, seqs)
        new = []
        for mp in range(half):
            xr, xi = carry[2 * mp], carry[2 * mp + 1]
            ar = are_ref[:, SLAB_STATE * mp:SLAB_STATE * (mp + 1)]
            ai = aim_ref[:, SLAB_STATE * mp:SLAB_STATE * (mp + 1)]
            nr = ar * xr - ai * xi + load_state(rows, mp, 0)
            ni = ar * xi + ai * xr + load_state(rows, mp, 1)
            store_state(rows, mp, 0, nr)
            store_state(rows, mp, 1, ni)
            new += [nr, ni]
        return tuple(new)

    init = []
    for mp in range(half):
        init += [sr_ref[:, SLAB_STATE * mp:SLAB_STATE * (mp + 1)], si_ref[:, SLAB_STATE * mp:SLAB_STATE * (mp + 1)]]
    final = lax.fori_loop(0, chunk, step, tuple(init), unroll=2)
    for mp in range(half):
        sr_ref[:, SLAB_STATE * mp:SLAB_STATE * (mp + 1)] = final[2 * mp]
        si_ref[:, SLAB_STATE * mp:SLAB_STATE * (mp + 1)] = final[2 * mp + 1]

    first_half = (lax.broadcasted_iota(jnp.int32, (rows_all, LANES), 0) & nb) == 0
    for mp in range(half):
        xs = jnp.concatenate([bu_ref[lane_blocks * mp + j] for j in range(lane_blocks)], axis=1).astype(BF16)
        yy = jnp.dot(xs, wc_ref[mp], preferred_element_type=F32)
        ysel_ref[mp] = jnp.where(first_half, yy[:, :LANES], yy[:, LANES:])
    for b in range(nb):
        for m in range(N_SLABS):
            gh, mp = divmod(m, half)
            cols = slice(LANES * m, LANES * (m + 1))
            y = ysel_ref[mp, pl.ds(gh * nb + b, chunk, stride=seqs), :]
            y_ref[b * chunk:(b + 1) * chunk, cols] = y + d_ref[:, cols] * u_ref[b, :, cols]

    y = _erf_gelu(y_ref[...])
    z = jnp.dot(y.astype(BF16), gw_ref[...], preferred_element_type=F32) + gb_ref[...]
    y = y * jax.nn.sigmoid(z)
    ms = jnp.mean(y * y, axis=-1, keepdims=True)
    yn = y * lax.rsqrt(ms + EPS) * gain_ref[...]
    for b in range(nb):
        o_ref[b] = yn[b * chunk:(b + 1) * chunk].astype(o_ref.dtype)


def _s5_mixer(u, lam_re, lam_im, log_dt, b_re, b_im, c_re, c_im, d_skip, glu_w, glu_b, out_gain):
    nb, s, _ = u.shape
    a_re, a_im, bb_re, bb_im = _ssm_zoh(lam_re, lam_im, log_dt, b_re, b_im)
    eye = jnp.eye(SSM_SLAB_GROUPS, dtype=F32)

    def in_slab(bb):
        w = jnp.einsum('mgnc,gh->mgchn', bb.reshape(N_SLABS, SSM_SLAB_GROUPS, SSM_STATE, SSM_GROUP), eye)
        return w.reshape(N_SLABS, LANES, SLAB_STATE)

    def out_slab(cc):
        w = jnp.einsum('mgcn,gh->mgnhc', cc.reshape(N_SLABS, SSM_SLAB_GROUPS, SSM_GROUP, SSM_STATE), eye)
        return w.reshape(N_SLABS, SLAB_STATE, LANES)

    wb = jnp.concatenate([in_slab(bb_re), in_slab(bb_im)], axis=2)
    wc = jnp.concatenate([out_slab(c_re.astype(F32)), -out_slab(c_im.astype(F32))], axis=1)
    hs = N_SLABS // 2
    wb = jnp.concatenate([wb[:hs], wb[hs:]], axis=1).astype(BF16)
    wc = jnp.concatenate([wc[:hs], wc[hs:]], axis=2).astype(BF16)
    half_states = (N_SSM_GROUPS // 2) * SSM_STATE

    def seq_rows(a):
        return jnp.repeat(a.reshape(2, half_states), nb, axis=0)

    fixed2 = lambda c: (0, 0)
    fixed3 = lambda c: (0, 0, 0)
    chunk = SSM_CHUNK
    return pl.pallas_call(
        functools.partial(_ssm_kernel, nb=nb, chunk=chunk),
        grid=(s // chunk,),
        in_specs=[pl.BlockSpec((nb, chunk, D_SSM), lambda c: (0, c, 0)),
                  pl.BlockSpec((N_SLABS // 2, 2 * LANES, 2 * SLAB_STATE), fixed3),
                  pl.BlockSpec((N_SLABS // 2, 2 * SLAB_STATE, 2 * LANES), fixed3),
                  pl.BlockSpec((2 * nb, half_states), fixed2),
                  pl.BlockSpec((2 * nb, half_states), fixed2),
                  pl.BlockSpec((1, D_SSM), fixed2),
                  pl.BlockSpec((D_SSM, D_SSM), fixed2),
                  pl.BlockSpec((1, D_SSM), fixed2),
                  pl.BlockSpec((1, D_SSM), fixed2)],
        out_specs=pl.BlockSpec((nb, chunk, D_SSM), lambda c: (0, c, 0)),
        out_shape=jax.ShapeDtypeStruct((nb, s, D_SSM), BF16),
        scratch_shapes=[pltpu.VMEM((N_SLABS, 2 * nb * chunk, LANES), F32),
                        pltpu.VMEM((2 * half_states // LANES, 2 * nb * chunk, LANES), F32),
                        pltpu.VMEM((N_SLABS // 2, 2 * nb * chunk, LANES), F32),
                        pltpu.VMEM((nb * chunk, D_SSM), F32),
                        pltpu.VMEM((2 * nb, half_states), F32),
                        pltpu.VMEM((2 * nb, half_states), F32)],
        compiler_params=_params(("arbitrary",)),
        name="ssm",
    )(u, wb, wc, seq_rows(a_re), seq_rows(a_im), d_skip.astype(F32).reshape(1, D_SSM),
      glu_w.astype(BF16), glu_b.astype(F32)[None, :], out_gain.astype(F32)[None, :])


def _outproj_kernel(a_ref, s_ref, x_ref, ag_ref, wa_ref, ws_ref, fg_ref, x1_ref, hn_ref):
    a = a_ref[...]
    ms = jnp.mean(a * a, axis=-1, keepdims=True)
    an = (a * lax.rsqrt(ms + EPS) * ag_ref[...]).astype(BF16)
    mixed = (jnp.dot(an, wa_ref[...], preferred_element_type=F32)
             + jnp.dot(s_ref[...], ws_ref[...], preferred_element_type=F32))
    x1 = x_ref[...] + mixed
    x1_ref[...] = x1
    ms1 = jnp.mean(x1 * x1, axis=-1, keepdims=True)
    hn_ref[...] = (x1 * lax.rsqrt(ms1 + EPS) * fg_ref[...]).astype(BF16)


def _out_proj(attn, ssm_n, x2, attn_g, w_out, ffn_g):
    t = x2.shape[0]
    row = lambda i: (i, 0)
    fixed = lambda i: (0, 0)
    w = w_out.astype(BF16)
    return pl.pallas_call(
        _outproj_kernel,
        grid=(t // ROW_TILE,),
        in_specs=[pl.BlockSpec((ROW_TILE, D_ATTN), row),
                  pl.BlockSpec((ROW_TILE, D_SSM), row),
                  pl.BlockSpec((ROW_TILE, D_MODEL), row),
                  pl.BlockSpec((1, D_ATTN), fixed),
                  pl.BlockSpec((D_ATTN, D_MODEL), fixed),
                  pl.BlockSpec((D_SSM, D_MODEL), fixed),
                  pl.BlockSpec((1, D_MODEL), fixed)],
        out_specs=[pl.BlockSpec((ROW_TILE, D_MODEL), row)] * 2,
        out_shape=[jax.ShapeDtypeStruct((t, D_MODEL), F32), jax.ShapeDtypeStruct((t, D_MODEL), BF16)],
        compiler_params=_params(("arbitrary",)),
        name="out_proj",
    )(attn, ssm_n, x2, attn_g.astype(F32)[None, :], w[:D_ATTN], w[D_ATTN:], ffn_g.astype(F32)[None, :])


def _take16(s, exact, index=None):
    if index is None:
        index = lax.broadcasted_iota(jnp.int32, s.shape, 0)
    vals = []
    for it in range(PEER_TOPK):
        m = jnp.max(s, axis=0, keepdims=True)
        if exact:
            first = jnp.min(jnp.where(s == m, index, jnp.iinfo(jnp.int32).max), axis=0, keepdims=True)
            hit = index == first
        else:
            hit = s == m
        s = jnp.where(hit, RANK_BASE * (1.0 + it / 32.0), s)
        vals.append(m)
    return vals, s


def _taken(marked):
    return jnp.logical_and(marked < 0.5 * RANK_BASE, marked > 2.0 * RANK_BASE)


def _rank_of(marked):
    rank = jnp.floor((marked * (1.0 / RANK_BASE) - 1.0) * 32.0 + 0.5)
    return jnp.where(_taken(marked), rank, NO_RANK)


def _stack_rows(rows, n):
    iota = lax.broadcasted_iota(jnp.int32, (n, rows[0].shape[1]), 0)
    out = jnp.zeros((n, rows[0].shape[1]), F32)
    for i, r in enumerate(rows):
        out = jnp.where(iota == i, r, out)
    return out


def _count(mask):
    return jnp.sum(jnp.where(mask, 1.0, 0.0), axis=0, keepdims=True)


def _route_head(s1, s2, exact):
    tb = s1.shape[1]
    v1, marked1 = _take16(s1, exact)
    v2, marked2 = _take16(s2, exact)
    rank1, rank2 = _rank_of(marked1), _rank_of(marked2)
    v2_all = _stack_rows(v2, PEER_TOPK)
    v1_hi = _stack_rows(v1[SUBLANES:], SUBLANES)
    sub = lax.broadcasted_iota(jnp.int32, (SUBLANES, tb), 0)
    pieces = [v1[0] + v2_all]
    flats = [lax.broadcasted_iota(jnp.int32, (PEER_TOPK, tb), 0)]
    for a in range(1, SUBLANES):
        limit = PEER_TOPK // (a + 1)
        pieces.append(jnp.where(sub < limit, v1[a] + v2_all[:SUBLANES], -jnp.inf))
        flats.append(a * PEER_TOPK + sub)
    pieces.append(v1_hi + v2[0])
    flats.append((sub + SUBLANES) * PEER_TOPK)
    cand = jnp.concatenate(pieces, axis=0)
    flat = jnp.concatenate(flats, axis=0)
    _, marked = _take16(cand, exact, flat)
    sel = jnp.where(_taken(marked), 1.0, 0.0)
    top = v1[0] + v2[0]
    z = jnp.sum(sel * jnp.exp(cand - top), axis=0, keepdims=True)
    cnt = [jnp.sum(sel[:PEER_TOPK], axis=0, keepdims=True)]
    for a in range(1, SUBLANES):
        lo = PEER_TOPK + SUBLANES * (a - 1)
        cnt.append(jnp.sum(sel[lo:lo + SUBLANES], axis=0, keepdims=True))
    lo = PEER_TOPK + SUBLANES * (SUBLANES - 1)
    for i in range(SUBLANES):
        cnt.append(sel[lo + i:lo + i + 1])
    c1 = jnp.zeros(rank1.shape, F32)
    for a in range(PEER_TOPK):
        c1 = jnp.where(rank1 == float(a), cnt[a], c1)
    e1 = jnp.exp(s1 - v1[0])
    e2n = jnp.exp(s2 - v2[0]) / z
    taken = jnp.maximum(jnp.maximum(_count(_taken(marked1)), _count(_taken(marked2))), jnp.sum(sel, axis=0, keepdims=True))
    return c1, e1, rank2, e2n, taken


def _route_one(h, qt_ref, k1_ref, k2_ref, c1_ref, e1_ref, r2_ref, e2_ref, exact):
    half = PEER_QDIM // 2
    q1 = qt_ref[PEER_QDIM * h:PEER_QDIM * h + half, :].astype(BF16)
    q2 = qt_ref[PEER_QDIM * h + half:PEER_QDIM * (h + 1), :].astype(BF16)
    s1 = jnp.dot(k1_ref[h], q1, preferred_element_type=F32)
    s2 = jnp.dot(k2_ref[h], q2, preferred_element_type=F32)
    c1, e1, r2, e2n, taken = _route_head(s1, s2, exact)
    c1_ref[h] = c1
    e1_ref[h] = e1
    r2_ref[h] = r2.astype(r2_ref.dtype)
    e2_ref[h] = e2n.astype(e2_ref.dtype)
    return taken


def _route_kernel(hn_ref, wq_ref, k1_ref, k2_ref, c1_ref, e1_ref, r2_ref, e2_ref, qt_ref):
    qt_ref[...] = lax.dot_general(wq_ref[...], hn_ref[...], (((1,), (1,)), ((), ())), preferred_element_type=F32)
    refs = (qt_ref, k1_ref, k2_ref, c1_ref, e1_ref, r2_ref, e2_ref)
    taken = [_route_one(h, *refs, exact=False) for h in range(PEER_HEADS)]
    for h in range(PEER_HEADS):
        @pl.when(jnp.max(taken[h]) > float(PEER_TOPK))
        def _():
            _route_one(h, *refs, exact=True)


def _peer_route(hn, w_q, keys1, keys2):
    t = hn.shape[0]
    blk = pl.BlockSpec((PEER_HEADS, PEER_KEYS, ROUTE_TILE), lambda i: (0, 0, i))
    fixed3 = lambda i: (0, 0, 0)
    shape = (PEER_HEADS, PEER_KEYS, t)
    return pl.pallas_call(
        _route_kernel,
        grid=(t // ROUTE_TILE,),
        in_specs=[pl.BlockSpec((ROUTE_TILE, D_MODEL), lambda i: (i, 0)),
                  pl.BlockSpec((PEER_HEADS * PEER_QDIM, D_MODEL), lambda i: (0, 0)),
                  pl.BlockSpec((PEER_HEADS, PEER_KEYS, PEER_QDIM // 2), fixed3),
                  pl.BlockSpec((PEER_HEADS, PEER_KEYS, PEER_QDIM // 2), fixed3)],
        out_specs=[blk] * 4,
        out_shape=[jax.ShapeDtypeStruct(shape, F32), jax.ShapeDtypeStruct(shape, F32),
                   jax.ShapeDtypeStruct(shape, BF16), jax.ShapeDtypeStruct(shape, BF16)],
        scratch_shapes=[pltpu.VMEM((PEER_HEADS * PEER_QDIM, ROUTE_TILE), F32)],
        compiler_params=_params(("arbitrary",)),
        name="peer_route",
    )(hn, w_q.T.astype(BF16), keys1.astype(BF16), keys2.astype(BF16))


def _peer_gated(c1_ref, e1_ref, r2_ref, e2_ref, at_ref, chunk, valid):
    packed = 2 * SUBLANES
    tiles = PEER_SUB // PEER_KEYS
    gs = []
    for tl in range(tiles):
        tile = chunk * tiles + tl
        gate = jnp.zeros((PEER_KEYS // packed, packed, PEER_TOKENS), BF16)
        for h in range(PEER_HEADS):
            c1 = jnp.broadcast_to(c1_ref[h, pl.ds(tile, 1), :], (packed, PEER_TOKENS)).astype(BF16)
            e1 = jnp.broadcast_to(e1_ref[h, pl.ds(tile, 1), :], (packed, PEER_TOKENS)).astype(BF16)
            r2 = r2_ref[h].reshape(gate.shape)
            e2 = e2_ref[h].reshape(gate.shape)
            gate = gate + jnp.where(r2 < c1[None], e1[None] * e2, jnp.zeros_like(e2))
        a = at_ref[PEER_KEYS * tl:PEER_KEYS * (tl + 1), :].astype(BF16)
        gs.append(gate.reshape(PEER_KEYS, PEER_TOKENS) * _erf_gelu(a))
    g = jnp.concatenate(gs, axis=0)
    return g if valid is True else jnp.where(valid, g, jnp.zeros_like(g))


def _peer_kernel(hn_ref, u_ref, *refs, n_chunks):
    n = PEER_EXPERTS_STEP // PEER_SUB
    vt_refs, vt_tail_ref, route = refs[:n], refs[n], refs[n + 1:n + 5]
    x1_ref, o_ref, acc_ref, at_ref = refs[n + 5:]
    e = pl.program_id(1)
    last = pl.num_programs(1) - 1
    nt = (((1,), (1,)), ((), ()))

    @pl.when(e == 0)
    def _():
        acc_ref[...] = jnp.zeros_like(acc_ref)
        at_ref[n - 1] = jnp.zeros((PEER_SUB, PEER_TOKENS), F32)

    hn = hn_ref[...]
    total = None
    for c in range(n):
        at_ref[c] = lax.dot_general(u_ref[PEER_SUB * c:PEER_SUB * (c + 1), :], hn, nt, preferred_element_type=F32)
        prev = n * e + c - 1
        g = _peer_gated(*route, at_ref.at[(c - 1) % n], jnp.maximum(prev, 0), (e > 0) if c == 0 else True)
        part = jnp.dot(vt_refs[c][...], g, preferred_element_type=F32)
        total = part if total is None else total + part
    acc_ref[...] += total

    @pl.when(e == last)
    def _():
        g = _peer_gated(*route, at_ref.at[n - 1], n_chunks - 1, True)
        tail = jnp.dot(vt_tail_ref[...], g, preferred_element_type=F32)
        o_ref[...] = x1_ref[...] + (acc_ref[...] + tail).T


def _peer_mix(hn, x1, u_bf, vt, c1, e1, r2, e2n):
    t = hn.shape[0]
    n = PEER_EXPERTS_STEP // PEER_SUB
    n_steps = u_bf.shape[0] // PEER_EXPERTS_STEP
    n_chunks = n * n_steps
    route = pl.BlockSpec((PEER_HEADS, PEER_KEYS, PEER_TOKENS), lambda i, e: (0, 0, i))
    tok = lambda i, e: (i, 0)
    vt_specs = [pl.BlockSpec((D_MODEL, PEER_SUB),
                             functools.partial(lambda i, e, c: (0, jnp.maximum(n * e + c - 1, 0)), c=c))
                for c in range(n)]
    vt_specs.append(pl.BlockSpec((D_MODEL, PEER_SUB), lambda i, e: (0, n_chunks - 1)))
    return pl.pallas_call(
        functools.partial(_peer_kernel, n_chunks=n_chunks),
        grid=(t // PEER_TOKENS, n_steps),
        in_specs=[pl.BlockSpec((PEER_TOKENS, D_MODEL), tok),
                  pl.BlockSpec((PEER_EXPERTS_STEP, D_MODEL), lambda i, e: (e, 0)),
                  *vt_specs,
                  route, route, route, route,
                  pl.BlockSpec((PEER_TOKENS, D_MODEL), tok)],
        out_specs=pl.BlockSpec((PEER_TOKENS, D_MODEL), tok),
        out_shape=jax.ShapeDtypeStruct((t, D_MODEL), F32),
        scratch_shapes=[pltpu.VMEM((D_MODEL, PEER_TOKENS), F32),
                        pltpu.VMEM((n, PEER_SUB, PEER_TOKENS), F32)],
        compiler_params=_params(("arbitrary", "arbitrary")),
        name="peer_mix",
    )(hn, u_bf, *([vt] * (n + 1)), c1, e1, r2, e2n, x1)


def kernel(x, norm_mix_g, w_in, q_norm_g, k_norm_g, rel_bias, ssm_lambda_re, ssm_lambda_im, ssm_log_dt,
           ssm_b_re, ssm_b_im, ssm_c_re, ssm_c_im, ssm_d, ssm_glu_w, ssm_glu_b, attn_out_g, ssm_out_g,
           w_out, norm_ffn_g, peer_w_q, peer_keys1, peer_keys2, peer_u, peer_v):
    b, s, d = x.shape
    x2 = x.reshape(b * s, d)
    q, k, v, u, peer_vt = _in_proj(x2, norm_mix_g, w_in, q_norm_g, k_norm_g, peer_v)
    bias = _bias_tables(rel_bias)
    attn, peer_u_bf = _attention(q.reshape(b, s, D_ATTN), k.reshape(b, s, D_ATTN), v.reshape(b, s, D_ATTN), bias,
                                 peer_u)
    ssm_n = _s5_mixer(u.reshape(b, s, D_SSM), ssm_lambda_re, ssm_lambda_im, ssm_log_dt, ssm_b_re, ssm_b_im,
                      ssm_c_re, ssm_c_im, ssm_d, ssm_glu_w, ssm_glu_b, ssm_out_g)
    x1, hn = _out_proj(attn.reshape(b * s, D_ATTN), ssm_n.reshape(b * s, D_SSM), x2, attn_out_g, w_out, norm_ffn_g)
    c1, e1, r2, e2n = _peer_route(hn, peer_w_q, peer_keys1, peer_keys2)
    out = _peer_mix(hn, x1, peer_u_bf, peer_vt, c1, e1, r2, e2n)
    return out.reshape(b, s, d).astype(x.dtype)
```

```python
import functools
import math

import jax
import jax.numpy as jnp
import numpy as np
from jax import lax
from jax.experimental import pallas as pl
from jax.experimental.pallas import tpu as pltpu

F32 = jnp.float32
BF16 = jnp.bfloat16

D_MODEL = 2048
HEAD_DIM = 64
N_ATTN_HEADS = 16
D_ATTN = N_ATTN_HEADS * HEAD_DIM
SSM_GROUP = 16
N_SSM_GROUPS = 64
D_SSM = N_SSM_GROUPS * SSM_GROUP
SSM_STATE = 64
D_IN_PROJ = 3 * D_ATTN + D_SSM
DILATED_BRANCHES = ((128, 1), (512, 4), (2048, 16))
BLK = 128
N_BUCKETS = 32
MAX_DISTANCE = 2048
PEER_HEADS = 8
PEER_KEYS = 128
PEER_QDIM = 256
PEER_TOPK = 16
EPS = 1e-6
NEG = -1e30
LOG2E = math.log2(math.e)

LANES = 128
SUBLANES = 8
VMEM_LIMIT = 56 * 1024 * 1024

ROW_TILE = 256
SSM_CHUNK = 128
SSM_SLAB_GROUPS = LANES // SSM_GROUP
N_SLABS = N_SSM_GROUPS // SSM_SLAB_GROUPS
SLAB_STATE = SSM_SLAB_GROUPS * SSM_STATE
ATTN_LOOKAHEAD = 2
ROUTE_TILE = 256
PEER_TOKENS = 512
PEER_EXPERTS_STEP = 512
PEER_SUB = 256
NO_RANK = 99.0
RANK_BASE = -2.0 ** 100


def _params(sem, vmem=VMEM_LIMIT):
    return pltpu.CompilerParams(dimension_semantics=sem, vmem_limit_bytes=vmem)


def _erf_gelu(x):
    return 0.5 * x * (1.0 + lax.erf(x * math.sqrt(0.5)))


def _t5_bucket(dist):
    max_exact = N_BUCKETS // 2
    n = np.maximum(dist, 0)
    nf = np.maximum(n, 1).astype(np.float32)
    large = max_exact + (np.log(nf / np.float32(max_exact)) / np.float32(math.log(MAX_DISTANCE / max_exact))
                         * np.float32(N_BUCKETS - max_exact)).astype(np.int32)
    large = np.minimum(large, N_BUCKETS - 1)
    return np.where(n < max_exact, n, large)


def _bias_kernel(bkt_ref, rb_ref, w_ref, out_ref, wo_ref):
    wo_ref[...] = w_ref[...].astype(BF16)
    bkt = bkt_ref[0]
    qi = lax.broadcasted_iota(jnp.int32, (BLK, 2 * BLK), 0)
    kj = lax.broadcasted_iota(jnp.int32, (BLK, 2 * BLK), 1)
    valid = jnp.where(kj < BLK, kj - qi, qi - (kj - BLK)) >= 0
    for h in range(N_ATTN_HEADS):
        acc = jnp.zeros((BLK, 2 * BLK), F32)
        for b in range(N_BUCKETS):
            acc = jnp.where(bkt == b, rb_ref[b * N_ATTN_HEADS + h], acc)
        out_ref[0, h] = jnp.where(valid, acc * LOG2E, NEG)


def _bias_tables(rel_bias, w):
    qi = np.arange(BLK)[:, None]
    kj = np.arange(2 * BLK)[None, :]
    rel = qi - kj + BLK
    buckets = jnp.asarray(np.stack([_t5_bucket(rel * dil) for _, dil in DILATED_BRANCHES]).astype(np.int32))
    nbr = len(DILATED_BRANCHES)
    steps = nbr + 1
    w_rows = w.shape[0] // steps
    assert w_rows * steps == w.shape[0]
    branch = lambda i: jnp.minimum(i, nbr - 1)
    return pl.pallas_call(
        _bias_kernel,
        grid=(steps,),
        in_specs=[pl.BlockSpec((1, BLK, 2 * BLK), lambda i: (branch(i), 0, 0)),
                  pl.BlockSpec(memory_space=pltpu.SMEM),
                  pl.BlockSpec((w_rows, w.shape[1]), lambda i: (i, 0))],
        out_specs=[pl.BlockSpec((1, N_ATTN_HEADS, BLK, 2 * BLK), lambda i: (branch(i), 0, 0, 0)),
                   pl.BlockSpec((w_rows, w.shape[1]), lambda i: (i, 0))],
        out_shape=[jax.ShapeDtypeStruct((nbr, N_ATTN_HEADS, BLK, 2 * BLK), F32),
                   jax.ShapeDtypeStruct(w.shape, BF16)],
        compiler_params=_params(("arbitrary",)),
        name="bias_table",
    )(buckets, rel_bias.astype(F32).reshape(N_BUCKETS * N_ATTN_HEADS), w)


def _head_rmsnorm(z, gain, ones, scale):
    outs = []
    for c in range(z.shape[1] // LANES):
        zc = z[:, LANES * c:LANES * (c + 1)]
        sq = zc * zc
        hi = sq.astype(BF16)
        lo = (sq - hi.astype(F32)).astype(BF16)
        msq = (jnp.dot(hi, ones, preferred_element_type=F32)
               + jnp.dot(lo, ones, preferred_element_type=F32))
        y = zc * lax.rsqrt(msq + EPS)
        outs.append(y * gain[:, LANES * c:LANES * (c + 1)] * scale)
    return jnp.concatenate(outs, axis=1)


def _inproj_kernel(x_ref, g_ref, w_ref, qg_ref, kg_ref, ones_ref, tab_ref, q_ref, k_ref, v_ref, u_ref, tabo_ref):
    x = x_ref[...]
    ms = jnp.mean(x * x, axis=-1, keepdims=True)
    h = (x * lax.rsqrt(ms + EPS) * g_ref[...]).astype(BF16)
    proj = jnp.dot(h, w_ref[...], preferred_element_type=F32)
    ones = ones_ref[...]
    q_ref[...] = _head_rmsnorm(proj[:, :D_ATTN], qg_ref[...], ones, LOG2E / math.sqrt(HEAD_DIM))
    k_ref[...] = _head_rmsnorm(proj[:, D_ATTN:2 * D_ATTN], kg_ref[...], ones, 1.0)
    v_ref[...] = proj[:, 2 * D_ATTN:3 * D_ATTN]
    u_ref[...] = proj[:, 3 * D_ATTN:]
    tabo_ref[...] = tab_ref[...].T.astype(BF16)


def _in_proj(x2, norm_g, w_in, q_g, k_g, table):
    t = x2.shape[0]
    steps = t // ROW_TILE
    tab_rows = table.shape[0] // steps
    assert tab_rows * steps == table.shape[0]
    head_of_lane = jnp.arange(LANES) // HEAD_DIM
    ones = jnp.where(head_of_lane[:, None] == head_of_lane[None, :], 1.0 / HEAD_DIM, 0.0).astype(BF16)
    qg = jnp.tile(q_g.astype(F32), N_ATTN_HEADS)[None, :]
    kg = jnp.tile(k_g.astype(F32), N_ATTN_HEADS)[None, :]
    row = lambda i: (i, 0)
    fixed = lambda i: (0, 0)
    outs = pl.pallas_call(
        _inproj_kernel,
        grid=(t // ROW_TILE,),
        in_specs=[pl.BlockSpec((ROW_TILE, D_MODEL), row),
                  pl.BlockSpec((1, D_MODEL), fixed),
                  pl.BlockSpec((D_MODEL, D_IN_PROJ), fixed, pipeline_mode=pl.Buffered(1)),
                  pl.BlockSpec((1, D_ATTN), fixed),
                  pl.BlockSpec((1, D_ATTN), fixed),
                  pl.BlockSpec((LANES, LANES), fixed),
                  pl.BlockSpec((tab_rows, D_MODEL), row)],
        out_specs=[pl.BlockSpec((ROW_TILE, D_ATTN), row)] * 3 + [pl.BlockSpec((ROW_TILE, D_SSM), row),
                                                                   pl.BlockSpec((D_MODEL, tab_rows), lambda i: (0, i))],
        out_shape=[jax.ShapeDtypeStruct((t, D_ATTN), F32)] * 3 + [jax.ShapeDtypeStruct((t, D_SSM), F32),
                                                                   jax.ShapeDtypeStruct(table.shape[::-1], BF16)],
        compiler_params=_params(("arbitrary",)),
        name="in_proj",
    )(x2, norm_g.astype(F32)[None, :], w_in.astype(BF16), qg, kg, ones, table)
    return outs


def _rows(start, size, stride):
    return pl.ds(start, size, stride=stride) if stride > 1 else pl.ds(start, size)


def _attn_scores(q_ref, k_ref, v_ref, bias_ref, blk, head0):
    br, dil, n, r = blk
    qrows = _rows(r + dil * BLK * n, BLK, dil)
    qb = q_ref[qrows, :].astype(BF16)
    krows = qrows if n == 0 else _rows(r + dil * BLK * (n - 1), 2 * BLK, dil)
    kb = k_ref[krows, :].astype(BF16)
    vb = v_ref[krows, :].astype(BF16)
    scores = []
    for h in range(2):
        mine = head0 if h == 0 else jnp.logical_not(head0)
        qh = jnp.where(mine, qb, jnp.zeros_like(qb))
        s = lax.dot_general(qh, kb, (((1,), (1,)), ((), ())), preferred_element_type=F32)
        scores.append(s + (bias_ref[br, h, :, BLK:] if n == 0 else bias_ref[br, h]))
    return scores, vb, qrows


def _attn_values(scores, vb, qrows, br, pv_ref, den_ref, m_ref, head0):
    ones = jnp.ones_like(vb)
    pv, mx = [], []
    for h in range(2):
        mine = head0 if h == 0 else jnp.logical_not(head0)
        m = jnp.max(scores[h], axis=-1, keepdims=True)
        p = jnp.exp2(scores[h] - m).astype(BF16)
        pv.append(jnp.dot(p, jnp.where(mine, vb, ones), preferred_element_type=F32))
        mx.append(m)
    pv_ref[br, qrows, :] = jnp.where(head0, pv[0], pv[1])
    den_ref[br, qrows, :] = jnp.where(head0, pv[1], pv[0])
    m_ref[br, qrows, :] = jnp.where(head0, mx[0], mx[1])


def _attn_kernel(q_ref, k_ref, v_ref, bias_ref, tab_ref, o_ref, tabo_ref, pv_ref, den_ref, m_ref, *, seq):
    tabo_ref[...] = tab_ref[...].astype(BF16)
    head0 = lax.broadcasted_iota(jnp.int32, (1, LANES), 1) < HEAD_DIM
    nbr = len(DILATED_BRANCHES)
    blocks = []
    for br, (window, dil) in enumerate(DILATED_BRANCHES):
        assert window // dil == BLK
        blocks += [(br, dil, n, r) for r in range(dil) for n in range(seq // dil // BLK)]
    ahead = [_attn_scores(q_ref, k_ref, v_ref, bias_ref, b, head0) for b in blocks[:ATTN_LOOKAHEAD]]
    for i, blk in enumerate(blocks):
        if i + ATTN_LOOKAHEAD < len(blocks):
            ahead.append(_attn_scores(q_ref, k_ref, v_ref, bias_ref, blocks[i + ATTN_LOOKAHEAD], head0))
        _attn_values(*ahead.pop(0), blk[0], pv_ref, den_ref, m_ref, head0)
    m_all = [m_ref[br] for br in range(nbr)]
    m_top = functools.reduce(jnp.maximum, m_all)
    num = jnp.zeros((seq, LANES), F32)
    den = jnp.zeros((seq, LANES), F32)
    for br in range(nbr):
        w = jnp.exp2(m_all[br] - m_top)
        num = num + w * pv_ref[br]
        den = den + w * pltpu.roll(den_ref[br], HEAD_DIM, axis=1)
    o_ref[...] = num / den


def _attention(q, k, v, bias, table):
    b, s, _ = q.shape
    pairs = D_ATTN // LANES
    tab_rows = table.shape[0] // (b * pairs)
    assert tab_rows * b * pairs == table.shape[0]
    blk = pl.BlockSpec((None, s, LANES), lambda i, p: (i, 0, p))
    tab = pl.BlockSpec((tab_rows, table.shape[1]), lambda i, p: (i * pairs + p, 0))
    nbr = len(DILATED_BRANCHES)
    return pl.pallas_call(
        functools.partial(_attn_kernel, seq=s),
        grid=(b, pairs),
        in_specs=[blk, blk, blk,
                  pl.BlockSpec((nbr, 2, BLK, 2 * BLK), lambda i, p: (0, p, 0, 0)),
                  tab],
        out_specs=[blk, tab],
        out_shape=[jax.ShapeDtypeStruct((b, s, D_ATTN), F32), jax.ShapeDtypeStruct(table.shape, BF16)],
        scratch_shapes=[pltpu.VMEM((nbr, s, LANES), F32)] * 3,
        compiler_params=_params(("arbitrary", "arbitrary")),
        name="attention",
    )(q, k, v, bias, table)


def _zoh_kernel(lr_ref, li_ref, dt_ref, lrr_ref, lir_ref, br_ref, bi_ref,
                are_ref, aim_ref, bbr_ref, bbi_ref):
    dt = jnp.exp(dt_ref[...])

    def zoh(lr, li):
        mag = jnp.exp(lr * dt)
        a_re, a_im = mag * jnp.cos(li * dt), mag * jnp.sin(li * dt)
        den = lr * lr + li * li
        f_re = ((a_re - 1.0) * lr + a_im * li) / den
        f_im = (a_im * lr - (a_re - 1.0) * li) / den
        return a_re, a_im, f_re, f_im

    a_re, a_im, _, _ = zoh(lr_ref[...], li_ref[...])
    are_ref[...] = a_re
    aim_ref[...] = a_im
    _, _, f_re, f_im = zoh(lrr_ref[...], lir_ref[...])
    br, bi = br_ref[...], bi_ref[...]
    bbr_ref[...] = f_re * br - f_im * bi
    bbi_ref[...] = f_re * bi + f_im * br


def _ssm_zoh(lam_re, lam_im, log_dt, b_re, b_im):
    g, n, c = b_re.shape
    rep = lambda a: jnp.repeat(a.astype(F32), c, axis=1)
    a_re, a_im, bb_re, bb_im = pl.pallas_call(
        _zoh_kernel,
        out_shape=[jax.ShapeDtypeStruct((g, n), F32)] * 2 + [jax.ShapeDtypeStruct((g, n * c), F32)] * 2,
        name="ssm_zoh",
    )(lam_re.astype(F32), lam_im.astype(F32), log_dt.astype(F32)[:, None], rep(lam_re), rep(lam_im),
      b_re.astype(F32).reshape(g, n * c), b_im.astype(F32).reshape(g, n * c))
    return a_re, a_im, bb_re.reshape(g, n, c), bb_im.reshape(g, n, c)


def _ssm_kernel(u_ref, wb_ref, wc_ref, are_ref, aim_ref, d_ref, gw_ref, gb_ref, gain_ref,
                o_ref, lhs_ref, bu_ref, ysel_ref, y_ref, sr_ref, si_ref, *, nb, chunk):
    half = N_SLABS // 2
    seqs = 2 * nb
    lane_blocks = 2 * SLAB_STATE // LANES
    rows_all = seqs * chunk

    @pl.when(pl.program_id(0) == 0)
    def _():
        sr_ref[...] = jnp.zeros_like(sr_ref)
        si_ref[...] = jnp.zeros_like(si_ref)
        lhs_ref[...] = jnp.zeros_like(lhs_ref)

    for b in range(nb):
        for m in range(N_SLABS):
            gh, mp = divmod(m, half)
            lhs_ref[2 * mp + gh, pl.ds(gh * nb + b, chunk, stride=seqs), :] = u_ref[b, :, LANES * m:LANES * (m + 1)]

    for mp in range(half):
        lhs = jnp.concatenate([lhs_ref[2 * mp], lhs_ref[2 * mp + 1]], axis=1).astype(BF16)
        bu = jnp.dot(lhs, wb_ref[mp], preferred_element_type=F32)
        for j in range(lane_blocks):
            bu_ref[lane_blocks * mp + j] = bu[:, LANES * j:LANES * (j + 1)]

    def load_state(rows, mp, part):
        j0 = lane_blocks * mp + part * (lane_blocks // 2)
        return jnp.concatenate([bu_ref[j0 + j, rows, :] for j in range(lane_blocks // 2)], axis=1)

    def store_state(rows, mp, part, val):
        j0 = lane_blocks * mp + part * (lane_blocks // 2)
        for j in range(lane_blocks // 2):
            bu_ref[j0 + j, rows, :] = val[:, LANES * j:LANES * (j + 1)]

    def step(t, carry):
        base = pl.multiple_of(t * seqs, seqs)
        rows = pl.ds(base, seqs)
        new = []
        for mp in range(half):
            xr, xi = carry[2 * mp], carry[2 * mp + 1]
            ar = are_ref[:, SLAB_STATE * mp:SLAB_STATE * (mp + 1)]
            ai = aim_ref[:, SLAB_STATE * mp:SLAB_STATE * (mp + 1)]
            nr = ar * xr - ai * xi + load_state(rows, mp, 0)
            ni = ar * xi + ai * xr + load_state(rows, mp, 1)
            store_state(rows, mp, 0, nr)
            store_state(rows, mp, 1, ni)
            new += [nr, ni]
        return tuple(new)

    init = []
    for mp in range(half):
        init += [sr_ref[:, SLAB_STATE * mp:SLAB_STATE * (mp + 1)], si_ref[:, SLAB_STATE * mp:SLAB_STATE * (mp + 1)]]
    final = lax.fori_loop(0, chunk, step, tuple(init), unroll=2)
    for mp in range(half):
        sr_ref[:, SLAB_STATE * mp:SLAB_STATE * (mp + 1)] = final[2 * mp]
        si_ref[:, SLAB_STATE * mp:SLAB_STATE * (mp + 1)] = final[2 * mp + 1]

    first_half = (lax.broadcasted_iota(jnp.int32, (rows_all, LANES), 0) & nb) == 0
    for mp in range(half):
        xs = jnp.concatenate([bu_ref[lane_blocks * mp + j] for j in range(lane_blocks)], axis=1).astype(BF16)
        yy = jnp.dot(xs, wc_ref[mp], preferred_element_type=F32)
        ysel_ref[mp] = jnp.where(first_half, yy[:, :LANES], yy[:, LANES:])
    for b in range(nb):
        for m in range(N_SLABS):
            gh, mp = divmod(m, half)
            cols = slice(LANES * m, LANES * (m + 1))
            y = ysel_ref[mp, pl.ds(gh * nb + b, chunk, stride=seqs), :]
            y_ref[b * chunk:(b + 1) * chunk, cols] = y + d_ref[:, cols] * u_ref[b, :, cols]

    y = _erf_gelu(y_ref[...])
    z = jnp.dot(y.astype(BF16), gw_ref[...], preferred_element_type=F32) + gb_ref[...]
    y = y * jax.nn.sigmoid(z)
    ms = jnp.mean(y * y, axis=-1, keepdims=True)
    yn = y * lax.rsqrt(ms + EPS) * gain_ref[...]
    for b in range(nb):
        o_ref[b] = yn[b * chunk:(b + 1) * chunk].astype(o_ref.dtype)


def _s5_mixer(u, lam_re, lam_im, log_dt, b_re, b_im, c_re, c_im, d_skip, glu_w, glu_b, out_gain):
    nb, s, _ = u.shape
    a_re, a_im, bb_re, bb_im = _ssm_zoh(lam_re, lam_im, log_dt, b_re, b_im)
    eye = jnp.eye(SSM_SLAB_GROUPS, dtype=F32)

    def in_slab(bb):
        w = jnp.einsum('mgnc,gh->mgchn', bb.reshape(N_SLABS, SSM_SLAB_GROUPS, SSM_STATE, SSM_GROUP), eye)
        return w.reshape(N_SLABS, LANES, SLAB_STATE)

    def out_slab(cc):
        w = jnp.einsum('mgcn,gh->mgnhc', cc.reshape(N_SLABS, SSM_SLAB_GROUPS, SSM_GROUP, SSM_STATE), eye)
        return w.reshape(N_SLABS, SLAB_STATE, LANES)

    wb = jnp.concatenate([in_slab(bb_re), in_slab(bb_im)], axis=2)
    wc = jnp.concatenate([out_slab(c_re.astype(F32)), -out_slab(c_im.astype(F32))], axis=1)
    hs = N_SLABS // 2
    wb = jnp.concatenate([wb[:hs], wb[hs:]], axis=1).astype(BF16)
    wc = jnp.concatenate([wc[:hs], wc[hs:]], axis=2).astype(BF16)
    half_states = (N_SSM_GROUPS // 2) * SSM_STATE

    def seq_rows(a):
        return jnp.repeat(a.reshape(2, half_states), nb, axis=0)

    fixed2 = lambda c: (0, 0)
    fixed3 = lambda c: (0, 0, 0)
    chunk = SSM_CHUNK
    return pl.pallas_call(
        functools.partial(_ssm_kernel, nb=nb, chunk=chunk),
        grid=(s // chunk,),
        in_specs=[pl.BlockSpec((nb, chunk, D_SSM), lambda c: (0, c, 0)),
                  pl.BlockSpec((N_SLABS // 2, 2 * LANES, 2 * SLAB_STATE), fixed3),
                  pl.BlockSpec((N_SLABS // 2, 2 * SLAB_STATE, 2 * LANES), fixed3),
                  pl.BlockSpec((2 * nb, half_states), fixed2),
                  pl.BlockSpec((2 * nb, half_states), fixed2),
                  pl.BlockSpec((1, D_SSM), fixed2),
                  pl.BlockSpec((D_SSM, D_SSM), fixed2),
                  pl.BlockSpec((1, D_SSM), fixed2),
                  pl.BlockSpec((1, D_SSM), fixed2)],
        out_specs=pl.BlockSpec((nb, chunk, D_SSM), lambda c: (0, c, 0)),
        out_shape=jax.ShapeDtypeStruct((nb, s, D_SSM), BF16),
        scratch_shapes=[pltpu.VMEM((N_SLABS, 2 * nb * chunk, LANES), F32),
                        pltpu.VMEM((2 * half_states // LANES, 2 * nb * chunk, LANES), F32),
                        pltpu.VMEM((N_SLABS // 2, 2 * nb * chunk, LANES), F32),
                        pltpu.VMEM((nb * chunk, D_SSM), F32),
                        pltpu.VMEM((2 * nb, half_states), F32),
                        pltpu.VMEM((2 * nb, half_states), F32)],
        compiler_params=_params(("arbitrary",)),
        name="ssm",
    )(u, wb, wc, seq_rows(a_re), seq_rows(a_im), d_skip.astype(F32).reshape(1, D_SSM),
      glu_w.astype(BF16), glu_b.astype(F32)[None, :], out_gain.astype(F32)[None, :])


def _outproj_kernel(a_ref, s_ref, x_ref, ag_ref, wa_ref, ws_ref, fg_ref, x1_ref, hn_ref):
    a = a_ref[...]
    ms = jnp.mean(a * a, axis=-1, keepdims=True)
    an = (a * lax.rsqrt(ms + EPS) * ag_ref[...]).astype(BF16)
    mixed = (jnp.dot(an, wa_ref[...], preferred_element_type=F32)
             + jnp.dot(s_ref[...], ws_ref[...], preferred_element_type=F32))
    x1 = x_ref[...] + mixed
    x1_ref[...] = x1
    ms1 = jnp.mean(x1 * x1, axis=-1, keepdims=True)
    hn_ref[...] = (x1 * lax.rsqrt(ms1 + EPS) * fg_ref[...]).astype(BF16)


def _out_proj(attn, ssm_n, x2, attn_g, w_out, ffn_g):
    t = x2.shape[0]
    row = lambda i: (i, 0)
    fixed = lambda i: (0, 0)
    w = w_out.astype(BF16)
    return pl.pallas_call(
        _outproj_kernel,
        grid=(t // ROW_TILE,),
        in_specs=[pl.BlockSpec((ROW_TILE, D_ATTN), row),
                  pl.BlockSpec((ROW_TILE, D_SSM), row),
                  pl.BlockSpec((ROW_TILE, D_MODEL), row),
                  pl.BlockSpec((1, D_ATTN), fixed),
                  pl.BlockSpec((D_ATTN, D_MODEL), fixed),
                  pl.BlockSpec((D_SSM, D_MODEL), fixed),
                  pl.BlockSpec((1, D_MODEL), fixed)],
        out_specs=[pl.BlockSpec((ROW_TILE, D_MODEL), row)] * 2,
        out_shape=[jax.ShapeDtypeStruct((t, D_MODEL), F32), jax.ShapeDtypeStruct((t, D_MODEL), BF16)],
        compiler_params=_params(("arbitrary",)),
        name="out_proj",
    )(attn, ssm_n, x2, attn_g.astype(F32)[None, :], w[:D_ATTN], w[D_ATTN:], ffn_g.astype(F32)[None, :])


def _take16(s, exact, index=None):
    if index is None:
        index = lax.broadcasted_iota(jnp.int32, s.shape, 0)
    vals = []
    for it in range(PEER_TOPK):
        m = jnp.max(s, axis=0, keepdims=True)
        if exact:
            first = jnp.min(jnp.where(s == m, index, jnp.iinfo(jnp.int32).max), axis=0, keepdims=True)
            hit = index == first
        else:
            hit = s == m
        s = jnp.where(hit, RANK_BASE * (1.0 + it / 32.0), s)
        vals.append(m)
    return vals, s


def _taken(marked):
    return jnp.logical_and(marked < 0.5 * RANK_BASE, marked > 2.0 * RANK_BASE)


def _rank_of(marked):
    rank = jnp.floor((marked * (1.0 / RANK_BASE) - 1.0) * 32.0 + 0.5)
    return jnp.where(_taken(marked), rank, NO_RANK)


def _stack_rows(rows, n):
    iota = lax.broadcasted_iota(jnp.int32, (n, rows[0].shape[1]), 0)
    out = jnp.zeros((n, rows[0].shape[1]), F32)
    for i, r in enumerate(rows):
        out = jnp.where(iota == i, r, out)
    return out


def _count(mask):
    return jnp.sum(jnp.where(mask, 1.0, 0.0), axis=0, keepdims=True)


def _route_head(s1, s2, exact):
    tb = s1.shape[1]
    v1, marked1 = _take16(s1, exact)
    v2, marked2 = _take16(s2, exact)
    rank1, rank2 = _rank_of(marked1), _rank_of(marked2)
    v2_all = _stack_rows(v2, PEER_TOPK)
    v1_hi = _stack_rows(v1[SUBLANES:], SUBLANES)
    sub = lax.broadcasted_iota(jnp.int32, (SUBLANES, tb), 0)
    pieces = [v1[0] + v2_all]
    flats = [lax.broadcasted_iota(jnp.int32, (PEER_TOPK, tb), 0)]
    for a in range(1, SUBLANES):
        limit = PEER_TOPK // (a + 1)
        pieces.append(jnp.where(sub < limit, v1[a] + v2_all[:SUBLANES], -jnp.inf))
        flats.append(a * PEER_TOPK + sub)
    pieces.append(v1_hi + v2[0])
    flats.append((sub + SUBLANES) * PEER_TOPK)
    cand = jnp.concatenate(pieces, axis=0)
    flat = jnp.concatenate(flats, axis=0)
    _, marked = _take16(cand, exact, flat)
    sel = jnp.where(_taken(marked), 1.0, 0.0)
    top = v1[0] + v2[0]
    z = jnp.sum(sel * jnp.exp(cand - top), axis=0, keepdims=True)
    cnt = [jnp.sum(sel[:PEER_TOPK], axis=0, keepdims=True)]
    for a in range(1, SUBLANES):
        lo = PEER_TOPK + SUBLANES * (a - 1)
        cnt.append(jnp.sum(sel[lo:lo + SUBLANES], axis=0, keepdims=True))
    lo = PEER_TOPK + SUBLANES * (SUBLANES - 1)
    for i in range(SUBLANES):
        cnt.append(sel[lo + i:lo + i + 1])
    c1 = jnp.zeros(rank1.shape, F32)
    for a in range(PEER_TOPK):
        c1 = jnp.where(rank1 == float(a), cnt[a], c1)
    e1 = jnp.exp(s1 - v1[0])
    e2n = jnp.exp(s2 - v2[0]) / z
    taken = jnp.maximum(jnp.maximum(_count(_taken(marked1)), _count(_taken(marked2))), jnp.sum(sel, axis=0, keepdims=True))
    return c1, e1, rank2, e2n, taken


def _route_one(h, qt_ref, k1_ref, k2_ref, c1_ref, e1_ref, r2_ref, e2_ref, exact):
    half = PEER_QDIM // 2
    q1 = qt_ref[PEER_QDIM * h:PEER_QDIM * h + half, :].astype(BF16)
    q2 = qt_ref[PEER_QDIM * h + half:PEER_QDIM * (h + 1), :].astype(BF16)
    s1 = jnp.dot(k1_ref[h], q1, preferred_element_type=F32)
    s2 = jnp.dot(k2_ref[h], q2, preferred_element_type=F32)
    c1, e1, r2, e2n, taken = _route_head(s1, s2, exact)
    c1_ref[h] = c1
    e1_ref[h] = e1
    r2_ref[h] = r2.astype(r2_ref.dtype)
    e2_ref[h] = e2n.astype(e2_ref.dtype)
    return taken


def _route_kernel(hn_ref, wq_ref, k1_ref, k2_ref, c1_ref, e1_ref, r2_ref, e2_ref, qt_ref):
    qt_ref[...] = lax.dot_general(wq_ref[...], hn_ref[...], (((1,), (1,)), ((), ())), preferred_element_type=F32)
    refs = (qt_ref, k1_ref, k2_ref, c1_ref, e1_ref, r2_ref, e2_ref)
    taken = [_route_one(h, *refs, exact=False) for h in range(PEER_HEADS)]
    for h in range(PEER_HEADS):
        @pl.when(jnp.max(taken[h]) > float(PEER_TOPK))
        def _():
            _route_one(h, *refs, exact=True)


def _peer_route(hn, w_q, keys1, keys2):
    t = hn.shape[0]
    blk = pl.BlockSpec((PEER_HEADS, PEER_KEYS, ROUTE_TILE), lambda i: (0, 0, i))
    fixed3 = lambda i: (0, 0, 0)
    shape = (PEER_HEADS, PEER_KEYS, t)
    return pl.pallas_call(
        _route_kernel,
        grid=(t // ROUTE_TILE,),
        in_specs=[pl.BlockSpec((ROUTE_TILE, D_MODEL), lambda i: (i, 0)),
                  pl.BlockSpec((PEER_HEADS * PEER_QDIM, D_MODEL), lambda i: (0, 0)),
                  pl.BlockSpec((PEER_HEADS, PEER_KEYS, PEER_QDIM // 2), fixed3),
                  pl.BlockSpec((PEER_HEADS, PEER_KEYS, PEER_QDIM // 2), fixed3)],
        out_specs=[blk] * 4,
        out_shape=[jax.ShapeDtypeStruct(shape, F32), jax.ShapeDtypeStruct(shape, F32),
                   jax.ShapeDtypeStruct(shape, BF16), jax.ShapeDtypeStruct(shape, BF16)],
        scratch_shapes=[pltpu.VMEM((PEER_HEADS * PEER_QDIM, ROUTE_TILE), F32)],
        compiler_params=_params(("arbitrary",)),
        name="peer_route",
    )(hn, w_q.T.astype(BF16), keys1.astype(BF16), keys2.astype(BF16))


def _peer_gated(c1_ref, e1_ref, r2_ref, e2_ref, at_ref, chunk, valid):
    packed = 2 * SUBLANES
    tiles = PEER_SUB // PEER_KEYS
    gs = []
    for tl in range(tiles):
        tile = chunk * tiles + tl
        gate = jnp.zeros((PEER_KEYS // packed, packed, PEER_TOKENS), BF16)
        for h in range(PEER_HEADS):
            c1 = jnp.broadcast_to(c1_ref[h, pl.ds(tile, 1), :], (packed, PEER_TOKENS)).astype(BF16)
            e1 = jnp.broadcast_to(e1_ref[h, pl.ds(tile, 1), :], (packed, PEER_TOKENS)).astype(BF16)
            r2 = r2_ref[h].reshape(gate.shape)
            e2 = e2_ref[h].reshape(gate.shape)
            gate = gate + jnp.where(r2 < c1[None], e1[None] * e2, jnp.zeros_like(e2))
        a = at_ref[PEER_KEYS * tl:PEER_KEYS * (tl + 1), :].astype(BF16)
        gs.append(gate.reshape(PEER_KEYS, PEER_TOKENS) * _erf_gelu(a))
    g = jnp.concatenate(gs, axis=0)
    return g if valid is True else jnp.where(valid, g, jnp.zeros_like(g))


def _peer_kernel(hn_ref, u_ref, *refs, n_chunks):
    n = PEER_EXPERTS_STEP // PEER_SUB
    vt_refs, vt_tail_ref, route = refs[:n], refs[n], refs[n + 1:n + 5]
    x1_ref, o_ref, acc_ref, at_ref = refs[n + 5:]
    e = pl.program_id(1)
    last = pl.num_programs(1) - 1
    nt = (((1,), (1,)), ((), ()))

    @pl.when(e == 0)
    def _():
        acc_ref[...] = jnp.zeros_like(acc_ref)
        at_ref[n - 1] = jnp.zeros((PEER_SUB, PEER_TOKENS), F32)

    hn = hn_ref[...]
    total = None
    for c in range(n):
        at_ref[c] = lax.dot_general(u_ref[PEER_SUB * c:PEER_SUB * (c + 1), :], hn, nt, preferred_element_type=F32)
        prev = n * e + c - 1
        g = _peer_gated(*route, at_ref.at[(c - 1) % n], jnp.maximum(prev, 0), (e > 0) if c == 0 else True)
        part = jnp.dot(vt_refs[c][...], g, preferred_element_type=F32)
        total = part if total is None else total + part
    acc_ref[...] += total

    @pl.when(e == last)
    def _():
        g = _peer_gated(*route, at_ref.at[n - 1], n_chunks - 1, True)
        tail = jnp.dot(vt_tail_ref[...], g, preferred_element_type=F32)
        o_ref[...] = x1_ref[...] + (acc_ref[...] + tail).T


def _peer_mix(hn, x1, u_bf, vt, c1, e1, r2, e2n):
    t = hn.shape[0]
    n = PEER_EXPERTS_STEP // PEER_SUB
    n_steps = u_bf.shape[0] // PEER_EXPERTS_STEP
    n_chunks = n * n_steps
    route = pl.BlockSpec((PEER_HEADS, PEER_KEYS, PEER_TOKENS), lambda i, e: (0, 0, i))
    tok = lambda i, e: (i, 0)
    vt_specs = [pl.BlockSpec((D_MODEL, PEER_SUB),
                             functools.partial(lambda i, e, c: (0, jnp.maximum(n * e + c - 1, 0)), c=c))
                for c in range(n)]
    vt_specs.append(pl.BlockSpec((D_MODEL, PEER_SUB), lambda i, e: (0, n_chunks - 1)))
    return pl.pallas_call(
        functools.partial(_peer_kernel, n_chunks=n_chunks),
        grid=(t // PEER_TOKENS, n_steps),
        in_specs=[pl.BlockSpec((PEER_TOKENS, D_MODEL), tok),
                  pl.BlockSpec((PEER_EXPERTS_STEP, D_MODEL), lambda i, e: (e, 0)),
                  *vt_specs,
                  route, route, route, route,
                  pl.BlockSpec((PEER_TOKENS, D_MODEL), tok)],
        out_specs=pl.BlockSpec((PEER_TOKENS, D_MODEL), tok),
        out_shape=jax.ShapeDtypeStruct((t, D_MODEL), F32),
        scratch_shapes=[pltpu.VMEM((D_MODEL, PEER_TOKENS), F32),
                        pltpu.VMEM((n, PEER_SUB, PEER_TOKENS), F32)],
        compiler_params=_params(("arbitrary", "arbitrary")),
        name="peer_mix",
    )(hn, u_bf, *([vt] * (n + 1)), c1, e1, r2, e2n, x1)


def kernel(x, norm_mix_g, w_in, q_norm_g, k_norm_g, rel_bias, ssm_lambda_re, ssm_lambda_im, ssm_log_dt,
           ssm_b_re, ssm_b_im, ssm_c_re, ssm_c_im, ssm_d, ssm_glu_w, ssm_glu_b, attn_out_g, ssm_out_g,
           w_out, norm_ffn_g, peer_w_q, peer_keys1, peer_keys2, peer_u, peer_v):
    b, s, d = x.shape
    x2 = x.reshape(b * s, d)
    bias, w_in_bf = _bias_tables(rel_bias, w_in)
    q, k, v, u, peer_vt = _in_proj(x2, norm_mix_g, w_in_bf, q_norm_g, k_norm_g, peer_v)
    attn, peer_u_bf = _attention(q.reshape(b, s, D_ATTN), k.reshape(b, s, D_ATTN), v.reshape(b, s, D_ATTN), bias,
                                 peer_u)
    ssm_n = _s5_mixer(u.reshape(b, s, D_SSM), ssm_lambda_re, ssm_lambda_im, ssm_log_dt, ssm_b_re, ssm_b_im,
                      ssm_c_re, ssm_c_im, ssm_d, ssm_glu_w, ssm_glu_b, ssm_out_g)
    x1, hn = _out_proj(attn.reshape(b * s, D_ATTN), ssm_n.reshape(b * s, D_SSM), x2, attn_out_g, w_out, norm_ffn_g)
    c1, e1, r2, e2n = _peer_route(hn, peer_w_q, peer_keys1, peer_keys2)
    out = _peer_mix(hn, x1, peer_u_bf, peer_vt, c1, e1, r2, e2n)
    return out.reshape(b, s, d).astype(x.dtype)
```
